```python
import math
import jax, jax.numpy as jnp
from jax import lax
import numpy as np

D_MODEL = 1024
BATCH = 8
SEQ = 2048
DEPTH = 2
DEC_BATCH = 32
DEC_SEQ = 1
PAST_LEN = 16384
PAGE_SIZE = 128

N_MIXERS = 2
N_LAYERS_A = (DEPTH + 1) // 2
N_LAYERS_B = DEPTH // 2
A_HEADS = 16
A_KV_HEADS = 4
A_HEAD_DIM = 64
A_GROUP = A_HEADS // A_KV_HEADS
WINDOW = 128
BAND_BLOCK = WINDOW
N_BUCKETS = 32
MAX_EXACT = N_BUCKETS // 2
MAX_DISTANCE = 128
B_HEADS = 16
Q_LORA = 768
KV_LORA = 256
NOPE_DIM = 64
ROPE_DIM = 32
V_DIM = 64
ROPE_THETA = 10000.0
Q_BLOCK = 128
D_FF = 4 * D_MODEL
EPS = 1e-6

kernel_name = 'hybrid_swa_sink_mla_decoder_step'


def rms_norm(x, g):
    xf = x.astype(jnp.float32)
    y = xf * lax.rsqrt(jnp.mean(xf * xf, axis=-1, keepdims=True) + EPS)
    return (y * g.astype(jnp.float32)).astype(x.dtype)


def t5_bucket(dist):
    d = jnp.maximum(dist, 0)
    df = jnp.maximum(d, 1).astype(jnp.float32)
    large = MAX_EXACT + (jnp.log(df / MAX_EXACT) / math.log(MAX_DISTANCE / MAX_EXACT)
                         * (N_BUCKETS - MAX_EXACT)).astype(jnp.int32)
    large = jnp.minimum(large, N_BUCKETS - 1)
    return jnp.where(d < MAX_EXACT, d, large)


def t5_bias(rel_bias, dist):
    b = rel_bias[t5_bucket(dist)].astype(jnp.float32)
    return jnp.transpose(b, (2, 0, 1)).reshape(A_KV_HEADS, A_GROUP, dist.shape[0], dist.shape[1])


def softmax_with_sink(s, sink, mask):
    s = jnp.where(mask, s, -jnp.inf)
    m = jnp.maximum(jnp.max(s, axis=-1, keepdims=True), sink)
    e = jnp.exp(s - m)
    return e / (jnp.sum(e, axis=-1, keepdims=True) + jnp.exp(sink - m))


def rope_angles(pos):
    inv = ROPE_THETA ** (-jnp.arange(0, ROPE_DIM, 2, dtype=jnp.float32) / ROPE_DIM)
    ang = pos[:, None] * inv[None, :]
    return jnp.cos(ang), jnp.sin(ang)


def apply_rope(x, cos, sin):
    xf = x.astype(jnp.float32)
    x1, x2 = xf[..., :ROPE_DIM // 2], xf[..., ROPE_DIM // 2:]
    return jnp.concatenate([x1 * cos - x2 * sin, x2 * cos + x1 * sin], axis=-1).astype(x.dtype)


def split_qkv_a(h, w_qkv):
    qkv = h @ w_qkv
    nq, nk = A_HEADS * A_HEAD_DIM, A_KV_HEADS * A_HEAD_DIM
    return qkv[..., :nq], qkv[..., nq:nq + nk], qkv[..., nq + nk:]


def window_attn_prompt(h, w_qkv, w_o, sinks, rel_bias):
    B, S, _ = h.shape
    nb = S // BAND_BLOCK
    q, k, v = split_qkv_a(h, w_qkv)
    q = q.reshape(B, nb, BAND_BLOCK, A_KV_HEADS, A_GROUP, A_HEAD_DIM)
    k = k.reshape(B, S, A_KV_HEADS, A_HEAD_DIM)
    v = v.reshape(B, S, A_KV_HEADS, A_HEAD_DIM)
    pad = jnp.zeros((B, BAND_BLOCK, A_KV_HEADS, A_HEAD_DIM), k.dtype)

    def band(t):
        tp = jnp.concatenate([pad, t], axis=1)
        prev = tp[:, :S].reshape(B, nb, BAND_BLOCK, A_KV_HEADS, A_HEAD_DIM)
        cur = tp[:, BAND_BLOCK:].reshape(B, nb, BAND_BLOCK, A_KV_HEADS, A_HEAD_DIM)
        return jnp.concatenate([prev, cur], axis=2)

    kb, vb = band(k), band(v)
    s = jnp.einsum('bnqkgd,bnjkd->bnkgqj', q, kb).astype(jnp.float32) * (A_HEAD_DIM ** -0.5)
    dist = jnp.arange(BAND_BLOCK)[:, None] + BAND_BLOCK - jnp.arange(2 * BAND_BLOCK)[None, :]
    valid = (dist >= 0) & (dist < WINDOW)
    real = (jnp.arange(nb)[:, None, None] > 0) | (jnp.arange(2 * BAND_BLOCK)[None, None, :] >= BAND_BLOCK)
    mask = (valid[None] & real)[None, :, None, None]
    sink = sinks.astype(jnp.float32).reshape(A_KV_HEADS, A_GROUP, 1, 1)
    p = softmax_with_sink(s + t5_bias(rel_bias, dist), sink, mask)
    o = jnp.einsum('bnkgqj,bnjkd->bnqkgd', p.astype(vb.dtype), vb).reshape(B, S, A_HEADS * A_HEAD_DIM)
    wbp = min(WINDOW, S)
    return o @ w_o, k[:, S - wbp:], v[:, S - wbp:]


def window_attn_sample(h, buf_k, buf_v, w_qkv, w_o, sinks, rel_bias):
    DB, T, _ = h.shape
    wb = buf_k.shape[1]
    q, k, v = split_qkv_a(h, w_qkv)
    q = q.reshape(DB, T, A_KV_HEADS, A_GROUP, A_HEAD_DIM)
    k_all = jnp.concatenate([buf_k, k.reshape(DB, T, A_KV_HEADS, A_HEAD_DIM).astype(buf_k.dtype)], axis=1)
    v_all = jnp.concatenate([buf_v, v.reshape(DB, T, A_KV_HEADS, A_HEAD_DIM).astype(buf_v.dtype)], axis=1)
    s = jnp.einsum('btkgd,bjkd->bkgtj', q, k_all).astype(jnp.float32) * (A_HEAD_DIM ** -0.5)
    dist = jnp.arange(T)[:, None] - (jnp.arange(wb + T)[None, :] - wb)
    mask = (dist >= 0) & (dist < WINDOW)
    sink = sinks.astype(jnp.float32).reshape(A_KV_HEADS, A_GROUP, 1, 1)
    p = softmax_with_sink(s + t5_bias(rel_bias, dist), sink, mask)
    o = jnp.einsum('bkgtj,bjkd->btkgd', p.astype(v_all.dtype), v_all).reshape(DB, T, A_HEADS * A_HEAD_DIM)
    return o @ w_o, k_all[:, -wb:], v_all[:, -wb:]


def mla_project(h, pos, w_in, q_norm, kv_norm, w_q_b):
    B, S, _ = h.shape
    c = h @ w_in
    c_q, c_kv, k_r = c[..., :Q_LORA], c[..., Q_LORA:Q_LORA + KV_LORA], c[..., Q_LORA + KV_LORA:]
    q = (rms_norm(c_q, q_norm) @ w_q_b).reshape(B, S, B_HEADS, NOPE_DIM + ROPE_DIM)
    cos, sin = rope_angles(pos)
    q_nope = q[..., :NOPE_DIM]
    q_rope = apply_rope(q[..., NOPE_DIM:], cos[:, None, :], sin[:, None, :])
    k_r = apply_rope(k_r, cos, sin)
    latent = rms_norm(c_kv, kv_norm)
    return q_nope, q_rope, latent, k_r


def mla_prompt(h, w_in, q_norm, kv_norm, w_q_b, w_kv_b, w_o):
    B, S, _ = h.shape
    nb = S // Q_BLOCK
    q_nope, q_rope, latent, k_r = mla_project(h, jnp.arange(S, dtype=jnp.float32), w_in, q_norm, kv_norm, w_q_b)
    kv = jnp.einsum('bsr,rhe->bshe', latent, w_kv_b)
    k_nope, v = kv[..., :NOPE_DIM], kv[..., NOPE_DIM:]
    qn = q_nope.reshape(B, nb, Q_BLOCK, B_HEADS, NOPE_DIM).transpose(1, 0, 2, 3, 4)
    qr = q_rope.reshape(B, nb, Q_BLOCK, B_HEADS, ROPE_DIM).transpose(1, 0, 2, 3, 4)
    kpos = jnp.arange(S)
    scale = (NOPE_DIM + ROPE_DIM) ** -0.5

    def block(args):
        qn_b, qr_b, n = args
        s = (jnp.einsum('bqhd,bkhd->bhqk', qn_b, k_nope)
             + jnp.einsum('bqhe,bke->bhqk', qr_b, k_r)).astype(jnp.float32) * scale
        qpos = n * Q_BLOCK + jnp.arange(Q_BLOCK)
        mask = kpos[None, :] <= qpos[:, None]
        p = jax.nn.softmax(jnp.where(mask, s, -jnp.inf), axis=-1)
        return jnp.einsum('bhqk,bkhd->bqhd', p.astype(v.dtype), v)

    o = lax.map(block, (qn, qr, jnp.arange(nb)))
    o = o.transpose(1, 0, 2, 3, 4).reshape(B, S, B_HEADS * V_DIM)
    return o @ w_o, latent, k_r


def mla_sample(h, lat_pool, kr_pool, page_table, w_in, q_norm, kv_norm, w_q_b, w_kv_b, w_o):
    DB, T, _ = h.shape
    past = page_table.shape[1] * lat_pool.shape[1]
    pos = (past + jnp.arange(T)).astype(jnp.float32)
    q_nope, q_rope, latent, k_r = mla_project(h, pos, w_in, q_norm, kv_norm, w_q_b)
    lat_all = jnp.concatenate([lat_pool[page_table].reshape(DB, past, KV_LORA), latent.astype(lat_pool.dtype)], axis=1)
    kr_all = jnp.concatenate([kr_pool[page_table].reshape(DB, past, ROPE_DIM), k_r.astype(kr_pool.dtype)], axis=1)
    w_uk, w_uv = w_kv_b[..., :NOPE_DIM], w_kv_b[..., NOPE_DIM:]
    q_lat = jnp.einsum('bthd,rhd->bthr', q_nope, w_uk)
    scale = (NOPE_DIM + ROPE_DIM) ** -0.5
    s = (jnp.einsum('bthr,bkr->bhtk', q_lat, lat_all)
         + jnp.einsum('bthe,bke->bhtk', q_rope, kr_all)).astype(jnp.float32) * scale
    mask = jnp.arange(past + T)[None, :] <= past + jnp.arange(T)[:, None]
    p = jax.nn.softmax(jnp.where(mask, s, -jnp.inf), axis=-1)
    o_lat = jnp.einsum('bhtk,bkr->bthr', p.astype(lat_all.dtype), lat_all)
    o = jnp.einsum('bthr,rhv->bthv', o_lat, w_uv).reshape(DB, T, B_HEADS * V_DIM)
    return o @ w_o, latent, k_r


def sqrelu_mlp(h, w_up, w_down):
    return jnp.square(jax.nn.relu(h @ w_up)) @ w_down


def setup_inputs(seed: int = 0) -> dict:
    key = jax.random.key(seed)
    ks = jax.random.split(key, 24)
    n_pages = PAST_LEN // PAGE_SIZE
    n_used = DEC_BATCH * n_pages
    n_phys = n_used + max(1, n_used // 4)
    wb = min(WINDOW, PAST_LEN)
    f32 = jnp.float32

    def w(k, shape, fan_in):
        return jax.random.normal(k, shape, f32) * (fan_in ** -0.5)

    def gain(k, shape):
        return 1.0 + 0.02 * jax.random.normal(k, shape, f32)

    return {
        'x_prompt': jax.random.normal(ks[0], (BATCH, SEQ, D_MODEL), f32),
        'x_sample': jax.random.normal(ks[1], (DEC_BATCH, DEC_SEQ, D_MODEL), f32),
        'cache_a_k': jax.random.normal(ks[2], (N_LAYERS_A, DEC_BATCH, wb, A_KV_HEADS, A_HEAD_DIM), f32),
        'cache_a_v': jax.random.normal(ks[3], (N_LAYERS_A, DEC_BATCH, wb, A_KV_HEADS, A_HEAD_DIM), f32),
        'cache_b_latent': jax.random.normal(ks[4], (N_LAYERS_B, n_phys, PAGE_SIZE, KV_LORA), f32),
        'cache_b_krope': jax.random.normal(ks[5], (N_LAYERS_B, n_phys, PAGE_SIZE, ROPE_DIM), f32),
        'page_table': jax.random.permutation(ks[6], n_phys)[:n_used].reshape(DEC_BATCH, n_pages).astype(jnp.int32),
        'rel_bias': 0.5 * jax.random.normal(ks[7], (N_BUCKETS, A_HEADS), f32),
        'norm_mix': gain(ks[8], (DEPTH, D_MODEL)),
        'norm_mlp': gain(ks[9], (DEPTH, D_MODEL)),
        'norm_final': gain(ks[10], (D_MODEL,)),
        'a_w_qkv': w(ks[11], (N_LAYERS_A, D_MODEL, (A_HEADS + 2 * A_KV_HEADS) * A_HEAD_DIM), D_MODEL),
        'a_w_o': w(ks[12], (N_LAYERS_A, A_HEADS * A_HEAD_DIM, D_MODEL), A_HEADS * A_HEAD_DIM),
        'a_sinks': 0.5 * jax.random.normal(ks[13], (N_LAYERS_A, A_HEADS), f32),
        'b_w_in': w(ks[14], (N_LAYERS_B, D_MODEL, Q_LORA + KV_LORA + ROPE_DIM), D_MODEL),
        'b_q_norm': gain(ks[15], (N_LAYERS_B, Q_LORA)),
        'b_kv_norm': gain(ks[16], (N_LAYERS_B, KV_LORA)),
        'b_w_q_b': w(ks[17], (N_LAYERS_B, Q_LORA, B_HEADS * (NOPE_DIM + ROPE_DIM)), Q_LORA),
        'b_w_kv_b': w(ks[18], (N_LAYERS_B, KV_LORA, B_HEADS, NOPE_DIM + V_DIM), KV_LORA),
        'b_w_o': w(ks[19], (N_LAYERS_B, B_HEADS * V_DIM, D_MODEL), B_HEADS * V_DIM),
        'mlp_w_up': w(ks[20], (DEPTH, D_MODEL, D_FF), D_MODEL),
        'mlp_w_down': w(ks[21], (DEPTH, D_FF, D_MODEL), D_FF),
    }


def reference(x_prompt, x_sample, cache_a_k, cache_a_v, cache_b_latent, cache_b_krope, page_table,
              rel_bias, norm_mix, norm_mlp, norm_final, a_w_qkv, a_w_o, a_sinks,
              b_w_in, b_q_norm, b_kv_norm, b_w_q_b, b_w_kv_b, b_w_o, mlp_w_up, mlp_w_down):
    xp, xs = x_prompt, x_sample
    a_k_p, a_v_p, a_k_s, a_v_s = [], [], [], []
    b_lat_p, b_kr_p, b_lat_s, b_kr_s = [], [], [], []
    for i in range(DEPTH):
        hp = rms_norm(xp, norm_mix[i])
        hs = rms_norm(xs, norm_mix[i])
        j = i // N_MIXERS
        if i % N_MIXERS == 0:
            yp, kp_, vp_ = window_attn_prompt(hp, a_w_qkv[j], a_w_o[j], a_sinks[j], rel_bias)
            ys, ks_, vs_ = window_attn_sample(hs, cache_a_k[j], cache_a_v[j], a_w_qkv[j], a_w_o[j], a_sinks[j], rel_bias)
            a_k_p.append(kp_); a_v_p.append(vp_); a_k_s.append(ks_); a_v_s.append(vs_)
        else:
            yp, lp_, rp_ = mla_prompt(hp, b_w_in[j], b_q_norm[j], b_kv_norm[j], b_w_q_b[j], b_w_kv_b[j], b_w_o[j])
            ys, ls_, rs_ = mla_sample(hs, cache_b_latent[j], cache_b_krope[j], page_table,
                                      b_w_in[j], b_q_norm[j], b_kv_norm[j], b_w_q_b[j], b_w_kv_b[j], b_w_o[j])
            b_lat_p.append(lp_); b_kr_p.append(rp_); b_lat_s.append(ls_); b_kr_s.append(rs_)
        xp = xp + yp
        xs = xs + ys
        xp = xp + sqrelu_mlp(rms_norm(xp, norm_mlp[i]), mlp_w_up[i], mlp_w_down[i])
        xs = xs + sqrelu_mlp(rms_norm(xs, norm_mlp[i]), mlp_w_up[i], mlp_w_down[i])
    y_prompt = rms_norm(xp, norm_final)
    y_sample = rms_norm(xs, norm_final)
    return (y_prompt, y_sample,
            jnp.stack(a_k_p), jnp.stack(a_v_p), jnp.stack(b_lat_p), jnp.stack(b_kr_p),
            jnp.stack(a_k_s), jnp.stack(a_v_s), jnp.stack(b_lat_s), jnp.stack(b_kr_s))
```

```python
import functools
import math

import jax
import jax.numpy as jnp
from jax import lax
from jax.experimental import pallas as pl
from jax.experimental.pallas import tpu as pltpu

F32 = jnp.float32
BF16 = jnp.bfloat16

D_MODEL = 1024
A_HEADS = 16
A_KV_HEADS = 4
A_HEAD_DIM = 64
A_GROUP = A_HEADS // A_KV_HEADS
WINDOW = 128
N_BUCKETS = 32
MAX_EXACT = N_BUCKETS // 2
MAX_DISTANCE = 128
B_HEADS = 16
Q_LORA = 768
KV_LORA = 256
NOPE_DIM = 64
ROPE_DIM = 32
HALF_ROPE = ROPE_DIM // 2
V_DIM = 64
ROPE_THETA = 10000.0
D_FF = 4 * D_MODEL
EPS = 1e-6

LANES = 128
VMEM_LIMIT_BYTES = 56 * 1024 * 1024

NEG = -1e30

NT_DIMS = (((1,), (1,)), ((), ()))


def _rms(x, g):
    return x * lax.rsqrt(jnp.mean(x * x, axis=-1, keepdims=True) + EPS) * g


def _params(n_axes):
    return pltpu.CompilerParams(
        dimension_semantics=("arbitrary",) * n_axes,
        vmem_limit_bytes=VMEM_LIMIT_BYTES,
    )


def _resident():
    return pl.BlockSpec(memory_space=pltpu.VMEM)


NQ_A = A_HEADS * A_HEAD_DIM
NKV_A = A_KV_HEADS * A_HEAD_DIM


def _qkv_kernel(x_ref, g_ref, w_ref, q_ref, k_ref, v_ref):
    h = _rms(x_ref[...], g_ref[...]).astype(BF16)
    qkv = jnp.dot(h, w_ref[...], preferred_element_type=F32)
    q_ref[...] = (qkv[:, :NQ_A] * (A_HEAD_DIM ** -0.5)).astype(BF16)
    k_ref[...] = qkv[:, NQ_A:NQ_A + NKV_A]
    v_ref[...] = qkv[:, NQ_A + NKV_A:]


def _norm_qkv(x, g, w_bf16, tm):
    m = x.shape[0]
    return pl.pallas_call(
        _qkv_kernel,
        grid=(m // tm,),
        in_specs=[
            pl.BlockSpec((tm, D_MODEL), lambda i: (i, 0)),
            _resident(),
            _resident(),
        ],
        out_specs=[
            pl.BlockSpec((tm, NQ_A), lambda i: (i, 0)),
            pl.BlockSpec((tm, NKV_A), lambda i: (i, 0)),
            pl.BlockSpec((tm, NKV_A), lambda i: (i, 0)),
        ],
        out_shape=[
            jax.ShapeDtypeStruct((m, NQ_A), BF16),
            jax.ShapeDtypeStruct((m, NKV_A), F32),
            jax.ShapeDtypeStruct((m, NKV_A), F32),
        ],
        compiler_params=_params(1),
        name="norm_qkv",
    )(x, g, w_bf16)


def _t5_bucket(dist):
    d = jnp.maximum(dist, 0)
    df = jnp.maximum(d, 1).astype(F32)
    large = MAX_EXACT + (jnp.log(df / MAX_EXACT) / math.log(MAX_DISTANCE / MAX_EXACT)
                         * (N_BUCKETS - MAX_EXACT)).astype(jnp.int32)
    large = jnp.minimum(large, N_BUCKETS - 1)
    return jnp.where(d < MAX_EXACT, d, large)


def _bias_kernel(bucket_ref, rb_ref, out_ref):
    bucket = bucket_ref[...]
    for h in range(A_HEADS):
        acc = jnp.full(bucket.shape, NEG, F32)
        for b in range(N_BUCKETS):
            acc = jnp.where(bucket == b, rb_ref[b, h], acc)
        out_ref[h] = acc


def _bias_table(rel_bias):
    qi = jnp.arange(WINDOW, dtype=jnp.int32)[:, None]
    kj = jnp.arange(2 * WINDOW, dtype=jnp.int32)[None, :]
    dist = qi + WINDOW - kj
    valid = (dist >= 0) & (dist < WINDOW)
    bucket = jnp.where(valid, _t5_bucket(dist), -1)
    return pl.pallas_call(
        _bias_kernel,
        in_specs=[_resident(), pl.BlockSpec(memory_space=pltpu.SMEM)],
        out_specs=_resident(),
        out_shape=jax.ShapeDtypeStruct((A_HEADS, WINDOW, 2 * WINDOW), F32),
        name="t5_bias_table",
    )(bucket, rel_bias)


def _swa_prompt_kernel(q_ref, kp_ref, kc_ref, vp_ref, vc_ref, bias_ref, sink_ref, o_ref):
    first = pl.program_id(1) == 0
    k = jnp.concatenate([kp_ref[...], kc_ref[...]], axis=0).astype(BF16)
    v = jnp.concatenate([vp_ref[...], vc_ref[...]], axis=0).astype(BF16)
    col = lax.broadcasted_iota(jnp.int32, (WINDOW, 2 * WINDOW), 1)
    pad = jnp.logical_and(first, col < WINDOW)
    for h in range(A_HEADS):
        g = h // A_GROUP
        qh = q_ref[:, h * A_HEAD_DIM:(h + 1) * A_HEAD_DIM]
        kg = k[:, g * A_HEAD_DIM:(g + 1) * A_HEAD_DIM]
        vg = v[:, g * A_HEAD_DIM:(g + 1) * A_HEAD_DIM]
        s = lax.dot_general(qh, kg, NT_DIMS, preferred_element_type=F32) + bias_ref[h]
        s = jnp.where(pad, NEG, s)
        sink = sink_ref[h]
        m = jnp.maximum(jnp.max(s, axis=-1, keepdims=True), sink)
        e = jnp.exp(s - m)
        denom = jnp.sum(e, axis=-1, keepdims=True) + jnp.exp(sink - m)
        pv = jnp.dot(e.astype(BF16), vg, preferred_element_type=F32)
        o_ref[:, h * A_HEAD_DIM:(h + 1) * A_HEAD_DIM] = (pv / denom).astype(BF16)


def _swa_prompt(q, k, v, bias, sinks, batch, seq):
    nb = seq // WINDOW
    q3 = q.reshape(batch, seq, NQ_A)
    k3 = k.reshape(batch, seq, NKV_A)
    v3 = v.reshape(batch, seq, NKV_A)
    prev = lambda b, j: (b, jnp.maximum(j - 1, 0), 0)
    cur = lambda b, j: (b, j, 0)
    o = pl.pallas_call(
        _swa_prompt_kernel,
        grid=(batch, nb),
        in_specs=[
            pl.BlockSpec((None, WINDOW, NQ_A), cur),
            pl.BlockSpec((None, WINDOW, NKV_A), prev),
            pl.BlockSpec((None, WINDOW, NKV_A), cur),
            pl.BlockSpec((None, WINDOW, NKV_A), prev),
            pl.BlockSpec((None, WINDOW, NKV_A), cur),
            _resident(),
            pl.BlockSpec(memory_space=pltpu.SMEM),
        ],
        out_specs=pl.BlockSpec((None, WINDOW, NQ_A), cur),
        out_shape=jax.ShapeDtypeStruct((batch, seq, NQ_A), BF16),
        compiler_params=_params(2),
        name="swa_prompt_attn",
    )(q3, k3, k3, v3, v3, bias, sinks)
    return o.reshape(batch * seq, NQ_A)


def _swa_sample_kernel(q_ref, kbuf_ref, vbuf_ref, knew_ref, vnew_ref, bias_ref, sink_ref,
                       o_ref, kout_ref, vout_ref):
    kw = jnp.concatenate([kbuf_ref[1:, :], knew_ref[...]], axis=0)
    vw = jnp.concatenate([vbuf_ref[1:, :], vnew_ref[...]], axis=0)
    kout_ref[...] = kw
    vout_ref[...] = vw
    kb = kw.astype(BF16)
    vb = vw.astype(BF16)
    for g in range(A_KV_HEADS):
        rows = slice(g * A_GROUP, (g + 1) * A_GROUP)
        cols = slice(g * A_HEAD_DIM, (g + 1) * A_HEAD_DIM)
        qg = q_ref[rows, :].astype(BF16)
        s = lax.dot_general(qg, kb[:, cols], NT_DIMS, preferred_element_type=F32) + bias_ref[rows, :]
        sink = sink_ref[rows, :]
        m = jnp.maximum(jnp.max(s, axis=-1, keepdims=True), sink)
        e = jnp.exp(s - m)
        denom = jnp.sum(e, axis=-1, keepdims=True) + jnp.exp(sink - m)
        pv = jnp.dot(e.astype(BF16), vb[:, cols], preferred_element_type=F32)
        o_ref[rows, :] = pv / denom


def _swa_sample(q, k_new, v_new, kbuf, vbuf, bias_row, sinks):
    nb, wb = kbuf.shape[0], kbuf.shape[1]
    q3 = q.astype(F32).reshape(nb, A_HEADS, A_HEAD_DIM)
    per_b = lambda b: (b, 0, 0)
    o, kout, vout = pl.pallas_call(
        _swa_sample_kernel,
        grid=(nb,),
        in_specs=[
            pl.BlockSpec((None, A_HEADS, A_HEAD_DIM), per_b),
            pl.BlockSpec((None, wb, NKV_A), per_b),
            pl.BlockSpec((None, wb, NKV_A), per_b),
            pl.BlockSpec((None, 1, NKV_A), per_b),
            pl.BlockSpec((None, 1, NKV_A), per_b),
            _resident(),
            _resident(),
        ],
        out_specs=[
            pl.BlockSpec((None, A_HEADS, A_HEAD_DIM), per_b),
            pl.BlockSpec((None, wb, NKV_A), per_b),
            pl.BlockSpec((None, wb, NKV_A), per_b),
        ],
        out_shape=[
            jax.ShapeDtypeStruct((nb, A_HEADS, A_HEAD_DIM), F32),
            jax.ShapeDtypeStruct((nb, wb, NKV_A), F32),
            jax.ShapeDtypeStruct((nb, wb, NKV_A), F32),
        ],
        compiler_params=_params(1),
        name="swa_sample_attn",
    )(q3, kbuf, vbuf, k_new.reshape(nb, 1, NKV_A), v_new.reshape(nb, 1, NKV_A),
      bias_row, sinks.reshape(A_HEADS, 1))
    return o.reshape(nb, NQ_A).astype(BF16), kout, vout


FF_CHUNK = 1024


def _out_mlp_kernel(*refs, pair_major, final):
    if final:
        o_ref, x_ref, wo_ref, g_ref, wup_ref, wdn_ref, gf_ref, out_ref = refs
    else:
        o_ref, x_ref, wo_ref, g_ref, wup_ref, wdn_ref, out_ref = refs
    if pair_major:
        o = jnp.concatenate([o_ref[j] for j in range(o_ref.shape[0])], axis=-1)
    else:
        o = o_ref[...]
    x1 = x_ref[...] + jnp.dot(o, wo_ref[...], preferred_element_type=F32)
    h = _rms(x1, g_ref[...]).astype(BF16)
    acc = x1
    for c in range(D_FF // FF_CHUNK):
        sl = slice(c * FF_CHUNK, (c + 1) * FF_CHUNK)
        u = jnp.dot(h, wup_ref[:, sl], preferred_element_type=F32)
        u = jnp.square(jnp.maximum(u, 0.0)).astype(BF16)
        acc = acc + jnp.dot(u, wdn_ref[sl, :], preferred_element_type=F32)
    if final:
        acc = _rms(acc, gf_ref[...])
    out_ref[...] = acc


def _out_mlp(o, x, wo, g, wup, wdn, gf, tm, pair_major):
    m = x.shape[0]
    final = gf is not None
    if pair_major:
        tiles_per_group = o.shape[2] // tm
        o_spec = pl.BlockSpec((None, o.shape[1], tm, LANES),
                              lambda i: (i // tiles_per_group, 0, i % tiles_per_group, 0))
    else:
        o_spec = pl.BlockSpec((tm, D_MODEL), lambda i: (i, 0))
    row_spec = pl.BlockSpec((tm, D_MODEL), lambda i: (i, 0))
    in_specs = [o_spec, row_spec, _resident(), _resident(), _resident(), _resident()]
    args = [o, x, wo, g, wup, wdn]
    if final:
        in_specs.append(_resident())
        args.append(gf)
    return pl.pallas_call(
        functools.partial(_out_mlp_kernel, pair_major=pair_major, final=final),
        grid=(m // tm,),
        in_specs=in_specs,
        out_specs=row_spec,
        out_shape=jax.ShapeDtypeStruct((m, D_MODEL), F32),
        compiler_params=_params(1),
        name="out_proj_mlp",
    )(*args)


N_PAIRS = B_HEADS // 2
ROPE_LANE0 = NOPE_DIM
C_IN_EXT = Q_LORA + KV_LORA + LANES


def _rope_slab(t, c, s1, s2):
    return t * c + pltpu.roll(t, HALF_ROPE, 1) * s1 + pltpu.roll(t, LANES - HALF_ROPE, 1) * s2


def _mla_proj_kernel(x_ref, g_ref, win_ref, qn_ref, kvn_ref, wqb_ref, wk_ref, wv_ref,
                     c_ref, s1_ref, s2_ref,
                     qcat_ref, kcat_ref, v_ref, lat_ref, krp_ref):
    h = _rms(x_ref[...], g_ref[...]).astype(BF16)
    c = jnp.dot(h, win_ref[...], preferred_element_type=F32)
    cq = _rms(c[:, :Q_LORA], qn_ref[...]).astype(BF16)
    lat = _rms(c[:, Q_LORA:Q_LORA + KV_LORA], kvn_ref[...])
    lat_ref[...] = lat
    cos, s1, s2 = c_ref[...], s1_ref[...], s2_ref[...]
    krp = _rope_slab(c[:, Q_LORA + KV_LORA:], cos, s1, s2)
    krp_ref[...] = krp
    q = jnp.dot(cq, wqb_ref[...], preferred_element_type=F32)
    latb = lat.astype(BF16)
    kn = jnp.dot(latb, wk_ref[...], preferred_element_type=F32)
    v = jnp.dot(latb, wv_ref[...], preferred_element_type=F32)
    for hh in range(B_HEADS):
        sl = slice(hh * LANES, (hh + 1) * LANES)
        qcat_ref[hh] = _rope_slab(q[:, sl], cos, s1, s2).astype(BF16)
        kcat_ref[hh] = (kn[:, sl] + krp).astype(BF16)
    for j in range(N_PAIRS):
        v_ref[j] = v[:, j * LANES:(j + 1) * LANES].astype(BF16)


def _mla_proj(x, g, win, qn, kvn, wqb, wk, wv, tabs, groups, rows, tm, tab_blocks):
    m = x.shape[0]
    tiles = rows // tm
    head_map = lambda i: (i // tiles, 0, i % tiles, 0)
    tab_spec = pl.BlockSpec((tm, LANES), lambda i: (i % tab_blocks, 0))
    return pl.pallas_call(
        _mla_proj_kernel,
        grid=(m // tm,),
        in_specs=[pl.BlockSpec((tm, D_MODEL), lambda i: (i, 0))] + [_resident()] * 7 + [tab_spec] * 3,
        out_specs=[
            pl.BlockSpec((None, B_HEADS, tm, LANES), head_map),
            pl.BlockSpec((None, B_HEADS, tm, LANES), head_map),
            pl.BlockSpec((None, N_PAIRS, tm, LANES), head_map),
            pl.BlockSpec((tm, KV_LORA), lambda i: (i, 0)),
            pl.BlockSpec((tm, LANES), lambda i: (i, 0)),
        ],
        out_shape=[
            jax.ShapeDtypeStruct((groups, B_HEADS, rows, LANES), BF16),
            jax.ShapeDtypeStruct((groups, B_HEADS, rows, LANES), BF16),
            jax.ShapeDtypeStruct((groups, N_PAIRS, rows, LANES), BF16),
            jax.ShapeDtypeStruct((m, KV_LORA), F32),
            jax.ShapeDtypeStruct((m, LANES), F32),
        ],
        compiler_params=_params(1),
        name="mla_proj",
    )(x, g, win, qn, kvn, wqb, wk, wv, *tabs)


def _rope_tables(pos):
    inv = ROPE_THETA ** (-jnp.arange(0, ROPE_DIM, 2, dtype=F32) / ROPE_DIM)
    ang = pos[:, None] * inv[None, :]
    cos, sin = jnp.cos(ang), jnp.sin(ang)
    s = pos.shape[0]
    ones = jnp.ones((s, ROPE_LANE0), F32)
    z16 = jnp.zeros((s, HALF_ROPE), F32)
    z64 = jnp.zeros((s, ROPE_LANE0), F32)
    z32 = jnp.zeros((s, LANES - ROPE_LANE0 - ROPE_DIM), F32)
    c = jnp.concatenate([ones, cos, cos, z32], axis=1)
    s1 = jnp.concatenate([z64, z16, sin, z32], axis=1)
    s2 = jnp.concatenate([z64, -sin, z16, z32], axis=1)
    return c, s1, s2


def _mla_attn_kernel(q_ref, k_ref, v_ref, o_ref, *, tq, scale):
    qi = pl.program_id(1)
    row = lax.broadcasted_iota(jnp.int32, (tq, tq), 0)
    col = lax.broadcasted_iota(jnp.int32, (tq, tq), 1)
    causal = col <= row
    lane = lax.broadcasted_iota(jnp.int32, (tq, LANES), 1)

    def head(j, parity):
        h = 2 * j + parity
        q = q_ref[h]

        def tile(t, carry, masked):
            m, l, acc = carry
            start = pl.multiple_of(t * tq, tq)
            k = k_ref[h, pl.ds(start, tq), :]
            v = v_ref[j, pl.ds(start, tq), :]
            s = lax.dot_general(q, k, NT_DIMS, preferred_element_type=F32) * scale
            if masked:
                s = jnp.where(causal, s, NEG)
            m_new = jnp.maximum(m, jnp.max(s, axis=-1, keepdims=True))
            alpha = jnp.exp(m - m_new)
            p = jnp.exp(s - m_new)
            l = alpha * l + jnp.sum(p, axis=-1, keepdims=True)
            acc = alpha * acc + jnp.dot(p.astype(BF16), v, preferred_element_type=F32)
            return m_new, l, acc

        init = (jnp.full((tq, 1), NEG, F32), jnp.zeros((tq, 1), F32), jnp.zeros((tq, LANES), F32))
        carry = lax.fori_loop(0, qi, lambda t, c: tile(t, c, False), init)
        _, l, acc = tile(qi, carry, True)
        return acc / l

    def pair(j, carry):
        even = head(j, 0)
        odd = head(j, 1)
        o_ref[j] = jnp.where(lane < V_DIM, even, odd).astype(BF16)
        return carry

    lax.fori_loop(0, N_PAIRS, pair, 0)


def _mla_attn(qcat, kcat, v, tq):
    batch, _, seq, _ = qcat.shape
    scale = (NOPE_DIM + ROPE_DIM) ** -0.5
    return pl.pallas_call(
        functools.partial(_mla_attn_kernel, tq=tq, scale=scale),
        grid=(batch, seq // tq),
        in_specs=[
            pl.BlockSpec((None, B_HEADS, tq, LANES), lambda b, i: (b, 0, i, 0)),
            pl.BlockSpec((None, B_HEADS, seq, LANES), lambda b, i: (b, 0, 0, 0)),
            pl.BlockSpec((None, N_PAIRS, seq, LANES), lambda b, i: (b, 0, 0, 0)),
        ],
        out_specs=pl.BlockSpec((None, N_PAIRS, tq, LANES), lambda b, i: (b, 0, i, 0)),
        out_shape=jax.ShapeDtypeStruct((batch, N_PAIRS, seq, LANES), BF16),
        compiler_params=_params(2),
        name="mla_prompt_attn",
    )(qcat, kcat, v)


def _absorb_kernel(qcat_ref, wuk_ref, qlat_ref):
    for h in range(B_HEADS):
        qlat_ref[h] = jnp.dot(qcat_ref[h], wuk_ref[h], preferred_element_type=F32).astype(BF16)


def _absorb(qcat, wuk_ext):
    nb = qcat.shape[1]
    return pl.pallas_call(
        _absorb_kernel,
        in_specs=[_resident(), _resident()],
        out_specs=_resident(),
        out_shape=jax.ShapeDtypeStruct((B_HEADS, nb, KV_LORA), BF16),
        name="mla_absorb_q",
    )(qcat, wuk_ext)


def _decode_kernel(pt_ref, qlat_ref, qr_ref, latnew_ref, krnew_ref, lat_hbm, kr_hbm, o_ref,
                   latbuf, krbuf, sem, m_sc, l_sc, acc_sc, *, pages_per_step, steps_per_batch, scale):
    b = pl.program_id(0)
    c = pl.program_id(1)
    step = b * steps_per_batch + c
    n_steps = pl.num_programs(0) * steps_per_batch
    slot = step % 2

    def chunk_copies(bb, cc, sl):
        copies = []
        for p in range(pages_per_step):
            page = pt_ref[bb, cc * pages_per_step + p]
            copies.append(pltpu.make_async_copy(lat_hbm.at[page], latbuf.at[sl, p], sem.at[0, sl]))
            copies.append(pltpu.make_async_copy(kr_hbm.at[page], krbuf.at[sl, p], sem.at[1, sl]))
        return copies

    @pl.when(step == 0)
    def _():
        for cp in chunk_copies(b, c, slot):
            cp.start()

    @pl.when(step + 1 < n_steps)
    def _():
        wrap = c + 1 == steps_per_batch
        nb = jnp.where(wrap, b + 1, b)
        nc = jnp.where(wrap, 0, c + 1)
        for cp in chunk_copies(nb, nc, 1 - slot):
            cp.start()

    qlat = qlat_ref[...]
    qr = qr_ref[...]

    @pl.when(c == 0)
    def _():
        ln = latnew_ref[...].astype(BF16).astype(F32)
        kn = krnew_ref[...].astype(BF16).astype(F32)
        s_new = (jnp.sum(qlat.astype(F32) * ln, axis=-1, keepdims=True)
                 + jnp.sum(qr.astype(F32) * kn, axis=-1, keepdims=True)) * scale
        m_sc[...] = s_new
        l_sc[...] = jnp.ones_like(s_new)
        acc_sc[...] = jnp.broadcast_to(ln, acc_sc.shape)

    for cp in chunk_copies(b, c, slot):
        cp.wait()

    n_keys = latbuf.shape[1] * latbuf.shape[2]
    lat = latbuf[slot].reshape(n_keys, KV_LORA).astype(BF16)
    kr = krbuf[slot].reshape(n_keys, ROPE_DIM).astype(BF16)
    s = (lax.dot_general(qlat, lat, NT_DIMS, preferred_element_type=F32)
         + lax.dot_general(qr, kr, NT_DIMS, preferred_element_type=F32)) * scale
    m_old = m_sc[...]
    m_new = jnp.maximum(m_old, jnp.max(s, axis=-1, keepdims=True))
    alpha = jnp.exp(m_old - m_new)
    p = jnp.exp(s - m_new)
    l_sc[...] = alpha * l_sc[...] + jnp.sum(p, axis=-1, keepdims=True)
    acc_sc[...] = alpha * acc_sc[...] + jnp.dot(p.astype(BF16), lat, preferred_element_type=F32)
    m_sc[...] = m_new

    @pl.when(c == steps_per_batch - 1)
    def _():
        o_ref[...] = acc_sc[...] / l_sc[...]


def _mla_decode(page_table, qlat, qr, lat_new, kr_new, lat_pool, kr_pool, pages_per_step):
    nb, n_pages = page_table.shape
    page_size = lat_pool.shape[1]
    steps_per_batch = n_pages // pages_per_step
    scale = (NOPE_DIM + ROPE_DIM) ** -0.5
    per_b = lambda b, c, pt: (b, 0, 0)
    grid_spec = pltpu.PrefetchScalarGridSpec(
        num_scalar_prefetch=1,
        grid=(nb, steps_per_batch),
        in_specs=[
            pl.BlockSpec((None, B_HEADS, KV_LORA), per_b),
            pl.BlockSpec((None, B_HEADS, ROPE_DIM), per_b),
            pl.BlockSpec((None, 1, KV_LORA), per_b),
            pl.BlockSpec((None, 1, ROPE_DIM), per_b),
            pl.BlockSpec(memory_space=pl.ANY),
            pl.BlockSpec(memory_space=pl.ANY),
        ],
        out_specs=pl.BlockSpec((None, B_HEADS, KV_LORA), per_b),
        scratch_shapes=[
            pltpu.VMEM((2, pages_per_step, page_size, KV_LORA), F32),
            pltpu.VMEM((2, pages_per_step, page_size, ROPE_DIM), F32),
            pltpu.SemaphoreType.DMA((2, 2)),
            pltpu.VMEM((B_HEADS, 1), F32),
            pltpu.VMEM((B_HEADS, 1), F32),
            pltpu.VMEM((B_HEADS, KV_LORA), F32),
        ],
    )
    return pl.pallas_call(
        functools.partial(_decode_kernel, pages_per_step=pages_per_step,
                          steps_per_batch=steps_per_batch, scale=scale),
        grid_spec=grid_spec,
        out_shape=jax.ShapeDtypeStruct((nb, B_HEADS, KV_LORA), F32),
        compiler_params=_params(2),
        name="mla_decode_attn",
    )(page_table, qlat, qr, lat_new, kr_new, lat_pool, kr_pool)


def _uv_kernel(olat_ref, wuv_ref, o_ref):
    for h in range(B_HEADS):
        o_ref[:, h * V_DIM:(h + 1) * V_DIM] = jnp.dot(
            olat_ref[h].astype(BF16), wuv_ref[h], preferred_element_type=F32).astype(BF16)


def _uv(olat_hm, wuv):
    nb = olat_hm.shape[1]
    return pl.pallas_call(
        _uv_kernel,
        in_specs=[_resident(), _resident()],
        out_specs=_resident(),
        out_shape=jax.ShapeDtypeStruct((nb, B_HEADS * V_DIM), BF16),
        name="mla_value_up",
    )(olat_hm, wuv)


PROMPT_TM = 256
MLA_TQ = 256
DECODE_PAGES = 16


def kernel(x_prompt, x_sample, cache_a_k, cache_a_v, cache_b_latent, cache_b_krope, page_table,
           rel_bias, norm_mix, norm_mlp, norm_final, a_w_qkv, a_w_o, a_sinks,
           b_w_in, b_q_norm, b_kv_norm, b_w_q_b, b_w_kv_b, b_w_o, mlp_w_up, mlp_w_down):
    batch, seq, _ = x_prompt.shape
    nb_s = x_sample.shape[0]
    past = page_table.shape[1] * cache_b_latent.shape[2]
    wb = cache_a_k.shape[2]

    xp = x_prompt.reshape(batch * seq, D_MODEL)
    xs = x_sample.reshape(nb_s, D_MODEL)
    row = lambda t: t.reshape(1, -1)

    w_qkv = a_w_qkv[0].astype(BF16)
    w_o_a = a_w_o[0].astype(BF16)
    w_up0, w_dn0 = mlp_w_up[0].astype(BF16), mlp_w_down[0].astype(BF16)
    bias = _bias_table(rel_bias)
    sinks = a_sinks[0]

    qp, kp, vp = _norm_qkv(xp, row(norm_mix[0]), w_qkv, PROMPT_TM)
    op = _swa_prompt(qp, kp, vp, bias, sinks, batch, seq)
    xp = _out_mlp(op, xp, w_o_a, row(norm_mlp[0]), w_up0, w_dn0, None, PROMPT_TM, False)

    qs, ks, vs = _norm_qkv(xs, row(norm_mix[0]), w_qkv, nb_s)
    bias_row = bias[:, WINDOW - 1, WINDOW:]
    os_, a_k_s, a_v_s = _swa_sample(qs, ks, vs,
                                    cache_a_k[0].reshape(nb_s, wb, NKV_A),
                                    cache_a_v[0].reshape(nb_s, wb, NKV_A), bias_row, sinks)
    xs = _out_mlp(os_, xs, w_o_a, row(norm_mlp[0]), w_up0, w_dn0, None, nb_s, False)

    w_in = b_w_in[0]
    zeros = lambda n: jnp.zeros((D_MODEL, n), F32)
    w_in_ext = jnp.concatenate([w_in[:, :Q_LORA + KV_LORA], zeros(ROPE_LANE0),
                                w_in[:, Q_LORA + KV_LORA:], zeros(LANES - ROPE_LANE0 - ROPE_DIM)],
                               axis=1).astype(BF16)
    qk_dim = NOPE_DIM + ROPE_DIM
    w_qb = jnp.pad(b_w_q_b[0].reshape(Q_LORA, B_HEADS, qk_dim),
                   ((0, 0), (0, 0), (0, LANES - qk_dim))).reshape(Q_LORA, B_HEADS * LANES).astype(BF16)
    w_kvb = b_w_kv_b[0]
    w_uk, w_uv = w_kvb[..., :NOPE_DIM], w_kvb[..., NOPE_DIM:]
    w_k = jnp.pad(w_uk, ((0, 0), (0, 0), (0, LANES - NOPE_DIM))).reshape(KV_LORA, B_HEADS * LANES).astype(BF16)
    w_v = w_uv.reshape(KV_LORA, B_HEADS * V_DIM).astype(BF16)
    w_o_b = b_w_o[0].astype(BF16)
    w_up1, w_dn1 = mlp_w_up[1].astype(BF16), mlp_w_down[1].astype(BF16)
    proj_w = (row(norm_mix[1]), w_in_ext, row(b_q_norm[0]), row(b_kv_norm[0]), w_qb, w_k, w_v)

    tabs_p = _rope_tables(jnp.arange(seq, dtype=F32))
    qcat, kcat, vpm, lat_p, krp_p = _mla_proj(xp, *proj_w, tabs_p, batch, seq, PROMPT_TM, seq // PROMPT_TM)
    o_pm = _mla_attn(qcat, kcat, vpm, MLA_TQ)
    yp = _out_mlp(o_pm, xp, w_o_b, row(norm_mlp[1]), w_up1, w_dn1, row(norm_final), PROMPT_TM, True)

    tabs_s = _rope_tables(jnp.full((nb_s,), past, F32))
    qcat_s, _, _, lat_s, krp_s = _mla_proj(xs, *proj_w, tabs_s, 1, nb_s, nb_s, 1)
    w_uk_ext = jnp.pad(jnp.transpose(w_uk, (1, 2, 0)), ((0, 0), (0, LANES - NOPE_DIM), (0, 0))).astype(BF16)
    qlat = jnp.transpose(_absorb(qcat_s[0], w_uk_ext), (1, 0, 2))
    qr = jnp.transpose(qcat_s[0, :, :, ROPE_LANE0:ROPE_LANE0 + ROPE_DIM], (1, 0, 2))
    kr_s = krp_s[:, ROPE_LANE0:ROPE_LANE0 + ROPE_DIM]
    olat = _mla_decode(page_table, qlat, qr, lat_s.reshape(nb_s, 1, KV_LORA), kr_s.reshape(nb_s, 1, ROPE_DIM),
                       cache_b_latent[0], cache_b_krope[0], DECODE_PAGES)
    o_s = _uv(jnp.transpose(olat, (1, 0, 2)), jnp.transpose(w_uv, (1, 0, 2)).astype(BF16))
    ys = _out_mlp(o_s, xs, w_o_b, row(norm_mlp[1]), w_up1, w_dn1, row(norm_final), nb_s, False)

    wbp = min(WINDOW, seq)
    k4 = kp.reshape(batch, seq, A_KV_HEADS, A_HEAD_DIM)[:, seq - wbp:]
    v4 = vp.reshape(batch, seq, A_KV_HEADS, A_HEAD_DIM)[:, seq - wbp:]
    kr_p = krp_p[:, ROPE_LANE0:ROPE_LANE0 + ROPE_DIM]
    return (
        yp.reshape(batch, seq, D_MODEL),
        ys.reshape(nb_s, 1, D_MODEL),
        k4[None], v4[None],
        lat_p.reshape(1, batch, seq, KV_LORA),
        kr_p.reshape(1, batch, seq, ROPE_DIM),
        a_k_s.reshape(1, nb_s, wb, A_KV_HEADS, A_HEAD_DIM),
        a_v_s.reshape(1, nb_s, wb, A_KV_HEADS, A_HEAD_DIM),
        lat_s.reshape(1, nb_s, 1, KV_LORA),
        kr_s.reshape(1, nb_s, 1, ROPE_DIM),
    )
```

```python
import functools
import math

import jax
import jax.numpy as jnp
from jax import lax
from jax.experimental import pallas as pl
from jax.experimental.pallas import tpu as pltpu

F32 = jnp.float32
BF16 = jnp.bfloat16

D_MODEL = 1024
A_HEADS = 16
A_KV_HEADS = 4
A_HEAD_DIM = 64
A_GROUP = A_HEADS // A_KV_HEADS
WINDOW = 128
N_BUCKETS = 32
MAX_EXACT = N_BUCKETS // 2
MAX_DISTANCE = 128
B_HEADS = 16
Q_LORA = 768
KV_LORA = 256
NOPE_DIM = 64
ROPE_DIM = 32
HALF_ROPE = ROPE_DIM // 2
V_DIM = 64
ROPE_THETA = 10000.0
D_FF = 4 * D_MODEL
EPS = 1e-6

LANES = 128
VMEM_LIMIT_BYTES = 56 * 1024 * 1024

NEG = -1e30
LOG2E = math.log2(math.e)

NT_DIMS = (((1,), (1,)), ((), ()))


def _rms(x, g):
    return x * lax.rsqrt(jnp.mean(x * x, axis=-1, keepdims=True) + EPS) * g


def _params(n_axes):
    return pltpu.CompilerParams(
        dimension_semantics=("arbitrary",) * n_axes,
        vmem_limit_bytes=VMEM_LIMIT_BYTES,
    )


def _resident():
    return pl.BlockSpec(memory_space=pltpu.VMEM)


NQ_A = A_HEADS * A_HEAD_DIM
NKV_A = A_KV_HEADS * A_HEAD_DIM


def _qkv_kernel(x_ref, g_ref, w_ref, q_ref, k_ref, v_ref):
    h = _rms(x_ref[...], g_ref[...]).astype(BF16)
    qkv = jnp.dot(h, w_ref[...], preferred_element_type=F32)
    q_ref[...] = (qkv[:, :NQ_A] * (A_HEAD_DIM ** -0.5)).astype(BF16)
    k_ref[...] = qkv[:, NQ_A:NQ_A + NKV_A]
    v_ref[...] = qkv[:, NQ_A + NKV_A:]


def _norm_qkv(x, g, w_bf16, tm):
    m = x.shape[0]
    return pl.pallas_call(
        _qkv_kernel,
        grid=(m // tm,),
        in_specs=[
            pl.BlockSpec((tm, D_MODEL), lambda i: (i, 0)),
            _resident(),
            _resident(),
        ],
        out_specs=[
            pl.BlockSpec((tm, NQ_A), lambda i: (i, 0)),
            pl.BlockSpec((tm, NKV_A), lambda i: (i, 0)),
            pl.BlockSpec((tm, NKV_A), lambda i: (i, 0)),
        ],
        out_shape=[
            jax.ShapeDtypeStruct((m, NQ_A), BF16),
            jax.ShapeDtypeStruct((m, NKV_A), F32),
            jax.ShapeDtypeStruct((m, NKV_A), F32),
        ],
        compiler_params=_params(1),
        name="norm_qkv",
    )(x, g, w_bf16)


def _t5_bucket(dist):
    d = jnp.maximum(dist, 0)
    df = jnp.maximum(d, 1).astype(F32)
    large = MAX_EXACT + (jnp.log(df / MAX_EXACT) / math.log(MAX_DISTANCE / MAX_EXACT)
                         * (N_BUCKETS - MAX_EXACT)).astype(jnp.int32)
    large = jnp.minimum(large, N_BUCKETS - 1)
    return jnp.where(d < MAX_EXACT, d, large)


def _bias_kernel(bucket_ref, rb_ref, out_ref):
    bucket = bucket_ref[...]
    for h in range(A_HEADS):
        acc = jnp.full(bucket.shape, NEG, F32)
        for b in range(N_BUCKETS):
            acc = jnp.where(bucket == b, rb_ref[b, h], acc)
        out_ref[h] = acc


def _bias_table(rel_bias):
    qi = jnp.arange(WINDOW, dtype=jnp.int32)[:, None]
    kj = jnp.arange(2 * WINDOW, dtype=jnp.int32)[None, :]
    dist = qi + WINDOW - kj
    valid = (dist >= 0) & (dist < WINDOW)
    bucket = jnp.where(valid, _t5_bucket(dist), -1)
    return pl.pallas_call(
        _bias_kernel,
        in_specs=[_resident(), pl.BlockSpec(memory_space=pltpu.SMEM)],
        out_specs=_resident(),
        out_shape=jax.ShapeDtypeStruct((A_HEADS, WINDOW, 2 * WINDOW), F32),
        name="t5_bias_table",
    )(bucket, rel_bias)


def _swa_prompt_kernel(q_ref, kp_ref, kc_ref, vp_ref, vc_ref, bias_ref, sink_ref, o_ref):
    first = pl.program_id(1) == 0
    k = jnp.concatenate([kp_ref[...], kc_ref[...]], axis=0).astype(BF16)
    v = jnp.concatenate([vp_ref[...], vc_ref[...]], axis=0).astype(BF16)
    col = lax.broadcasted_iota(jnp.int32, (WINDOW, 2 * WINDOW), 1)
    pad = jnp.logical_and(first, col < WINDOW)
    for h in range(A_HEADS):
        g = h // A_GROUP
        qh = q_ref[:, h * A_HEAD_DIM:(h + 1) * A_HEAD_DIM]
        kg = k[:, g * A_HEAD_DIM:(g + 1) * A_HEAD_DIM]
        vg = v[:, g * A_HEAD_DIM:(g + 1) * A_HEAD_DIM]
        s = lax.dot_general(qh, kg, NT_DIMS, preferred_element_type=F32) + bias_ref[h]
        s = jnp.where(pad, NEG, s)
        sink = sink_ref[h]
        m = jnp.maximum(jnp.max(s, axis=-1, keepdims=True), sink)
        e = jnp.exp(s - m)
        denom = jnp.sum(e, axis=-1, keepdims=True) + jnp.exp(sink - m)
        pv = jnp.dot(e.astype(BF16), vg, preferred_element_type=F32)
        o_ref[:, h * A_HEAD_DIM:(h + 1) * A_HEAD_DIM] = (pv / denom).astype(BF16)


def _swa_prompt(q, k, v, bias, sinks, batch, seq):
    nb = seq // WINDOW
    q3 = q.reshape(batch, seq, NQ_A)
    k3 = k.reshape(batch, seq, NKV_A)
    v3 = v.reshape(batch, seq, NKV_A)
    prev = lambda b, j: (b, jnp.maximum(j - 1, 0), 0)
    cur = lambda b, j: (b, j, 0)
    o = pl.pallas_call(
        _swa_prompt_kernel,
        grid=(batch, nb),
        in_specs=[
            pl.BlockSpec((None, WINDOW, NQ_A), cur),
            pl.BlockSpec((None, WINDOW, NKV_A), prev),
            pl.BlockSpec((None, WINDOW, NKV_A), cur),
            pl.BlockSpec((None, WINDOW, NKV_A), prev),
            pl.BlockSpec((None, WINDOW, NKV_A), cur),
            _resident(),
            pl.BlockSpec(memory_space=pltpu.SMEM),
        ],
        out_specs=pl.BlockSpec((None, WINDOW, NQ_A), cur),
        out_shape=jax.ShapeDtypeStruct((batch, seq, NQ_A), BF16),
        compiler_params=_params(2),
        name="swa_prompt_attn",
    )(q3, k3, k3, v3, v3, bias, sinks)
    return o.reshape(batch * seq, NQ_A)


def _swa_sample_kernel(q_ref, kbuf_ref, vbuf_ref, knew_ref, vnew_ref, bias_ref, sink_ref,
                       o_ref, kout_ref, vout_ref):
    kw = jnp.concatenate([kbuf_ref[1:, :], knew_ref[...]], axis=0)
    vw = jnp.concatenate([vbuf_ref[1:, :], vnew_ref[...]], axis=0)
    kout_ref[...] = kw
    vout_ref[...] = vw
    kb = kw.astype(BF16)
    vb = vw.astype(BF16)
    for g in range(A_KV_HEADS):
        rows = slice(g * A_GROUP, (g + 1) * A_GROUP)
        cols = slice(g * A_HEAD_DIM, (g + 1) * A_HEAD_DIM)
        qg = q_ref[rows, :].astype(BF16)
        s = lax.dot_general(qg, kb[:, cols], NT_DIMS, preferred_element_type=F32) + bias_ref[rows, :]
        sink = sink_ref[rows, :]
        m = jnp.maximum(jnp.max(s, axis=-1, keepdims=True), sink)
        e = jnp.exp(s - m)
        denom = jnp.sum(e, axis=-1, keepdims=True) + jnp.exp(sink - m)
        pv = jnp.dot(e.astype(BF16), vb[:, cols], preferred_element_type=F32)
        o_ref[rows, :] = pv / denom


def _swa_sample(q, k_new, v_new, kbuf, vbuf, bias_row, sinks):
    nb, wb = kbuf.shape[0], kbuf.shape[1]
    q3 = q.astype(F32).reshape(nb, A_HEADS, A_HEAD_DIM)
    per_b = lambda b: (b, 0, 0)
    o, kout, vout = pl.pallas_call(
        _swa_sample_kernel,
        grid=(nb,),
        in_specs=[
            pl.BlockSpec((None, A_HEADS, A_HEAD_DIM), per_b),
            pl.BlockSpec((None, wb, NKV_A), per_b),
            pl.BlockSpec((None, wb, NKV_A), per_b),
            pl.BlockSpec((None, 1, NKV_A), per_b),
            pl.BlockSpec((None, 1, NKV_A), per_b),
            _resident(),
            _resident(),
        ],
        out_specs=[
            pl.BlockSpec((None, A_HEADS, A_HEAD_DIM), per_b),
            pl.BlockSpec((None, wb, NKV_A), per_b),
            pl.BlockSpec((None, wb, NKV_A), per_b),
        ],
        out_shape=[
            jax.ShapeDtypeStruct((nb, A_HEADS, A_HEAD_DIM), F32),
            jax.ShapeDtypeStruct((nb, wb, NKV_A), F32),
            jax.ShapeDtypeStruct((nb, wb, NKV_A), F32),
        ],
        compiler_params=_params(1),
        name="swa_sample_attn",
    )(q3, kbuf, vbuf, k_new.reshape(nb, 1, NKV_A), v_new.reshape(nb, 1, NKV_A),
      bias_row, sinks.reshape(A_HEADS, 1))
    return o.reshape(nb, NQ_A).astype(BF16), kout, vout


FF_CHUNK = 1024


def _out_mlp_kernel(*refs, pair_major, final):
    if final:
        o_ref, x_ref, wo_ref, g_ref, wup_ref, wdn_ref, gf_ref, out_ref = refs
    else:
        o_ref, x_ref, wo_ref, g_ref, wup_ref, wdn_ref, out_ref = refs
    if pair_major:
        o = jnp.concatenate([o_ref[j] for j in range(o_ref.shape[0])], axis=-1)
    else:
        o = o_ref[...]
    x1 = x_ref[...] + jnp.dot(o, wo_ref[...], preferred_element_type=F32)
    h = _rms(x1, g_ref[...]).astype(BF16)
    acc = x1
    for c in range(D_FF // FF_CHUNK):
        sl = slice(c * FF_CHUNK, (c + 1) * FF_CHUNK)
        u = jnp.dot(h, wup_ref[:, sl], preferred_element_type=F32)
        u = jnp.square(jnp.maximum(u, 0.0)).astype(BF16)
        acc = acc + jnp.dot(u, wdn_ref[sl, :], preferred_element_type=F32)
    if final:
        acc = _rms(acc, gf_ref[...])
    out_ref[...] = acc


def _out_mlp(o, x, wo, g, wup, wdn, gf, tm, pair_major):
    m = x.shape[0]
    final = gf is not None
    if pair_major:
        tiles_per_group = o.shape[2] // tm
        o_spec = pl.BlockSpec((None, o.shape[1], tm, LANES),
                              lambda i: (i // tiles_per_group, 0, i % tiles_per_group, 0))
    else:
        o_spec = pl.BlockSpec((tm, D_MODEL), lambda i: (i, 0))
    row_spec = pl.BlockSpec((tm, D_MODEL), lambda i: (i, 0))
    in_specs = [o_spec, row_spec, _resident(), _resident(), _resident(), _resident()]
    args = [o, x, wo, g, wup, wdn]
    if final:
        in_specs.append(_resident())
        args.append(gf)
    return pl.pallas_call(
        functools.partial(_out_mlp_kernel, pair_major=pair_major, final=final),
        grid=(m // tm,),
        in_specs=in_specs,
        out_specs=row_spec,
        out_shape=jax.ShapeDtypeStruct((m, D_MODEL), F32),
        compiler_params=_params(1),
        name="out_proj_mlp",
    )(*args)


N_PAIRS = B_HEADS // 2
ROPE_LANE0 = NOPE_DIM
C_IN_EXT = Q_LORA + KV_LORA + LANES


def _rope_slab(t, c, s1, s2):
    return t * c + pltpu.roll(t, HALF_ROPE, 1) * s1 + pltpu.roll(t, LANES - HALF_ROPE, 1) * s2


def _mla_proj_kernel(x_ref, g_ref, win_ref, qn_ref, kvn_ref, wqb_ref, wk_ref, wvt_ref,
                     c_ref, s1_ref, s2_ref,
                     qcat_ref, kcat_ref, vt_ref, lat_ref, krp_ref, *, q_scale):
    h = _rms(x_ref[...], g_ref[...]).astype(BF16)
    c = jnp.dot(h, win_ref[...], preferred_element_type=F32)
    cq = _rms(c[:, :Q_LORA], qn_ref[...]).astype(BF16)
    lat = _rms(c[:, Q_LORA:Q_LORA + KV_LORA], kvn_ref[...])
    lat_ref[...] = lat
    cos, s1, s2 = c_ref[...], s1_ref[...], s2_ref[...]
    krp = _rope_slab(c[:, Q_LORA + KV_LORA:], cos, s1, s2)
    krp_ref[...] = krp
    q = jnp.dot(cq, wqb_ref[...], preferred_element_type=F32)
    latb = lat.astype(BF16)
    kn = jnp.dot(latb, wk_ref[...], preferred_element_type=F32)
    vt = lax.dot_general(wvt_ref[...], latb, NT_DIMS, preferred_element_type=F32)
    for hh in range(B_HEADS):
        sl = slice(hh * LANES, (hh + 1) * LANES)
        qh = _rope_slab(q[:, sl], cos, s1, s2)
        if q_scale != 1.0:
            qh = qh * q_scale
        qcat_ref[hh] = qh.astype(BF16)
        kcat_ref[hh] = (kn[:, sl] + krp).astype(BF16)
    for j in range(N_PAIRS):
        vt_ref[j] = vt[j * LANES:(j + 1) * LANES, :].astype(BF16)


def _mla_proj(x, g, win, qn, kvn, wqb, wk, wvt, tabs, groups, rows, tm, tab_blocks, q_scale):
    m = x.shape[0]
    tiles = rows // tm
    head_map = lambda i: (i // tiles, 0, i % tiles, 0)
    tab_spec = pl.BlockSpec((tm, LANES), lambda i: (i % tab_blocks, 0))
    return pl.pallas_call(
        functools.partial(_mla_proj_kernel, q_scale=q_scale),
        grid=(m // tm,),
        in_specs=[pl.BlockSpec((tm, D_MODEL), lambda i: (i, 0))] + [_resident()] * 7 + [tab_spec] * 3,
        out_specs=[
            pl.BlockSpec((None, B_HEADS, tm, LANES), head_map),
            pl.BlockSpec((None, B_HEADS, tm, LANES), head_map),
            pl.BlockSpec((None, N_PAIRS, LANES, tm), lambda i: (i // tiles, 0, 0, i % tiles)),
            pl.BlockSpec((tm, KV_LORA), lambda i: (i, 0)),
            pl.BlockSpec((tm, LANES), lambda i: (i, 0)),
        ],
        out_shape=[
            jax.ShapeDtypeStruct((groups, B_HEADS, rows, LANES), BF16),
            jax.ShapeDtypeStruct((groups, B_HEADS, rows, LANES), BF16),
            jax.ShapeDtypeStruct((groups, N_PAIRS, LANES, rows), BF16),
            jax.ShapeDtypeStruct((m, KV_LORA), F32),
            jax.ShapeDtypeStruct((m, LANES), F32),
        ],
        compiler_params=_params(1),
        name="mla_proj",
    )(x, g, win, qn, kvn, wqb, wk, wvt, *tabs)


def _rope_tables(pos):
    inv = ROPE_THETA ** (-jnp.arange(0, ROPE_DIM, 2, dtype=F32) / ROPE_DIM)
    ang = pos[:, None] * inv[None, :]
    cos, sin = jnp.cos(ang), jnp.sin(ang)
    s = pos.shape[0]
    ones = jnp.ones((s, ROPE_LANE0), F32)
    z16 = jnp.zeros((s, HALF_ROPE), F32)
    z64 = jnp.zeros((s, ROPE_LANE0), F32)
    z32 = jnp.zeros((s, LANES - ROPE_LANE0 - ROPE_DIM), F32)
    c = jnp.concatenate([ones, cos, cos, z32], axis=1)
    s1 = jnp.concatenate([z64, z16, sin, z32], axis=1)
    s2 = jnp.concatenate([z64, -sin, z16, z32], axis=1)
    return c, s1, s2


HEADS_PER_BODY = 8
REDUCE_ROWS = 64


def _col_reduce(x, op):
    n, w = x.shape
    part = op(x.reshape(n // REDUCE_ROWS, REDUCE_ROWS, w), axis=0)
    return op(part, axis=0, keepdims=True)


def _mla_attn_kernel(q_ref, k_ref, vt_ref, o_ref, *, tq, n_tiles):
    qi = pl.program_id(1)
    key = lax.broadcasted_iota(jnp.int32, (tq, tq), 0)
    qry = lax.broadcasted_iota(jnp.int32, (tq, tq), 1)
    causal = key <= qry

    def scores(h, n_keys):
        k = k_ref[h, pl.ds(0, n_keys), :]
        return lax.dot_general(k, q_ref[h], NT_DIMS, preferred_element_type=F32)

    def attend(st, j, n_keys):
        diag = jnp.where(causal, st[n_keys - tq:, :], NEG)
        if n_keys > tq:
            full = st[:n_keys - tq, :]
            m = jnp.maximum(_col_reduce(full, jnp.max), _col_reduce(diag, jnp.max))
            p = jnp.concatenate([jnp.exp2(full - m), jnp.exp2(diag - m)], axis=0)
        else:
            m = _col_reduce(diag, jnp.max)
            p = jnp.exp2(diag - m)
        l = _col_reduce(p, jnp.sum)
        ot = jnp.dot(vt_ref[j, :, pl.ds(0, n_keys)], p.astype(BF16), preferred_element_type=F32)
        return ot / l

    for c in range(n_tiles):
        @pl.when(qi == c)
        def _(c=c):
            n_keys = (c + 1) * tq

            def body(jj, carry):
                heads = [HEADS_PER_BODY * jj + u for u in range(HEADS_PER_BODY)]
                sts = [scores(h, n_keys) for h in heads]
                outs = [attend(st, (HEADS_PER_BODY // 2) * jj + u // 2, n_keys) for u, st in enumerate(sts)]
                for u in range(0, HEADS_PER_BODY, 2):
                    ot = jnp.concatenate([outs[u][:V_DIM], outs[u + 1][V_DIM:]], axis=0)
                    o_ref[(HEADS_PER_BODY // 2) * jj + u // 2] = ot.T.astype(BF16)
                return carry

            lax.fori_loop(0, B_HEADS // HEADS_PER_BODY, body, 0)


def _mla_attn(qcat, kcat, vt, tq):
    batch, _, seq, _ = qcat.shape
    n_tiles = seq // tq
    return pl.pallas_call(
        functools.partial(_mla_attn_kernel, tq=tq, n_tiles=n_tiles),
        grid=(batch, n_tiles),
        in_specs=[
            pl.BlockSpec((None, B_HEADS, tq, LANES), lambda b, i: (b, 0, i, 0)),
            pl.BlockSpec((None, B_HEADS, seq, LANES), lambda b, i: (b, 0, 0, 0)),
            pl.BlockSpec((None, N_PAIRS, LANES, seq), lambda b, i: (b, 0, 0, 0)),
        ],
        out_specs=pl.BlockSpec((None, N_PAIRS, tq, LANES), lambda b, i: (b, 0, i, 0)),
        out_shape=jax.ShapeDtypeStruct((batch, N_PAIRS, seq, LANES), BF16),
        compiler_params=_params(2),
        name="mla_prompt_attn",
    )(qcat, kcat, vt)


def _absorb_kernel(qcat_ref, wuk_ref, qlat_ref):
    for h in range(B_HEADS):
        qlat_ref[h] = jnp.dot(qcat_ref[h], wuk_ref[h], preferred_element_type=F32).astype(BF16)


def _absorb(qcat, wuk_ext):
    nb = qcat.shape[1]
    return pl.pallas_call(
        _absorb_kernel,
        in_specs=[_resident(), _resident()],
        out_specs=_resident(),
        out_shape=jax.ShapeDtypeStruct((B_HEADS, nb, KV_LORA), BF16),
        name="mla_absorb_q",
    )(qcat, wuk_ext)


def _decode_kernel(pt_ref, qlat_ref, qr_ref, latnew_ref, krnew_ref, lat_hbm, krt_hbm, o_ref,
                   latbuf, krbuf, sem, m_sc, l_sc, acc_sc,
                   *, pages_per_step, steps_per_batch, n_chains, scale):
    b = pl.program_id(0)
    c = pl.program_id(1)
    step = b * steps_per_batch + c
    last_step = step + 1 == pl.num_programs(0) * steps_per_batch
    slot = step % 2
    page_size = latbuf.shape[2]

    def chunk_copies(bb, cc, sl):
        copies = []
        for p in range(pages_per_step):
            page = pt_ref[bb, cc * pages_per_step + p]
            copies.append(pltpu.make_async_copy(lat_hbm.at[page], latbuf.at[sl, p], sem.at[0, sl]))
            copies.append(pltpu.make_async_copy(krt_hbm.at[page], krbuf.at[sl, p], sem.at[1, sl]))
        return copies

    @pl.when(step == 0)
    def _():
        for cp in chunk_copies(b, c, slot):
            cp.start()

    wrap = c + 1 == steps_per_batch
    nb = jnp.where(last_step, b, jnp.where(wrap, b + 1, b))
    nc = jnp.where(last_step, c, jnp.where(wrap, 0, c + 1))
    for cp in chunk_copies(nb, nc, 1 - slot):
        cp.start()

    qlat = qlat_ref[...]
    qr = qr_ref[...]

    @pl.when(c == 0)
    def _():
        ln = latnew_ref[...].astype(BF16).astype(F32)
        kn = krnew_ref[...].astype(BF16).astype(F32)
        s_new = (jnp.sum(qlat.astype(F32) * ln, axis=-1, keepdims=True)
                 + jnp.sum(qr.astype(F32) * kn, axis=-1, keepdims=True)) * scale
        m_sc[...] = jnp.full(m_sc.shape, NEG, F32)
        l_sc[...] = jnp.zeros(l_sc.shape, F32)
        acc_sc[...] = jnp.zeros(acc_sc.shape, F32)
        m_sc[0] = s_new
        l_sc[0] = jnp.ones_like(s_new)
        acc_sc[0] = jnp.broadcast_to(ln, acc_sc.shape[1:])

    for cp in chunk_copies(b, c, slot):
        cp.wait()

    ppc = pages_per_step // n_chains
    chains = range(n_chains)
    s_rope = [jnp.concatenate(
        [jnp.dot(qr, krbuf[slot, i * ppc + p].astype(BF16), preferred_element_type=F32)
         for p in range(ppc)], axis=1) for i in chains]
    lat = [latbuf[slot, pl.ds(i * ppc, ppc)].reshape(ppc * page_size, KV_LORA).astype(BF16) for i in chains]
    s = [(lax.dot_general(qlat, lat[i], NT_DIMS, preferred_element_type=F32) + s_rope[i]) * scale
         for i in chains]
    m_old = [m_sc[i] for i in chains]
    m_new = [jnp.maximum(m_old[i], jnp.max(s[i], axis=-1, keepdims=True)) for i in chains]
    alpha = [jnp.exp(m_old[i] - m_new[i]) for i in chains]
    p = [jnp.exp(s[i] - m_new[i]) for i in chains]
    pv = [jnp.dot(p[i].astype(BF16), lat[i], preferred_element_type=F32) for i in chains]
    for i in chains:
        l_sc[i] = alpha[i] * l_sc[i] + jnp.sum(p[i], axis=-1, keepdims=True)
        acc_sc[i] = alpha[i] * acc_sc[i] + pv[i]
        m_sc[i] = m_new[i]

    @pl.when(c == steps_per_batch - 1)
    def _():
        m = m_sc[0]
        for i in range(1, n_chains):
            m = jnp.maximum(m, m_sc[i])
        l = jnp.zeros_like(m)
        acc = jnp.zeros(acc_sc.shape[1:], F32)
        for i in range(n_chains):
            w = jnp.exp(m_sc[i] - m)
            l = l + w * l_sc[i]
            acc = acc + w * acc_sc[i]
        o_ref[...] = acc / l

    @pl.when(last_step)
    def _():
        for cp in chunk_copies(nb, nc, 1 - slot):
            cp.wait()


def _mla_decode(page_table, qlat, qr, lat_new, kr_new, lat_pool, krt_pool, pages_per_step, n_chains):
    nb, n_pages = page_table.shape
    page_size = lat_pool.shape[1]
    steps_per_batch = n_pages // pages_per_step
    scale = (NOPE_DIM + ROPE_DIM) ** -0.5
    per_b = lambda b, c, pt: (b, 0, 0)
    grid_spec = pltpu.PrefetchScalarGridSpec(
        num_scalar_prefetch=1,
        grid=(nb, steps_per_batch),
        in_specs=[
            pl.BlockSpec((None, B_HEADS, KV_LORA), per_b),
            pl.BlockSpec((None, B_HEADS, ROPE_DIM), per_b),
            pl.BlockSpec((None, 1, KV_LORA), per_b),
            pl.BlockSpec((None, 1, ROPE_DIM), per_b),
            pl.BlockSpec(memory_space=pl.ANY),
            pl.BlockSpec(memory_space=pl.ANY),
        ],
        out_specs=pl.BlockSpec((None, B_HEADS, KV_LORA), per_b),
        scratch_shapes=[
            pltpu.VMEM((2, pages_per_step, page_size, KV_LORA), F32),
            pltpu.VMEM((2, pages_per_step, ROPE_DIM, page_size), F32),
            pltpu.SemaphoreType.DMA((2, 2)),
            pltpu.VMEM((n_chains, B_HEADS, 1), F32),
            pltpu.VMEM((n_chains, B_HEADS, 1), F32),
            pltpu.VMEM((n_chains, B_HEADS, KV_LORA), F32),
        ],
    )
    return pl.pallas_call(
        functools.partial(_decode_kernel, pages_per_step=pages_per_step,
                          steps_per_batch=steps_per_batch, n_chains=n_chains, scale=scale),
        grid_spec=grid_spec,
        out_shape=jax.ShapeDtypeStruct((nb, B_HEADS, KV_LORA), F32),
        compiler_params=_params(2),
        name="mla_decode_attn",
    )(page_table, qlat, qr, lat_new, kr_new, lat_pool, krt_pool)


def _uv_kernel(olat_ref, wuv_ref, o_ref):
    for h in range(B_HEADS):
        o_ref[:, h * V_DIM:(h + 1) * V_DIM] = jnp.dot(
            olat_ref[h].astype(BF16), wuv_ref[h], preferred_element_type=F32).astype(BF16)


def _uv(olat_hm, wuv):
    nb = olat_hm.shape[1]
    return pl.pallas_call(
        _uv_kernel,
        in_specs=[_resident(), _resident()],
        out_specs=_resident(),
        out_shape=jax.ShapeDtypeStruct((nb, B_HEADS * V_DIM), BF16),
        name="mla_value_up",
    )(olat_hm, wuv)


PROMPT_TM = 256
MLA_TQ = 256
DECODE_PAGES = 32
DECODE_CHAINS = 4


def kernel(x_prompt, x_sample, cache_a_k, cache_a_v, cache_b_latent, cache_b_krope, page_table,
           rel_bias, norm_mix, norm_mlp, norm_final, a_w_qkv, a_w_o, a_sinks,
           b_w_in, b_q_norm, b_kv_norm, b_w_q_b, b_w_kv_b, b_w_o, mlp_w_up, mlp_w_down):
    batch, seq, _ = x_prompt.shape
    nb_s = x_sample.shape[0]
    past = page_table.shape[1] * cache_b_latent.shape[2]
    wb = cache_a_k.shape[2]

    xp = x_prompt.reshape(batch * seq, D_MODEL)
    xs = x_sample.reshape(nb_s, D_MODEL)
    row = lambda t: t.reshape(1, -1)

    w_qkv = a_w_qkv[0].astype(BF16)
    w_o_a = a_w_o[0].astype(BF16)
    w_up0, w_dn0 = mlp_w_up[0].astype(BF16), mlp_w_down[0].astype(BF16)
    bias = _bias_table(rel_bias)
    sinks = a_sinks[0]

    qp, kp, vp = _norm_qkv(xp, row(norm_mix[0]), w_qkv, PROMPT_TM)
    op = _swa_prompt(qp, kp, vp, bias, sinks, batch, seq)
    xp = _out_mlp(op, xp, w_o_a, row(norm_mlp[0]), w_up0, w_dn0, None, PROMPT_TM, False)

    qs, ks, vs = _norm_qkv(xs, row(norm_mix[0]), w_qkv, nb_s)
    bias_row = bias[:, WINDOW - 1, WINDOW:]
    os_, a_k_s, a_v_s = _swa_sample(qs, ks, vs,
                                    cache_a_k[0].reshape(nb_s, wb, NKV_A),
                                    cache_a_v[0].reshape(nb_s, wb, NKV_A), bias_row, sinks)
    xs = _out_mlp(os_, xs, w_o_a, row(norm_mlp[0]), w_up0, w_dn0, None, nb_s, False)

    w_in = b_w_in[0]
    zeros = lambda n: jnp.zeros((D_MODEL, n), F32)
    w_in_ext = jnp.concatenate([w_in[:, :Q_LORA + KV_LORA], zeros(ROPE_LANE0),
                                w_in[:, Q_LORA + KV_LORA:], zeros(LANES - ROPE_LANE0 - ROPE_DIM)],
                               axis=1).astype(BF16)
    qk_dim = NOPE_DIM + ROPE_DIM
    w_qb = jnp.pad(b_w_q_b[0].reshape(Q_LORA, B_HEADS, qk_dim),
                   ((0, 0), (0, 0), (0, LANES - qk_dim))).reshape(Q_LORA, B_HEADS * LANES).astype(BF16)
    w_kvb = b_w_kv_b[0]
    w_uk, w_uv = w_kvb[..., :NOPE_DIM], w_kvb[..., NOPE_DIM:]
    w_k = jnp.pad(w_uk, ((0, 0), (0, 0), (0, LANES - NOPE_DIM))).reshape(KV_LORA, B_HEADS * LANES).astype(BF16)
    w_vt = w_uv.reshape(KV_LORA, B_HEADS * V_DIM).T.astype(BF16)
    w_o_b = b_w_o[0].astype(BF16)
    w_up1, w_dn1 = mlp_w_up[1].astype(BF16), mlp_w_down[1].astype(BF16)
    proj_w = (row(norm_mix[1]), w_in_ext, row(b_q_norm[0]), row(b_kv_norm[0]), w_qb, w_k, w_vt)

    q_scale_p = (NOPE_DIM + ROPE_DIM) ** -0.5 * LOG2E
    tabs_p = _rope_tables(jnp.arange(seq, dtype=F32))
    qcat, kcat, vt, lat_p, krp_p = _mla_proj(xp, *proj_w, tabs_p, batch, seq, PROMPT_TM, seq // PROMPT_TM,
                                            q_scale_p)
    o_pm = _mla_attn(qcat, kcat, vt, MLA_TQ)
    yp = _out_mlp(o_pm, xp, w_o_b, row(norm_mlp[1]), w_up1, w_dn1, row(norm_final), PROMPT_TM, True)

    tabs_s = _rope_tables(jnp.full((nb_s,), past, F32))
    qcat_s, _, _, lat_s, krp_s = _mla_proj(xs, *proj_w, tabs_s, 1, nb_s, nb_s, 1, 1.0)
    w_uk_ext = jnp.pad(jnp.transpose(w_uk, (1, 2, 0)), ((0, 0), (0, LANES - NOPE_DIM), (0, 0))).astype(BF16)
    qlat = jnp.transpose(_absorb(qcat_s[0], w_uk_ext), (1, 0, 2))
    qr = jnp.transpose(qcat_s[0, :, :, ROPE_LANE0:ROPE_LANE0 + ROPE_DIM], (1, 0, 2))
    kr_s = krp_s[:, ROPE_LANE0:ROPE_LANE0 + ROPE_DIM]
    krt_pool = jnp.swapaxes(cache_b_krope[0], 1, 2)
    olat = _mla_decode(page_table, qlat, qr, lat_s.reshape(nb_s, 1, KV_LORA), kr_s.reshape(nb_s, 1, ROPE_DIM),
                       cache_b_latent[0], krt_pool, DECODE_PAGES, DECODE_CHAINS)
    o_s = _uv(jnp.transpose(olat, (1, 0, 2)), jnp.transpose(w_uv, (1, 0, 2)).astype(BF16))
    ys = _out_mlp(o_s, xs, w_o_b, row(norm_mlp[1]), w_up1, w_dn1, row(norm_final), nb_s, False)

    wbp = min(WINDOW, seq)
    k4 = kp.reshape(batch, seq, A_KV_HEADS, A_HEAD_DIM)[:, seq - wbp:]
    v4 = vp.reshape(batch, seq, A_KV_HEADS, A_HEAD_DIM)[:, seq - wbp:]
    kr_p = krp_p[:, ROPE_LANE0:ROPE_LANE0 + ROPE_DIM]
    return (
        yp.reshape(batch, seq, D_MODEL),
        ys.reshape(nb_s, 1, D_MODEL),
        k4[None], v4[None],
        lat_p.reshape(1, batch, seq, KV_LORA),
        kr_p.reshape(1, batch, seq, ROPE_DIM),
        a_k_s.reshape(1, nb_s, wb, A_KV_HEADS, A_HEAD_DIM),
        a_v_s.reshape(1, nb_s, wb, A_KV_HEADS, A_HEAD_DIM),
        lat_s.reshape(1, nb_s, 1, KV_LORA),
        kr_s.reshape(1, nb_s, 1, ROPE_DIM),
    )
```

```python
import functools
import math

import jax
import jax.numpy as jnp
from jax import lax
from jax.experimental import pallas as pl
from jax.experimental.pallas import tpu as pltpu

F32 = jnp.float32
BF16 = jnp.bfloat16

D_MODEL = 1024
A_HEADS = 16
A_KV_HEADS = 4
A_HEAD_DIM = 64
A_GROUP = A_HEADS // A_KV_HEADS
WINDOW = 128
N_BUCKETS = 32
MAX_EXACT = N_BUCKETS // 2
MAX_DISTANCE = 128
B_HEADS = 16
Q_LORA = 768
KV_LORA = 256
NOPE_DIM = 64
ROPE_DIM = 32
HALF_ROPE = ROPE_DIM // 2
V_DIM = 64
ROPE_THETA = 10000.0
D_FF = 4 * D_MODEL
EPS = 1e-6

LANES = 128
VMEM_LIMIT_BYTES = 56 * 1024 * 1024

NEG = -1e30
LOG2E = math.log2(math.e)

NT_DIMS = (((1,), (1,)), ((), ()))


def _rms(x, g):
    return x * lax.rsqrt(jnp.mean(x * x, axis=-1, keepdims=True) + EPS) * g


def _params(n_axes):
    return pltpu.CompilerParams(
        dimension_semantics=("arbitrary",) * n_axes,
        vmem_limit_bytes=VMEM_LIMIT_BYTES,
    )


def _resident():
    return pl.BlockSpec(memory_space=pltpu.VMEM)


NQ_A = A_HEADS * A_HEAD_DIM
NKV_A = A_KV_HEADS * A_HEAD_DIM


def _qkv_kernel(x_ref, g_ref, wqt_ref, wkv_ref, qt_ref, k_ref, v_ref, vt_ref):
    h = _rms(x_ref[...], g_ref[...]).astype(BF16)
    qt = lax.dot_general(wqt_ref[...], h, NT_DIMS, preferred_element_type=F32)
    qt_ref[...] = (qt * (A_HEAD_DIM ** -0.5 * LOG2E)).astype(BF16)
    kv = jnp.dot(h, wkv_ref[...], preferred_element_type=F32)
    k_ref[...] = kv[:, :NKV_A]
    v = kv[:, NKV_A:]
    v_ref[...] = v
    vt_ref[...] = v.T.astype(BF16)


def _norm_qkv(x, g, wqt_bf16, wkv_bf16, tm):
    m = x.shape[0]
    return pl.pallas_call(
        _qkv_kernel,
        grid=(m // tm,),
        in_specs=[
            pl.BlockSpec((tm, D_MODEL), lambda i: (i, 0)),
            _resident(),
            _resident(),
            _resident(),
        ],
        out_specs=[
            pl.BlockSpec((NQ_A, tm), lambda i: (0, i)),
            pl.BlockSpec((tm, NKV_A), lambda i: (i, 0)),
            pl.BlockSpec((tm, NKV_A), lambda i: (i, 0)),
            pl.BlockSpec((NKV_A, tm), lambda i: (0, i)),
        ],
        out_shape=[
            jax.ShapeDtypeStruct((NQ_A, m), BF16),
            jax.ShapeDtypeStruct((m, NKV_A), F32),
            jax.ShapeDtypeStruct((m, NKV_A), F32),
            jax.ShapeDtypeStruct((NKV_A, m), BF16),
        ],
        compiler_params=_params(1),
        name="norm_qkv",
    )(x, g, wqt_bf16, wkv_bf16)


def _t5_bucket(dist):
    d = jnp.maximum(dist, 0)
    df = jnp.maximum(d, 1).astype(F32)
    large = MAX_EXACT + (jnp.log(df / MAX_EXACT) / math.log(MAX_DISTANCE / MAX_EXACT)
                         * (N_BUCKETS - MAX_EXACT)).astype(jnp.int32)
    large = jnp.minimum(large, N_BUCKETS - 1)
    return jnp.where(d < MAX_EXACT, d, large)


HEADS_PER_SLAB = LANES // A_HEAD_DIM
N_KV_SLABS = NKV_A // LANES
HEADS_PER_KV_SLAB = HEADS_PER_SLAB * A_GROUP
TILE_COLS = HEADS_PER_KV_SLAB * WINDOW


def _bias_kernel(bucket_ref, rb_ref, out_ref):
    for var in range(bucket_ref.shape[0]):
        bucket = bucket_ref[var]
        for h in range(A_HEADS):
            acc = jnp.full(bucket.shape, NEG, F32)
            for b in range(N_BUCKETS):
                acc = jnp.where(bucket == b, rb_ref[b, h], acc)
            t, hh = divmod(h, HEADS_PER_KV_SLAB)
            out_ref[var, t, :, hh * WINDOW:(hh + 1) * WINDOW] = acc * LOG2E


def _bias_table(rel_bias):
    kj = jnp.arange(2 * WINDOW, dtype=jnp.int32)[:, None]
    qi = jnp.arange(WINDOW, dtype=jnp.int32)[None, :]
    dist = qi + WINDOW - kj
    valid = (dist >= 0) & (dist < WINDOW)
    bucket = jnp.where(valid, _t5_bucket(dist), -1)
    buckets = jnp.stack([bucket, jnp.where(kj >= WINDOW, bucket, -1)])
    return pl.pallas_call(
        _bias_kernel,
        in_specs=[_resident(), pl.BlockSpec(memory_space=pltpu.SMEM)],
        out_specs=_resident(),
        out_shape=jax.ShapeDtypeStruct((2, N_KV_SLABS, 2 * WINDOW, TILE_COLS), F32),
        name="t5_bias_table",
    )(buckets, rel_bias)


SWA_BLOCKS = 2
SWA_HEAD_ORDER = tuple(
    HEADS_PER_KV_SLAB * t + A_GROUP * half + i
    for t in range(N_KV_SLABS) for i in range(A_GROUP) for half in range(HEADS_PER_SLAB))


def _swa_prompt_kernel(qt_ref, kp_ref, kc_ref, vtp_ref, vtc_ref, bias0_ref, bias_ref, sink_ref, o_ref):
    k = jnp.concatenate([kp_ref[...], kc_ref[...]], axis=0).astype(BF16)
    vt = jnp.concatenate([vtp_ref[...], vtc_ref[...]], axis=1)
    top = lax.broadcasted_iota(jnp.int32, (LANES, WINDOW), 0) < A_HEAD_DIM
    zero = jnp.zeros((LANES, WINDOW), BF16)
    units = [(blk, t) for blk in range(SWA_BLOCKS) for t in range(N_KV_SLABS)]

    scores = []
    for blk, t in units:
        cols = slice(blk * WINDOW, (blk + 1) * WINDOW)
        qs = [qt_ref[(A_GROUP * t + i) * LANES:(A_GROUP * t + i + 1) * LANES, cols] for i in range(A_GROUP)]
        rhs = jnp.concatenate([jnp.where(top, x, zero) for x in qs] + [jnp.where(top, zero, x) for x in qs],
                              axis=1)
        kwin = k[blk * WINDOW:(blk + 2) * WINDOW, t * LANES:(t + 1) * LANES]
        scores.append(jnp.dot(kwin, rhs, preferred_element_type=F32))

    probs, inv = [], []
    for (blk, t), s in zip(units, scores):
        s = s + (bias0_ref[t] if blk == 0 else bias_ref[t])
        sink = sink_ref[t] * LOG2E
        m = jnp.maximum(jnp.max(s, axis=0, keepdims=True), sink)
        e = jnp.exp2(s - m)
        inv.append(1.0 / (jnp.sum(e, axis=0, keepdims=True) + jnp.exp2(sink - m)))
        probs.append(e.astype(BF16))

    outs = []
    for (blk, t), p in zip(units, probs):
        vwin = vt[t * LANES:(t + 1) * LANES, blk * WINDOW:(blk + 2) * WINDOW]
        outs.append(jnp.dot(vwin, p, preferred_element_type=F32))

    for (blk, t), ot, r in zip(units, outs, inv):
        ot = ot * r
        for i in range(A_GROUP):
            first = ot[:, i * WINDOW:(i + 1) * WINDOW]
            second = ot[:, (A_GROUP + i) * WINDOW:(A_GROUP + i + 1) * WINDOW]
            o_ref[blk * WINDOW:(blk + 1) * WINDOW, (A_GROUP * t + i) * LANES:(A_GROUP * t + i + 1) * LANES] = (
                jnp.where(top, first, second).T.astype(BF16))


def _swa_prompt(qt, k, vt, bias, sinks, batch, seq):
    tq = SWA_BLOCKS * WINDOW
    steps = seq // tq
    blocks = seq // WINDOW
    sink_rows = jnp.repeat(sinks, WINDOW).reshape(N_KV_SLABS, 1, TILE_COLS)
    prev_blk = lambda b, j: b * blocks + jnp.maximum(SWA_BLOCKS * j - 1, 0)
    tile = (None, N_KV_SLABS, 2 * WINDOW, TILE_COLS)
    return pl.pallas_call(
        _swa_prompt_kernel,
        grid=(batch, steps),
        in_specs=[
            pl.BlockSpec((NQ_A, tq), lambda b, j: (0, b * steps + j)),
            pl.BlockSpec((WINDOW, NKV_A), lambda b, j: (prev_blk(b, j), 0)),
            pl.BlockSpec((tq, NKV_A), lambda b, j: (b * steps + j, 0)),
            pl.BlockSpec((NKV_A, WINDOW), lambda b, j: (0, prev_blk(b, j))),
            pl.BlockSpec((NKV_A, tq), lambda b, j: (0, b * steps + j)),
            pl.BlockSpec(tile, lambda b, j: (jnp.where(j == 0, 1, 0), 0, 0, 0)),
            _resident(),
            _resident(),
        ],
        out_specs=pl.BlockSpec((tq, NQ_A), lambda b, j: (b * steps + j, 0)),
        out_shape=jax.ShapeDtypeStruct((batch * seq, NQ_A), BF16),
        compiler_params=_params(2),
        name="swa_prompt_attn",
    )(qt, k, k, vt, vt, bias, bias[0], sink_rows)


SAMPLE_ROWS_PER_STEP = 8


def _swa_sample_kernel(q_ref, kbuf_ref, vbuf_ref, knew_ref, vnew_ref, bias_ref, sink_ref,
                       o_ref, kout_ref, vout_ref):
    sink_all = sink_ref[...] * LOG2E
    n_rows = q_ref.shape[0]
    kb, vb = [], []
    for r in range(n_rows):
        kw = jnp.concatenate([kbuf_ref[r, 1:, :], knew_ref[r]], axis=0)
        vw = jnp.concatenate([vbuf_ref[r, 1:, :], vnew_ref[r]], axis=0)
        kout_ref[r] = kw
        vout_ref[r] = vw
        kb.append(kw.astype(BF16))
        vb.append(vw.astype(BF16))
    units = [(r, g) for r in range(n_rows) for g in range(A_KV_HEADS)]
    heads = lambda g: slice(g * A_GROUP, (g + 1) * A_GROUP)
    dims = lambda g: slice(g * A_HEAD_DIM, (g + 1) * A_HEAD_DIM)
    scores = [lax.dot_general(q_ref[r, heads(g), :].astype(BF16), kb[r][:, dims(g)], NT_DIMS,
                              preferred_element_type=F32) for r, g in units]
    probs, inv = [], []
    for (r, g), s in zip(units, scores):
        s = s + bias_ref[heads(g), :]
        sink = sink_all[heads(g), :]
        m = jnp.maximum(jnp.max(s, axis=-1, keepdims=True), sink)
        e = jnp.exp2(s - m)
        inv.append(1.0 / (jnp.sum(e, axis=-1, keepdims=True) + jnp.exp2(sink - m)))
        probs.append(e.astype(BF16))
    outs = [jnp.dot(p, vb[r][:, dims(g)], preferred_element_type=F32) for (r, g), p in zip(units, probs)]
    for (r, g), pv, rcp in zip(units, outs, inv):
        o_ref[r, heads(g), :] = pv * rcp


def _swa_sample(q, k_new, v_new, kbuf, vbuf, bias_row, sinks):
    nb, wb = kbuf.shape[0], kbuf.shape[1]
    rs = SAMPLE_ROWS_PER_STEP
    per_b = lambda b: (b, 0, 0)
    return pl.pallas_call(
        _swa_sample_kernel,
        grid=(nb // rs,),
        in_specs=[
            pl.BlockSpec((rs, A_HEADS, A_HEAD_DIM), per_b),
            pl.BlockSpec((rs, wb, NKV_A), per_b),
            pl.BlockSpec((rs, wb, NKV_A), per_b),
            pl.BlockSpec((rs, 1, NKV_A), per_b),
            pl.BlockSpec((rs, 1, NKV_A), per_b),
            _resident(),
            _resident(),
        ],
        out_specs=[
            pl.BlockSpec((rs, A_HEADS, A_HEAD_DIM), per_b),
            pl.BlockSpec((rs, wb, NKV_A), per_b),
            pl.BlockSpec((rs, wb, NKV_A), per_b),
        ],
        out_shape=[
            jax.ShapeDtypeStruct((nb, A_HEADS, A_HEAD_DIM), F32),
            jax.ShapeDtypeStruct((nb, wb, NKV_A), F32),
            jax.ShapeDtypeStruct((nb, wb, NKV_A), F32),
        ],
        compiler_params=_params(1),
        name="swa_sample_attn",
    )(q, kbuf, vbuf, k_new.reshape(nb, 1, NKV_A), v_new.reshape(nb, 1, NKV_A),
      bias_row, sinks.reshape(A_HEADS, 1))


FF_CHUNK = 1024


def _out_mlp_kernel(*refs, pair_major, final):
    if final:
        o_ref, x_ref, wo_ref, g_ref, wup_ref, wdn_ref, gf_ref, out_ref = refs
    else:
        o_ref, x_ref, wo_ref, g_ref, wup_ref, wdn_ref, out_ref = refs
    if pair_major:
        o = jnp.concatenate([o_ref[j] for j in range(o_ref.shape[0])], axis=-1)
    else:
        o = o_ref[...]
    x1 = x_ref[...] + jnp.dot(o, wo_ref[...], preferred_element_type=F32)
    h = _rms(x1, g_ref[...]).astype(BF16)
    acc = x1
    for c in range(D_FF // FF_CHUNK):
        sl = slice(c * FF_CHUNK, (c + 1) * FF_CHUNK)
        u = jnp.dot(h, wup_ref[:, sl], preferred_element_type=F32)
        u = jnp.square(jnp.maximum(u, 0.0)).astype(BF16)
        acc = acc + jnp.dot(u, wdn_ref[sl, :], preferred_element_type=F32)
    if final:
        acc = _rms(acc, gf_ref[...])
    out_ref[...] = acc


def _out_mlp(o, x, wo, g, wup, wdn, gf, tm, pair_major):
    m = x.shape[0]
    final = gf is not None
    if pair_major:
        tiles_per_group = o.shape[2] // tm
        o_spec = pl.BlockSpec((None, o.shape[1], tm, LANES),
                              lambda i: (i // tiles_per_group, 0, i % tiles_per_group, 0))
    else:
        o_spec = pl.BlockSpec((tm, D_MODEL), lambda i: (i, 0))
    row_spec = pl.BlockSpec((tm, D_MODEL), lambda i: (i, 0))
    in_specs = [o_spec, row_spec, _resident(), _resident(), _resident(), _resident()]
    args = [o, x, wo, g, wup, wdn]
    if final:
        in_specs.append(_resident())
        args.append(gf)
    return pl.pallas_call(
        functools.partial(_out_mlp_kernel, pair_major=pair_major, final=final),
        grid=(m // tm,),
        in_specs=in_specs,
        out_specs=row_spec,
        out_shape=jax.ShapeDtypeStruct((m, D_MODEL), F32),
        compiler_params=_params(1),
        name="out_proj_mlp",
    )(*args)


N_PAIRS = B_HEADS // 2
ROPE_LANE0 = NOPE_DIM
C_IN_EXT = Q_LORA + KV_LORA + LANES


def _rope_slab(t, c, s1, s2):
    return t * c + pltpu.roll(t, HALF_ROPE, 1) * s1 + pltpu.roll(t, LANES - HALF_ROPE, 1) * s2


def _mla_proj_kernel(x_ref, g_ref, win_ref, qn_ref, kvn_ref, wqb_ref, wk_ref, wvt_ref,
                     c_ref, s1_ref, s2_ref,
                     qcat_ref, kcat_ref, vt_ref, lat_ref, krp_ref, *, q_scale):
    h = _rms(x_ref[...], g_ref[...]).astype(BF16)
    c = jnp.dot(h, win_ref[...], preferred_element_type=F32)
    cq = _rms(c[:, :Q_LORA], qn_ref[...]).astype(BF16)
    lat = _rms(c[:, Q_LORA:Q_LORA + KV_LORA], kvn_ref[...])
    lat_ref[...] = lat
    cos, s1, s2 = c_ref[...], s1_ref[...], s2_ref[...]
    krp = _rope_slab(c[:, Q_LORA + KV_LORA:], cos, s1, s2)
    krp_ref[...] = krp
    q = jnp.dot(cq, wqb_ref[...], preferred_element_type=F32)
    latb = lat.astype(BF16)
    kn = jnp.dot(latb, wk_ref[...], preferred_element_type=F32)
    vt = lax.dot_general(wvt_ref[...], latb, NT_DIMS, preferred_element_type=F32)
    for hh in range(B_HEADS):
        sl = slice(hh * LANES, (hh + 1) * LANES)
        qh = _rope_slab(q[:, sl], cos, s1, s2)
        if q_scale != 1.0:
            qh = qh * q_scale
        qcat_ref[hh] = qh.astype(BF16)
        kcat_ref[hh] = (kn[:, sl] + krp).astype(BF16)
    for j in range(N_PAIRS):
        vt_ref[j] = vt[j * LANES:(j + 1) * LANES, :].astype(BF16)


def _mla_proj(x, g, win, qn, kvn, wqb, wk, wvt, tabs, groups, rows, tm, tab_blocks, q_scale):
    m = x.shape[0]
    tiles = rows // tm
    head_map = lambda i: (i // tiles, 0, i % tiles, 0)
    tab_spec = pl.BlockSpec((tm, LANES), lambda i: (i % tab_blocks, 0))
    return pl.pallas_call(
        functools.partial(_mla_proj_kernel, q_scale=q_scale),
        grid=(m // tm,),
        in_specs=[pl.BlockSpec((tm, D_MODEL), lambda i: (i, 0))] + [_resident()] * 7 + [tab_spec] * 3,
        out_specs=[
            pl.BlockSpec((None, B_HEADS, tm, LANES), head_map),
            pl.BlockSpec((None, B_HEADS, tm, LANES), head_map),
            pl.BlockSpec((None, N_PAIRS, LANES, tm), lambda i: (i // tiles, 0, 0, i % tiles)),
            pl.BlockSpec((tm, KV_LORA), lambda i: (i, 0)),
            pl.BlockSpec((tm, LANES), lambda i: (i, 0)),
        ],
        out_shape=[
            jax.ShapeDtypeStruct((groups, B_HEADS, rows, LANES), BF16),
            jax.ShapeDtypeStruct((groups, B_HEADS, rows, LANES), BF16),
            jax.ShapeDtypeStruct((groups, N_PAIRS, LANES, rows), BF16),
            jax.ShapeDtypeStruct((m, KV_LORA), F32),
            jax.ShapeDtypeStruct((m, LANES), F32),
        ],
        compiler_params=_params(1),
        name="mla_proj",
    )(x, g, win, qn, kvn, wqb, wk, wvt, *tabs)


def _rope_tables(pos):
    inv = ROPE_THETA ** (-jnp.arange(0, ROPE_DIM, 2, dtype=F32) / ROPE_DIM)
    ang = pos[:, None] * inv[None, :]
    cos, sin = jnp.cos(ang), jnp.sin(ang)
    s = pos.shape[0]
    ones = jnp.ones((s, ROPE_LANE0), F32)
    z16 = jnp.zeros((s, HALF_ROPE), F32)
    z64 = jnp.zeros((s, ROPE_LANE0), F32)
    z32 = jnp.zeros((s, LANES - ROPE_LANE0 - ROPE_DIM), F32)
    c = jnp.concatenate([ones, cos, cos, z32], axis=1)
    s1 = jnp.concatenate([z64, z16, sin, z32], axis=1)
    s2 = jnp.concatenate([z64, -sin, z16, z32], axis=1)
    return c, s1, s2


HEADS_PER_BODY = 8
REDUCE_ROWS = 64


def _col_reduce(x, op):
    n, w = x.shape
    part = op(x.reshape(n // REDUCE_ROWS, REDUCE_ROWS, w), axis=0)
    return op(part, axis=0, keepdims=True)


def _mla_attn_kernel(q_ref, k_ref, vt_ref, o_ref, *, tq, n_tiles):
    qi = pl.program_id(1)
    key = lax.broadcasted_iota(jnp.int32, (tq, tq), 0)
    qry = lax.broadcasted_iota(jnp.int32, (tq, tq), 1)
    causal = key <= qry

    def scores(h, n_keys):
        k = k_ref[h, pl.ds(0, n_keys), :]
        return lax.dot_general(k, q_ref[h], NT_DIMS, preferred_element_type=F32)

    def attend(st, j, n_keys):
        diag = jnp.where(causal, st[n_keys - tq:, :], NEG)
        if n_keys > tq:
            full = st[:n_keys - tq, :]
            m = jnp.maximum(_col_reduce(full, jnp.max), _col_reduce(diag, jnp.max))
            p = jnp.concatenate([jnp.exp2(full - m), jnp.exp2(diag - m)], axis=0)
        else:
            m = _col_reduce(diag, jnp.max)
            p = jnp.exp2(diag - m)
        l = _col_reduce(p, jnp.sum)
        ot = jnp.dot(vt_ref[j, :, pl.ds(0, n_keys)], p.astype(BF16), preferred_element_type=F32)
        return ot / l

    for c in range(n_tiles):
        @pl.when(qi == c)
        def _(c=c):
            n_keys = (c + 1) * tq

            def body(jj, carry):
                heads = [HEADS_PER_BODY * jj + u for u in range(HEADS_PER_BODY)]
                sts = [scores(h, n_keys) for h in heads]
                outs = [attend(st, (HEADS_PER_BODY // 2) * jj + u // 2, n_keys) for u, st in enumerate(sts)]
                for u in range(0, HEADS_PER_BODY, 2):
                    ot = jnp.concatenate([outs[u][:V_DIM], outs[u + 1][V_DIM:]], axis=0)
                    o_ref[(HEADS_PER_BODY // 2) * jj + u // 2] = ot.T.astype(BF16)
                return carry

            lax.fori_loop(0, B_HEADS // HEADS_PER_BODY, body, 0)


def _mla_attn(qcat, kcat, vt, tq):
    batch, _, seq, _ = qcat.shape
    n_tiles = seq // tq
    return pl.pallas_call(
        functools.partial(_mla_attn_kernel, tq=tq, n_tiles=n_tiles),
        grid=(batch, n_tiles),
        in_specs=[
            pl.BlockSpec((None, B_HEADS, tq, LANES), lambda b, i: (b, 0, i, 0)),
            pl.BlockSpec((None, B_HEADS, seq, LANES), lambda b, i: (b, 0, 0, 0)),
            pl.BlockSpec((None, N_PAIRS, LANES, seq), lambda b, i: (b, 0, 0, 0)),
        ],
        out_specs=pl.BlockSpec((None, N_PAIRS, tq, LANES), lambda b, i: (b, 0, i, 0)),
        out_shape=jax.ShapeDtypeStruct((batch, N_PAIRS, seq, LANES), BF16),
        compiler_params=_params(2),
        name="mla_prompt_attn",
    )(qcat, kcat, vt)


def _absorb_kernel(qcat_ref, wuk_ref, qlat_ref):
    for h in range(B_HEADS):
        qlat_ref[h] = jnp.dot(qcat_ref[h], wuk_ref[h], preferred_element_type=F32).astype(BF16)


def _absorb(qcat, wuk_ext):
    nb = qcat.shape[1]
    return pl.pallas_call(
        _absorb_kernel,
        in_specs=[_resident(), _resident()],
        out_specs=_resident(),
        out_shape=jax.ShapeDtypeStruct((B_HEADS, nb, KV_LORA), BF16),
        name="mla_absorb_q",
    )(qcat, wuk_ext)


def _decode_kernel(pt_ref, qlat_ref, qr_ref, latnew_ref, krnew_ref, lat_hbm, krt_hbm, o_ref,
                   latbuf, krbuf, sem, m_sc, l_sc, acc_sc,
                   *, pages_per_step, steps_per_batch, n_chains, scale):
    b = pl.program_id(0)
    c = pl.program_id(1)
    step = b * steps_per_batch + c
    last_step = step + 1 == pl.num_programs(0) * steps_per_batch
    slot = step % 2
    page_size = latbuf.shape[2]

    def chunk_copies(bb, cc, sl):
        copies = []
        for p in range(pages_per_step):
            page = pt_ref[bb, cc * pages_per_step + p]
            copies.append(pltpu.make_async_copy(lat_hbm.at[page], latbuf.at[sl, p], sem.at[0, sl]))
            copies.append(pltpu.make_async_copy(krt_hbm.at[page], krbuf.at[sl, p], sem.at[1, sl]))
        return copies

    @pl.when(step == 0)
    def _():
        for cp in chunk_copies(b, c, slot):
            cp.start()

    wrap = c + 1 == steps_per_batch
    nb = jnp.where(last_step, b, jnp.where(wrap, b + 1, b))
    nc = jnp.where(last_step, c, jnp.where(wrap, 0, c + 1))
    for cp in chunk_copies(nb, nc, 1 - slot):
        cp.start()

    qlat = qlat_ref[...]
    qr = qr_ref[...]

    @pl.when(c == 0)
    def _():
        ln = latnew_ref[...].astype(BF16).astype(F32)
        kn = krnew_ref[...].astype(BF16).astype(F32)
        s_new = (jnp.sum(qlat.astype(F32) * ln, axis=-1, keepdims=True)
                 + jnp.sum(qr.astype(F32) * kn, axis=-1, keepdims=True)) * scale
        m_sc[...] = jnp.full(m_sc.shape, NEG, F32)
        l_sc[...] = jnp.zeros(l_sc.shape, F32)
        acc_sc[...] = jnp.zeros(acc_sc.shape, F32)
        m_sc[0] = s_new
        l_sc[0] = jnp.ones_like(s_new)
        acc_sc[0] = jnp.broadcast_to(ln, acc_sc.shape[1:])

    for cp in chunk_copies(b, c, slot):
        cp.wait()

    ppc = pages_per_step // n_chains
    chains = range(n_chains)
    s_rope = [jnp.concatenate(
        [jnp.dot(qr, krbuf[slot, i * ppc + p].astype(BF16), preferred_element_type=F32)
         for p in range(ppc)], axis=1) for i in chains]
    lat = [latbuf[slot, pl.ds(i * ppc, ppc)].reshape(ppc * page_size, KV_LORA).astype(BF16) for i in chains]
    s = [(lax.dot_general(qlat, lat[i], NT_DIMS, preferred_element_type=F32) + s_rope[i]) * scale
         for i in chains]
    m_old = [m_sc[i] for i in chains]
    m_new = [jnp.maximum(m_old[i], jnp.max(s[i], axis=-1, keepdims=True)) for i in chains]
    alpha = [jnp.exp(m_old[i] - m_new[i]) for i in chains]
    p = [jnp.exp(s[i] - m_new[i]) for i in chains]
    pv = [jnp.dot(p[i].astype(BF16), lat[i], preferred_element_type=F32) for i in chains]
    for i in chains:
        l_sc[i] = alpha[i] * l_sc[i] + jnp.sum(p[i], axis=-1, keepdims=True)
        acc_sc[i] = alpha[i] * acc_sc[i] + pv[i]
        m_sc[i] = m_new[i]

    @pl.when(c == steps_per_batch - 1)
    def _():
        m = m_sc[0]
        for i in range(1, n_chains):
            m = jnp.maximum(m, m_sc[i])
        l = jnp.zeros_like(m)
        acc = jnp.zeros(acc_sc.shape[1:], F32)
        for i in range(n_chains):
            w = jnp.exp(m_sc[i] - m)
            l = l + w * l_sc[i]
            acc = acc + w * acc_sc[i]
        o_ref[...] = acc / l

    @pl.when(last_step)
    def _():
        for cp in chunk_copies(nb, nc, 1 - slot):
            cp.wait()


def _mla_decode(page_table, qlat, qr, lat_new, kr_new, lat_pool, krt_pool, pages_per_step, n_chains):
    nb, n_pages = page_table.shape
    page_size = lat_pool.shape[1]
    steps_per_batch = n_pages // pages_per_step
    scale = (NOPE_DIM + ROPE_DIM) ** -0.5
    per_b = lambda b, c, pt: (b, 0, 0)
    grid_spec = pltpu.PrefetchScalarGridSpec(
        num_scalar_prefetch=1,
        grid=(nb, steps_per_batch),
        in_specs=[
            pl.BlockSpec((None, B_HEADS, KV_LORA), per_b),
            pl.BlockSpec((None, B_HEADS, ROPE_DIM), per_b),
            pl.BlockSpec((None, 1, KV_LORA), per_b),
            pl.BlockSpec((None, 1, ROPE_DIM), per_b),
            pl.BlockSpec(memory_space=pl.ANY),
            pl.BlockSpec(memory_space=pl.ANY),
        ],
        out_specs=pl.BlockSpec((None, B_HEADS, KV_LORA), per_b),
        scratch_shapes=[
            pltpu.VMEM((2, pages_per_step, page_size, KV_LORA), F32),
            pltpu.VMEM((2, pages_per_step, ROPE_DIM, page_size), F32),
            pltpu.SemaphoreType.DMA((2, 2)),
            pltpu.VMEM((n_chains, B_HEADS, 1), F32),
            pltpu.VMEM((n_chains, B_HEADS, 1), F32),
            pltpu.VMEM((n_chains, B_HEADS, KV_LORA), F32),
        ],
    )
    return pl.pallas_call(
        functools.partial(_decode_kernel, pages_per_step=pages_per_step,
                          steps_per_batch=steps_per_batch, n_chains=n_chains, scale=scale),
        grid_spec=grid_spec,
        out_shape=jax.ShapeDtypeStruct((nb, B_HEADS, KV_LORA), F32),
        compiler_params=_params(2),
        name="mla_decode_attn",
    )(page_table, qlat, qr, lat_new, kr_new, lat_pool, krt_pool)


def _uv_kernel(olat_ref, wuv_ref, o_ref):
    for h in range(B_HEADS):
        o_ref[:, h * V_DIM:(h + 1) * V_DIM] = jnp.dot(
            olat_ref[h].astype(BF16), wuv_ref[h], preferred_element_type=F32).astype(BF16)


def _uv(olat_hm, wuv):
    nb = olat_hm.shape[1]
    return pl.pallas_call(
        _uv_kernel,
        in_specs=[_resident(), _resident()],
        out_specs=_resident(),
        out_shape=jax.ShapeDtypeStruct((nb, B_HEADS * V_DIM), BF16),
        name="mla_value_up",
    )(olat_hm, wuv)


PROMPT_TM = 256
MLA_TQ = 256
DECODE_PAGES = 32
DECODE_CHAINS = 4


def kernel(x_prompt, x_sample, cache_a_k, cache_a_v, cache_b_latent, cache_b_krope, page_table,
           rel_bias, norm_mix, norm_mlp, norm_final, a_w_qkv, a_w_o, a_sinks,
           b_w_in, b_q_norm, b_kv_norm, b_w_q_b, b_w_kv_b, b_w_o, mlp_w_up, mlp_w_down):
    batch, seq, _ = x_prompt.shape
    nb_s = x_sample.shape[0]
    past = page_table.shape[1] * cache_b_latent.shape[2]
    wb = cache_a_k.shape[2]

    xp = x_prompt.reshape(batch * seq, D_MODEL)
    xs = x_sample.reshape(nb_s, D_MODEL)
    row = lambda t: t.reshape(1, -1)

    order = jnp.array(SWA_HEAD_ORDER)
    inverse = jnp.argsort(order)
    w_qt = (a_w_qkv[0][:, :NQ_A].reshape(D_MODEL, A_HEADS, A_HEAD_DIM)[:, order]
            .reshape(D_MODEL, NQ_A).T.astype(BF16))
    w_kv = a_w_qkv[0][:, NQ_A:].astype(BF16)
    w_o_a = a_w_o[0].reshape(A_HEADS, A_HEAD_DIM, D_MODEL)[order].reshape(NQ_A, D_MODEL).astype(BF16)
    w_up0, w_dn0 = mlp_w_up[0].astype(BF16), mlp_w_down[0].astype(BF16)
    bias = _bias_table(rel_bias)
    sinks = a_sinks[0]

    qtp, kp, vp, vtp = _norm_qkv(xp, row(norm_mix[0]), w_qt, w_kv, PROMPT_TM)
    op = _swa_prompt(qtp, kp, vtp, bias, sinks, batch, seq)
    xp = _out_mlp(op, xp, w_o_a, row(norm_mlp[0]), w_up0, w_dn0, None, PROMPT_TM, False)

    qts, ks, vs, _ = _norm_qkv(xs, row(norm_mix[0]), w_qt, w_kv, nb_s)
    qs3 = qts.T.astype(F32).reshape(nb_s, A_HEADS, A_HEAD_DIM)[:, inverse]
    bias_row = jnp.transpose(
        bias[0].reshape(N_KV_SLABS, 2 * WINDOW, HEADS_PER_KV_SLAB, WINDOW)[:, WINDOW:, :, WINDOW - 1],
        (0, 2, 1)).reshape(A_HEADS, WINDOW)
    os3, a_k_s, a_v_s = _swa_sample(qs3, ks, vs,
                                    cache_a_k[0].reshape(nb_s, wb, NKV_A),
                                    cache_a_v[0].reshape(nb_s, wb, NKV_A), bias_row, sinks)
    os_ = os3[:, order].reshape(nb_s, NQ_A).astype(BF16)
    xs = _out_mlp(os_, xs, w_o_a, row(norm_mlp[0]), w_up0, w_dn0, None, nb_s, False)

    w_in = b_w_in[0]
    zeros = lambda n: jnp.zeros((D_MODEL, n), F32)
    w_in_ext = jnp.concatenate([w_in[:, :Q_LORA + KV_LORA], zeros(ROPE_LANE0),
                                w_in[:, Q_LORA + KV_LORA:], zeros(LANES - ROPE_LANE0 - ROPE_DIM)],
                               axis=1).astype(BF16)
    qk_dim = NOPE_DIM + ROPE_DIM
    w_qb = jnp.pad(b_w_q_b[0].reshape(Q_LORA, B_HEADS, qk_dim),
                   ((0, 0), (0, 0), (0, LANES - qk_dim))).reshape(Q_LORA, B_HEADS * LANES).astype(BF16)
    w_kvb = b_w_kv_b[0]
    w_uk, w_uv = w_kvb[..., :NOPE_DIM], w_kvb[..., NOPE_DIM:]
    w_k = jnp.pad(w_uk, ((0, 0), (0, 0), (0, LANES - NOPE_DIM))).reshape(KV_LORA, B_HEADS * LANES).astype(BF16)
    w_vt = w_uv.reshape(KV_LORA, B_HEADS * V_DIM).T.astype(BF16)
    w_o_b = b_w_o[0].astype(BF16)
    w_up1, w_dn1 = mlp_w_up[1].astype(BF16), mlp_w_down[1].astype(BF16)
    proj_w = (row(norm_mix[1]), w_in_ext, row(b_q_norm[0]), row(b_kv_norm[0]), w_qb, w_k, w_vt)

    q_scale_p = (NOPE_DIM + ROPE_DIM) ** -0.5 * LOG2E
    tabs_p = _rope_tables(jnp.arange(seq, dtype=F32))
    qcat, kcat, vt, lat_p, krp_p = _mla_proj(xp, *proj_w, tabs_p, batch, seq, PROMPT_TM, seq // PROMPT_TM,
                                            q_scale_p)
    o_pm = _mla_attn(qcat, kcat, vt, MLA_TQ)
    yp = _out_mlp(o_pm, xp, w_o_b, row(norm_mlp[1]), w_up1, w_dn1, row(norm_final), PROMPT_TM, True)

    tabs_s = _rope_tables(jnp.full((nb_s,), past, F32))
    qcat_s, _, _, lat_s, krp_s = _mla_proj(xs, *proj_w, tabs_s, 1, nb_s, nb_s, 1, 1.0)
    w_uk_ext = jnp.pad(jnp.transpose(w_uk, (1, 2, 0)), ((0, 0), (0, LANES - NOPE_DIM), (0, 0))).astype(BF16)
    qlat = jnp.transpose(_absorb(qcat_s[0], w_uk_ext), (1, 0, 2))
    qr = jnp.transpose(qcat_s[0, :, :, ROPE_LANE0:ROPE_LANE0 + ROPE_DIM], (1, 0, 2))
    kr_s = krp_s[:, ROPE_LANE0:ROPE_LANE0 + ROPE_DIM]
    krt_pool = jnp.swapaxes(cache_b_krope[0], 1, 2)
    olat = _mla_decode(page_table, qlat, qr, lat_s.reshape(nb_s, 1, KV_LORA), kr_s.reshape(nb_s, 1, ROPE_DIM),
                       cache_b_latent[0], krt_pool, DECODE_PAGES, DECODE_CHAINS)
    o_s = _uv(jnp.transpose(olat, (1, 0, 2)), jnp.transpose(w_uv, (1, 0, 2)).astype(BF16))
    ys = _out_mlp(o_s, xs, w_o_b, row(norm_mlp[1]), w_up1, w_dn1, row(norm_final), nb_s, False)

    wbp = min(WINDOW, seq)
    k4 = kp.reshape(batch, seq, A_KV_HEADS, A_HEAD_DIM)[:, seq - wbp:]
    v4 = vp.reshape(batch, seq, A_KV_HEADS, A_HEAD_DIM)[:, seq - wbp:]
    kr_p = krp_p[:, ROPE_LANE0:ROPE_LANE0 + ROPE_DIM]
    return (
        yp.reshape(batch, seq, D_MODEL),
        ys.reshape(nb_s, 1, D_MODEL),
        k4[None], v4[None],
        lat_p.reshape(1, batch, seq, KV_LORA),
        kr_p.reshape(1, batch, seq, ROPE_DIM),
        a_k_s.reshape(1, nb_s, wb, A_KV_HEADS, A_HEAD_DIM),
        a_v_s.reshape(1, nb_s, wb, A_KV_HEADS, A_HEAD_DIM),
        lat_s.reshape(1, nb_s, 1, KV_LORA),
        kr_s.reshape(1, nb_s, 1, ROPE_DIM),
    )
```

```python
import functools
import math

import jax
import jax.numpy as jnp
import numpy as np
from jax import lax
from jax.experimental import pallas as pl
from jax.experimental.pallas import tpu as pltpu

F32 = jnp.float32
BF16 = jnp.bfloat16

D_MODEL = 1024
A_HEADS = 16
A_KV_HEADS = 4
A_HEAD_DIM = 64
A_GROUP = A_HEADS // A_KV_HEADS
WINDOW = 128
N_BUCKETS = 32
MAX_EXACT = N_BUCKETS // 2
MAX_DISTANCE = 128
B_HEADS = 16
Q_LORA = 768
KV_LORA = 256
NOPE_DIM = 64
ROPE_DIM = 32
HALF_ROPE = ROPE_DIM // 2
V_DIM = 64
ROPE_THETA = 10000.0
D_FF = 4 * D_MODEL
EPS = 1e-6

LANES = 128
VMEM_LIMIT_BYTES = 56 * 1024 * 1024

NEG = -1e30
LOG2E = math.log2(math.e)

NT_DIMS = (((1,), (1,)), ((), ()))


def _rms(x, g):
    return x * lax.rsqrt(jnp.mean(x * x, axis=-1, keepdims=True) + EPS) * g


def _params(n_axes):
    return pltpu.CompilerParams(
        dimension_semantics=("arbitrary",) * n_axes,
        vmem_limit_bytes=VMEM_LIMIT_BYTES,
    )


def _resident():
    return pl.BlockSpec(memory_space=pltpu.VMEM)


NQ_A = A_HEADS * A_HEAD_DIM
NKV_A = A_KV_HEADS * A_HEAD_DIM


def _qkv_kernel(x_ref, g_ref, wqt_ref, wkvt_ref, *out_refs, tail_rows, tiles_per_group):
    h = _rms(x_ref[...], g_ref[...]).astype(BF16)
    qt = lax.dot_general(wqt_ref[...], h, NT_DIMS, preferred_element_type=F32)
    out_refs[0][...] = (qt * (A_HEAD_DIM ** -0.5 * LOG2E)).astype(BF16)
    kvt = lax.dot_general(wkvt_ref[...], h, NT_DIMS, preferred_element_type=F32)
    if tail_rows:
        _, k_ref, vt_ref, ktail_ref, vtail_ref = out_refs
        k_ref[...] = kvt[:NKV_A].T
        vt_ref[...] = kvt[NKV_A:].astype(BF16)

        @pl.when(pl.program_id(0) % tiles_per_group == tiles_per_group - 1)
        def _():
            ktail_ref[...] = kvt[:NKV_A, kvt.shape[1] - tail_rows:]
            vtail_ref[...] = kvt[NKV_A:, kvt.shape[1] - tail_rows:]
    else:
        _, kt_ref, vt_ref = out_refs
        kt_ref[...] = kvt[:NKV_A]
        vt_ref[...] = kvt[NKV_A:]


def _norm_qkv(x, g, wqt_bf16, wkvt_bf16, tm, groups=1, tail_rows=0):
    m = x.shape[0]
    tiles_per_group = m // groups // tm
    col_blk = lambda i: (0, i)
    in_specs = [pl.BlockSpec((tm, D_MODEL), lambda i: (i, 0)), _resident(), _resident(), _resident()]
    qt_spec = pl.BlockSpec((NQ_A, tm), col_blk)
    qt_shape = jax.ShapeDtypeStruct((NQ_A, m), BF16)
    if tail_rows:
        tail_spec = pl.BlockSpec((None, NKV_A, tail_rows), lambda i: (i // tiles_per_group, 0, 0))
        tail_shape = jax.ShapeDtypeStruct((groups, NKV_A, tail_rows), F32)
        out_specs = [qt_spec, pl.BlockSpec((tm, NKV_A), lambda i: (i, 0)),
                     pl.BlockSpec((NKV_A, tm), col_blk), tail_spec, tail_spec]
        out_shape = [qt_shape, jax.ShapeDtypeStruct((m, NKV_A), F32),
                     jax.ShapeDtypeStruct((NKV_A, m), BF16), tail_shape, tail_shape]
    else:
        out_specs = [qt_spec, pl.BlockSpec((NKV_A, tm), col_blk), pl.BlockSpec((NKV_A, tm), col_blk)]
        out_shape = [qt_shape, jax.ShapeDtypeStruct((NKV_A, m), F32), jax.ShapeDtypeStruct((NKV_A, m), F32)]
    return pl.pallas_call(
        functools.partial(_qkv_kernel, tail_rows=tail_rows, tiles_per_group=tiles_per_group),
        grid=(m // tm,),
        in_specs=in_specs,
        out_specs=out_specs,
        out_shape=out_shape,
        compiler_params=_params(1),
        name="norm_qkv",
    )(x, g, wqt_bf16, wkvt_bf16)


def _t5_bucket(dist):
    d = np.maximum(dist, 0)
    df = np.maximum(d, 1).astype(np.float32)
    large = MAX_EXACT + (np.log(df / np.float32(MAX_EXACT)) / np.float32(math.log(MAX_DISTANCE / MAX_EXACT))
                         * np.float32(N_BUCKETS - MAX_EXACT)).astype(np.int32)
    large = np.minimum(large, N_BUCKETS - 1)
    return np.where(d < MAX_EXACT, d, large)


HEADS_PER_SLAB = LANES // A_HEAD_DIM
N_KV_SLABS = NKV_A // LANES
HEADS_PER_KV_SLAB = HEADS_PER_SLAB * A_GROUP
TILE_COLS = HEADS_PER_KV_SLAB * WINDOW


def _bias_kernel(bucket_ref, rb_ref, out_ref):
    for var in range(bucket_ref.shape[0]):
        bucket = bucket_ref[var]
        for h in range(A_HEADS):
            acc = jnp.full(bucket.shape, NEG, F32)
            for b in range(N_BUCKETS):
                acc = jnp.where(bucket == b, rb_ref[b, h], acc)
            t, hh = divmod(h, HEADS_PER_KV_SLAB)
            out_ref[var, t, :, hh * WINDOW:(hh + 1) * WINDOW] = acc * LOG2E


def _bias_table(rel_bias):
    kj = np.arange(2 * WINDOW, dtype=np.int32)[:, None]
    qi = np.arange(WINDOW, dtype=np.int32)[None, :]
    dist = qi + WINDOW - kj
    valid = (dist >= 0) & (dist < WINDOW)
    bucket = np.where(valid, _t5_bucket(dist), -1).astype(np.int32)
    buckets = jnp.asarray(np.stack([bucket, np.where(kj >= WINDOW, bucket, -1).astype(np.int32)]))
    return pl.pallas_call(
        _bias_kernel,
        in_specs=[_resident(), pl.BlockSpec(memory_space=pltpu.SMEM)],
        out_specs=_resident(),
        out_shape=jax.ShapeDtypeStruct((2, N_KV_SLABS, 2 * WINDOW, TILE_COLS), F32),
        name="t5_bias_table",
    )(buckets, rel_bias)


SWA_BLOCKS = 2
SWA_HEAD_ORDER = tuple(
    HEADS_PER_KV_SLAB * t + A_GROUP * half + i
    for t in range(N_KV_SLABS) for i in range(A_GROUP) for half in range(HEADS_PER_SLAB))


def _swa_prompt_kernel(qt_ref, kp_ref, kc_ref, vtp_ref, vtc_ref, bias0_ref, bias_ref, sink_ref, o_ref):
    k = jnp.concatenate([kp_ref[...], kc_ref[...]], axis=0).astype(BF16)
    vt = jnp.concatenate([vtp_ref[...], vtc_ref[...]], axis=1)
    top = lax.broadcasted_iota(jnp.int32, (LANES, WINDOW), 0) < A_HEAD_DIM
    zero = jnp.zeros((LANES, WINDOW), BF16)
    units = [(blk, t) for blk in range(SWA_BLOCKS) for t in range(N_KV_SLABS)]

    scores = []
    for blk, t in units:
        cols = slice(blk * WINDOW, (blk + 1) * WINDOW)
        qs = [qt_ref[(A_GROUP * t + i) * LANES:(A_GROUP * t + i + 1) * LANES, cols] for i in range(A_GROUP)]
        rhs = jnp.concatenate([jnp.where(top, x, zero) for x in qs] + [jnp.where(top, zero, x) for x in qs],
                              axis=1)
        kwin = k[blk * WINDOW:(blk + 2) * WINDOW, t * LANES:(t + 1) * LANES]
        scores.append(jnp.dot(kwin, rhs, preferred_element_type=F32))

    probs, inv = [], []
    for (blk, t), s in zip(units, scores):
        s = s + (bias0_ref[t] if blk == 0 else bias_ref[t])
        sink = sink_ref[t] * LOG2E
        m = jnp.maximum(jnp.max(s, axis=0, keepdims=True), sink)
        e = jnp.exp2(s - m)
        inv.append(1.0 / (jnp.sum(e, axis=0, keepdims=True) + jnp.exp2(sink - m)))
        probs.append(e.astype(BF16))

    outs = []
    for (blk, t), p in zip(units, probs):
        vwin = vt[t * LANES:(t + 1) * LANES, blk * WINDOW:(blk + 2) * WINDOW]
        outs.append(jnp.dot(vwin, p, preferred_element_type=F32))

    for (blk, t), ot, r in zip(units, outs, inv):
        ot = ot * r
        for i in range(A_GROUP):
            first = ot[:, i * WINDOW:(i + 1) * WINDOW]
            second = ot[:, (A_GROUP + i) * WINDOW:(A_GROUP + i + 1) * WINDOW]
            o_ref[blk * WINDOW:(blk + 1) * WINDOW, (A_GROUP * t + i) * LANES:(A_GROUP * t + i + 1) * LANES] = (
                jnp.where(top, first, second).T.astype(BF16))


def _swa_prompt(qt, k, vt, bias, sinks, batch, seq):
    tq = SWA_BLOCKS * WINDOW
    steps = seq // tq
    blocks = seq // WINDOW
    sink_rows = jnp.repeat(sinks, WINDOW).reshape(N_KV_SLABS, 1, TILE_COLS)
    prev_blk = lambda b, j: b * blocks + jnp.maximum(SWA_BLOCKS * j - 1, 0)
    tile = (None, N_KV_SLABS, 2 * WINDOW, TILE_COLS)
    return pl.pallas_call(
        _swa_prompt_kernel,
        grid=(batch, steps),
        in_specs=[
            pl.BlockSpec((NQ_A, tq), lambda b, j: (0, b * steps + j)),
            pl.BlockSpec((WINDOW, NKV_A), lambda b, j: (prev_blk(b, j), 0)),
            pl.BlockSpec((tq, NKV_A), lambda b, j: (b * steps + j, 0)),
            pl.BlockSpec((NKV_A, WINDOW), lambda b, j: (0, prev_blk(b, j))),
            pl.BlockSpec((NKV_A, tq), lambda b, j: (0, b * steps + j)),
            pl.BlockSpec(tile, lambda b, j: (jnp.where(j == 0, 1, 0), 0, 0, 0)),
            _resident(),
            _resident(),
        ],
        out_specs=pl.BlockSpec((tq, NQ_A), lambda b, j: (b * steps + j, 0)),
        out_shape=jax.ShapeDtypeStruct((batch * seq, NQ_A), BF16),
        compiler_params=_params(2),
        name="swa_prompt_attn",
    )(qt, k, k, vt, vt, bias, bias[0], sink_rows)


def _swa_sample_kernel(q_ref, kbt_ref, vbt_ref, knt_ref, vnt_ref, bias_ref, sink_ref,
                       o_ref, kout_ref, vout_ref):
    sink_all = sink_ref[...] * LOG2E
    n_rows, _, wb = kbt_ref.shape
    newest = lax.broadcasted_iota(jnp.int32, (NKV_A, wb), 1) == wb - 1
    kb, vb = [], []
    for r in range(n_rows):
        kw = jnp.where(newest, knt_ref[:, r:r + 1], pltpu.roll(kbt_ref[r], wb - 1, 1))
        vw = jnp.where(newest, vnt_ref[:, r:r + 1], pltpu.roll(vbt_ref[r], wb - 1, 1))
        kout_ref[r] = kw
        vout_ref[r] = vw
        kb.append(kw.astype(BF16))
        vb.append(vw.astype(BF16))
    units = [(r, g) for r in range(n_rows) for g in range(A_KV_HEADS)]
    heads = lambda g: slice(g * A_GROUP, (g + 1) * A_GROUP)
    dims = lambda g: slice(g * A_HEAD_DIM, (g + 1) * A_HEAD_DIM)
    scores = [jnp.dot(q_ref[r, heads(g), :].astype(BF16), kb[r][dims(g), :], preferred_element_type=F32)
              for r, g in units]
    probs, inv = [], []
    for (r, g), s in zip(units, scores):
        s = s + bias_ref[heads(g), :]
        sink = sink_all[heads(g), :]
        m = jnp.maximum(jnp.max(s, axis=-1, keepdims=True), sink)
        e = jnp.exp2(s - m)
        inv.append(1.0 / (jnp.sum(e, axis=-1, keepdims=True) + jnp.exp2(sink - m)))
        probs.append(e.astype(BF16))
    outs = [lax.dot_general(p, vb[r][dims(g), :], NT_DIMS, preferred_element_type=F32)
            for (r, g), p in zip(units, probs)]
    for (r, g), pv, rcp in zip(units, outs, inv):
        o_ref[r, heads(g), :] = pv * rcp


def _swa_sample(q, kt_new, vt_new, kbuf_t, vbuf_t, bias_row, sinks):
    nb = kbuf_t.shape[0]
    return pl.pallas_call(
        _swa_sample_kernel,
        in_specs=[_resident()] * 7,
        out_specs=[_resident()] * 3,
        out_shape=[
            jax.ShapeDtypeStruct((nb, A_HEADS, A_HEAD_DIM), F32),
            jax.ShapeDtypeStruct(kbuf_t.shape, F32),
            jax.ShapeDtypeStruct(vbuf_t.shape, F32),
        ],
        compiler_params=pltpu.CompilerParams(vmem_limit_bytes=VMEM_LIMIT_BYTES),
        name="swa_sample_attn",
    )(q, kbuf_t, vbuf_t, kt_new, vt_new, bias_row, sinks.reshape(A_HEADS, 1))


FF_CHUNK = 1024


def _out_mlp_kernel(*refs, pair_major, final, layer):
    if final:
        o_ref, x_ref, wo_ref, g_ref, wup_hbm, wdn_hbm, gf_ref, out_ref, wup_ref, wdn_ref, sem = refs
    else:
        o_ref, x_ref, wo_ref, g_ref, wup_hbm, wdn_hbm, out_ref, wup_ref, wdn_ref, sem = refs

    @pl.when(pl.program_id(0) == 0)
    def _():
        copies = [pltpu.make_async_copy(wup_hbm.at[layer], wup_ref, sem.at[0]),
                  pltpu.make_async_copy(wdn_hbm.at[layer], wdn_ref, sem.at[1])]
        for cp in copies:
            cp.start()
        for cp in copies:
            cp.wait()

    if pair_major:
        o = jnp.concatenate([o_ref[j] for j in range(o_ref.shape[0])], axis=-1)
    else:
        o = o_ref[...]
    x1 = x_ref[...] + jnp.dot(o, wo_ref[...], preferred_element_type=F32)
    h = _rms(x1, g_ref[...]).astype(BF16)
    acc = x1
    for c in range(D_FF // FF_CHUNK):
        sl = slice(c * FF_CHUNK, (c + 1) * FF_CHUNK)
        u = jnp.dot(h, wup_ref[:, sl], preferred_element_type=F32)
        u = jnp.square(jnp.maximum(u, 0.0)).astype(BF16)
        acc = acc + jnp.dot(u, wdn_ref[sl, :], preferred_element_type=F32)
    if final:
        acc = _rms(acc, gf_ref[...])
    out_ref[...] = acc


def _out_mlp(o, x, wo, g, wup_all, wdn_all, layer, gf, tm, pair_major):
    m = x.shape[0]
    final = gf is not None
    if pair_major:
        tiles_per_group = o.shape[2] // tm
        o_spec = pl.BlockSpec((None, o.shape[1], tm, LANES),
                              lambda i: (i // tiles_per_group, 0, i % tiles_per_group, 0))
    else:
        o_spec = pl.BlockSpec((tm, D_MODEL), lambda i: (i, 0))
    row_spec = pl.BlockSpec((tm, D_MODEL), lambda i: (i, 0))
    in_hbm = pl.BlockSpec(memory_space=pl.ANY)
    in_specs = [o_spec, row_spec, _resident(), _resident(), in_hbm, in_hbm]
    args = [o, x, wo, g, wup_all, wdn_all]
    if final:
        in_specs.append(_resident())
        args.append(gf)
    return pl.pallas_call(
        functools.partial(_out_mlp_kernel, pair_major=pair_major, final=final, layer=layer),
        grid=(m // tm,),
        in_specs=in_specs,
        out_specs=row_spec,
        out_shape=jax.ShapeDtypeStruct((m, D_MODEL), F32),
        scratch_shapes=[
            pltpu.VMEM((D_MODEL, D_FF), BF16),
            pltpu.VMEM((D_FF, D_MODEL), BF16),
            pltpu.SemaphoreType.DMA((2,)),
        ],
        compiler_params=_params(1),
        name="out_proj_mlp",
    )(*args)


N_PAIRS = B_HEADS // 2
ROPE_LANE0 = NOPE_DIM
C_IN_EXT = Q_LORA + KV_LORA + LANES


def _rope_slab(t, c, s1, s2):
    return t * c + pltpu.roll(t, HALF_ROPE, 1) * s1 + pltpu.roll(t, LANES - HALF_ROPE, 1) * s2


def _mla_proj_kernel(x_ref, g_ref, win_ref, qn_ref, kvn_ref, wqb_ref, wk_ref, wvt_ref,
                     c_ref, s1_ref, s2_ref,
                     qcat_ref, kcat_ref, vt_ref, lat_ref, krt_ref, *, q_scale):
    h = _rms(x_ref[...], g_ref[...]).astype(BF16)
    c = jnp.dot(h, win_ref[...], preferred_element_type=F32)
    cq = _rms(c[:, :Q_LORA], qn_ref[...]).astype(BF16)
    lat = _rms(c[:, Q_LORA:Q_LORA + KV_LORA], kvn_ref[...])
    lat_ref[...] = lat
    cos, s1, s2 = c_ref[...], s1_ref[...], s2_ref[...]
    krp = _rope_slab(c[:, Q_LORA + KV_LORA:], cos, s1, s2)
    krt_ref[...] = krp.T[ROPE_LANE0:ROPE_LANE0 + ROPE_DIM, :]
    q = jnp.dot(cq, wqb_ref[...], preferred_element_type=F32)
    latb = lat.astype(BF16)
    kn = jnp.dot(latb, wk_ref[...], preferred_element_type=F32)
    vt = lax.dot_general(wvt_ref[...], latb, NT_DIMS, preferred_element_type=F32)
    for hh in range(B_HEADS):
        sl = slice(hh * LANES, (hh + 1) * LANES)
        qh = _rope_slab(q[:, sl], cos, s1, s2)
        if q_scale != 1.0:
            qh = qh * q_scale
        qcat_ref[hh] = qh.astype(BF16)
        kcat_ref[hh] = (kn[:, sl] + krp).astype(BF16)
    for j in range(N_PAIRS):
        vt_ref[j] = vt[j * LANES:(j + 1) * LANES, :].astype(BF16)


def _mla_proj(x, g, win, qn, kvn, wqb, wk, wvt, tabs, groups, rows, tm, tab_blocks, q_scale):
    m = x.shape[0]
    tiles = rows // tm
    head_map = lambda i: (i // tiles, 0, i % tiles, 0)
    tab_spec = pl.BlockSpec((tm, LANES), lambda i: (i % tab_blocks, 0))
    return pl.pallas_call(
        functools.partial(_mla_proj_kernel, q_scale=q_scale),
        grid=(m // tm,),
        in_specs=[pl.BlockSpec((tm, D_MODEL), lambda i: (i, 0))] + [_resident()] * 7 + [tab_spec] * 3,
        out_specs=[
            pl.BlockSpec((None, B_HEADS, tm, LANES), head_map),
            pl.BlockSpec((None, B_HEADS, tm, LANES), head_map),
            pl.BlockSpec((None, N_PAIRS, LANES, tm), lambda i: (i // tiles, 0, 0, i % tiles)),
            pl.BlockSpec((tm, KV_LORA), lambda i: (i, 0)),
            pl.BlockSpec((None, ROPE_DIM, tm), lambda i: (i // tiles, 0, i % tiles)),
        ],
        out_shape=[
            jax.ShapeDtypeStruct((groups, B_HEADS, rows, LANES), BF16),
            jax.ShapeDtypeStruct((groups, B_HEADS, rows, LANES), BF16),
            jax.ShapeDtypeStruct((groups, N_PAIRS, LANES, rows), BF16),
            jax.ShapeDtypeStruct((m, KV_LORA), F32),
            jax.ShapeDtypeStruct((groups, ROPE_DIM, rows), F32),
        ],
        compiler_params=_params(1),
        name="mla_proj",
    )(x, g, win, qn, kvn, wqb, wk, wvt, *tabs)


def _rope_tables(pos):
    pos = np.asarray(pos, np.float32)
    inv = np.float32(ROPE_THETA) ** (-np.arange(0, ROPE_DIM, 2, dtype=np.float32) / np.float32(ROPE_DIM))
    ang = pos[:, None] * inv[None, :]
    cos, sin = np.cos(ang), np.sin(ang)
    s = pos.shape[0]
    ones = np.ones((s, ROPE_LANE0), np.float32)
    z16 = np.zeros((s, HALF_ROPE), np.float32)
    z64 = np.zeros((s, ROPE_LANE0), np.float32)
    z32 = np.zeros((s, LANES - ROPE_LANE0 - ROPE_DIM), np.float32)
    c = np.concatenate([ones, cos, cos, z32], axis=1)
    s1 = np.concatenate([z64, z16, sin, z32], axis=1)
    s2 = np.concatenate([z64, -sin, z16, z32], axis=1)
    return jnp.asarray(c), jnp.asarray(s1), jnp.asarray(s2)


HEADS_PER_BODY = 8
REDUCE_ROWS = 64


def _col_reduce(x, op):
    n, w = x.shape
    part = op(x.reshape(n // REDUCE_ROWS, REDUCE_ROWS, w), axis=0)
    return op(part, axis=0, keepdims=True)


def _mla_attn_kernel(q_ref, k_ref, vt_ref, o_ref, *, tq, n_tiles):
    qi = pl.program_id(1)
    key = lax.broadcasted_iota(jnp.int32, (tq, tq), 0)
    qry = lax.broadcasted_iota(jnp.int32, (tq, tq), 1)
    causal = key <= qry

    def scores(h, n_keys):
        k = k_ref[h, pl.ds(0, n_keys), :]
        return lax.dot_general(k, q_ref[h], NT_DIMS, preferred_element_type=F32)

    def attend(st, j, n_keys):
        diag = jnp.where(causal, st[n_keys - tq:, :], NEG)
        if n_keys > tq:
            full = st[:n_keys - tq, :]
            m = jnp.maximum(_col_reduce(full, jnp.max), _col_reduce(diag, jnp.max))
            p = jnp.concatenate([jnp.exp2(full - m), jnp.exp2(diag - m)], axis=0)
        else:
            m = _col_reduce(diag, jnp.max)
            p = jnp.exp2(diag - m)
        l = _col_reduce(p, jnp.sum)
        ot = jnp.dot(vt_ref[j, :, pl.ds(0, n_keys)], p.astype(BF16), preferred_element_type=F32)
        return ot / l

    for c in range(n_tiles):
        @pl.when(qi == c)
        def _(c=c):
            n_keys = (c + 1) * tq

            def body(jj, carry):
                heads = [HEADS_PER_BODY * jj + u for u in range(HEADS_PER_BODY)]
                sts = [scores(h, n_keys) for h in heads]
                outs = [attend(st, (HEADS_PER_BODY // 2) * jj + u // 2, n_keys) for u, st in enumerate(sts)]
                for u in range(0, HEADS_PER_BODY, 2):
                    ot = jnp.concatenate([outs[u][:V_DIM], outs[u + 1][V_DIM:]], axis=0)
                    o_ref[(HEADS_PER_BODY // 2) * jj + u // 2] = ot.T.astype(BF16)
                return carry

            lax.fori_loop(0, B_HEADS // HEADS_PER_BODY, body, 0)


def _mla_attn(qcat, kcat, vt, tq):
    batch, _, seq, _ = qcat.shape
    n_tiles = seq // tq
    return pl.pallas_call(
        functools.partial(_mla_attn_kernel, tq=tq, n_tiles=n_tiles),
        grid=(batch, n_tiles),
        in_specs=[
            pl.BlockSpec((None, B_HEADS, tq, LANES), lambda b, i: (b, 0, i, 0)),
            pl.BlockSpec((None, B_HEADS, seq, LANES), lambda b, i: (b, 0, 0, 0)),
            pl.BlockSpec((None, N_PAIRS, LANES, seq), lambda b, i: (b, 0, 0, 0)),
        ],
        out_specs=pl.BlockSpec((None, N_PAIRS, tq, LANES), lambda b, i: (b, 0, i, 0)),
        out_shape=jax.ShapeDtypeStruct((batch, N_PAIRS, seq, LANES), BF16),
        compiler_params=_params(2),
        name="mla_prompt_attn",
    )(qcat, kcat, vt)


def _absorb_kernel(qcat_ref, wuk_ref, qlat_ref):
    for h in range(B_HEADS):
        qlat_ref[h] = jnp.dot(qcat_ref[h], wuk_ref[h], preferred_element_type=F32).astype(BF16)


def _absorb(qcat, wuk_ext):
    nb = qcat.shape[1]
    return pl.pallas_call(
        _absorb_kernel,
        in_specs=[_resident(), _resident()],
        out_specs=_resident(),
        out_shape=jax.ShapeDtypeStruct((B_HEADS, nb, KV_LORA), BF16),
        name="mla_absorb_q",
    )(qcat, wuk_ext)


def _decode_kernel(pt_ref, qlat_ref, qr_ref, latnew_ref, krnew_ref, lat_hbm, krt_hbm, o_ref,
                   latbuf, krbuf, sem, m_sc, l_sc, acc_sc,
                   *, pages_per_step, steps_per_batch, n_chains, scale):
    b = pl.program_id(0)
    c = pl.program_id(1)
    step = b * steps_per_batch + c
    last_step = step + 1 == pl.num_programs(0) * steps_per_batch
    slot = step % 2
    page_size = latbuf.shape[2]

    def chunk_copies(bb, cc, sl):
        copies = []
        for p in range(pages_per_step):
            page = pt_ref[bb, cc * pages_per_step + p]
            copies.append(pltpu.make_async_copy(lat_hbm.at[page], latbuf.at[sl, p], sem.at[0, sl]))
            copies.append(pltpu.make_async_copy(krt_hbm.at[page], krbuf.at[sl, p], sem.at[1, sl]))
        return copies

    @pl.when(step == 0)
    def _():
        for cp in chunk_copies(b, c, slot):
            cp.start()

    wrap = c + 1 == steps_per_batch
    nb = jnp.where(last_step, b, jnp.where(wrap, b + 1, b))
    nc = jnp.where(last_step, c, jnp.where(wrap, 0, c + 1))
    for cp in chunk_copies(nb, nc, 1 - slot):
        cp.start()

    qlat = qlat_ref[...]
    qr = qr_ref[...]

    @pl.when(c == 0)
    def _():
        ln = latnew_ref[...].astype(BF16).astype(F32)
        kn = krnew_ref[...].astype(BF16).astype(F32)
        s_new = (jnp.sum(qlat.astype(F32) * ln, axis=-1, keepdims=True)
                 + jnp.sum(qr.astype(F32) * kn, axis=-1, keepdims=True)) * scale
        m_sc[...] = jnp.full(m_sc.shape, NEG, F32)
        l_sc[...] = jnp.zeros(l_sc.shape, F32)
        acc_sc[...] = jnp.zeros(acc_sc.shape, F32)
        m_sc[0] = s_new
        l_sc[0] = jnp.ones_like(s_new)
        acc_sc[0] = jnp.broadcast_to(ln, acc_sc.shape[1:])

    for cp in chunk_copies(b, c, slot):
        cp.wait()

    ppc = pages_per_step // n_chains
    chains = range(n_chains)
    s_rope = [jnp.concatenate(
        [jnp.dot(qr, krbuf[slot, i * ppc + p].astype(BF16), preferred_element_type=F32)
         for p in range(ppc)], axis=1) for i in chains]
    lat = [latbuf[slot, pl.ds(i * ppc, ppc)].reshape(ppc * page_size, KV_LORA).astype(BF16) for i in chains]
    s = [(lax.dot_general(qlat, lat[i], NT_DIMS, preferred_element_type=F32) + s_rope[i]) * scale
         for i in chains]
    m_old = [m_sc[i] for i in chains]
    m_new = [jnp.maximum(m_old[i], jnp.max(s[i], axis=-1, keepdims=True)) for i in chains]
    alpha = [jnp.exp(m_old[i] - m_new[i]) for i in chains]
    p = [jnp.exp(s[i] - m_new[i]) for i in chains]
    pv = [jnp.dot(p[i].astype(BF16), lat[i], preferred_element_type=F32) for i in chains]
    for i in chains:
        l_sc[i] = alpha[i] * l_sc[i] + jnp.sum(p[i], axis=-1, keepdims=True)
        acc_sc[i] = alpha[i] * acc_sc[i] + pv[i]
        m_sc[i] = m_new[i]

    @pl.when(c == steps_per_batch - 1)
    def _():
        m = m_sc[0]
        for i in range(1, n_chains):
            m = jnp.maximum(m, m_sc[i])
        l = jnp.zeros_like(m)
        acc = jnp.zeros(acc_sc.shape[1:], F32)
        for i in range(n_chains):
            w = jnp.exp(m_sc[i] - m)
            l = l + w * l_sc[i]
            acc = acc + w * acc_sc[i]
        o_ref[...] = acc / l

    @pl.when(last_step)
    def _():
        for cp in chunk_copies(nb, nc, 1 - slot):
            cp.wait()


def _mla_decode(page_table, qlat, qr, lat_new, kr_new, lat_pool, krt_pool, pages_per_step, n_chains):
    nb, n_pages = page_table.shape
    page_size = lat_pool.shape[1]
    steps_per_batch = n_pages // pages_per_step
    scale = (NOPE_DIM + ROPE_DIM) ** -0.5
    per_b = lambda b, c, pt: (b, 0, 0)
    grid_spec = pltpu.PrefetchScalarGridSpec(
        num_scalar_prefetch=1,
        grid=(nb, steps_per_batch),
        in_specs=[
            pl.BlockSpec((None, B_HEADS, KV_LORA), per_b),
            pl.BlockSpec((None, B_HEADS, ROPE_DIM), per_b),
            pl.BlockSpec((None, 1, KV_LORA), per_b),
            pl.BlockSpec((None, 1, ROPE_DIM), per_b),
            pl.BlockSpec(memory_space=pl.ANY),
            pl.BlockSpec(memory_space=pl.ANY),
        ],
        out_specs=pl.BlockSpec((None, B_HEADS, KV_LORA), per_b),
        scratch_shapes=[
            pltpu.VMEM((2, pages_per_step, page_size, KV_LORA), F32),
            pltpu.VMEM((2, pages_per_step, ROPE_DIM, page_size), F32),
            pltpu.SemaphoreType.DMA((2, 2)),
            pltpu.VMEM((n_chains, B_HEADS, 1), F32),
            pltpu.VMEM((n_chains, B_HEADS, 1), F32),
            pltpu.VMEM((n_chains, B_HEADS, KV_LORA), F32),
        ],
    )
    return pl.pallas_call(
        functools.partial(_decode_kernel, pages_per_step=pages_per_step,
                          steps_per_batch=steps_per_batch, n_chains=n_chains, scale=scale),
        grid_spec=grid_spec,
        out_shape=jax.ShapeDtypeStruct((nb, B_HEADS, KV_LORA), F32),
        compiler_params=_params(2),
        name="mla_decode_attn",
    )(page_table, qlat, qr, lat_new, kr_new, lat_pool, krt_pool)


def _uv_kernel(olat_ref, wuv_ref, o_ref):
    for h in range(B_HEADS):
        o_ref[:, h * V_DIM:(h + 1) * V_DIM] = jnp.dot(
            olat_ref[h].astype(BF16), wuv_ref[h], preferred_element_type=F32).astype(BF16)


def _uv(olat_hm, wuv):
    nb = olat_hm.shape[1]
    return pl.pallas_call(
        _uv_kernel,
        in_specs=[_resident(), _resident()],
        out_specs=_resident(),
        out_shape=jax.ShapeDtypeStruct((nb, B_HEADS * V_DIM), BF16),
        name="mla_value_up",
    )(olat_hm, wuv)


PROMPT_TM = 256
MLA_TQ = 256
DECODE_PAGES = 32
DECODE_CHAINS = 4


def kernel(x_prompt, x_sample, cache_a_k, cache_a_v, cache_b_latent, cache_b_krope, page_table,
           rel_bias, norm_mix, norm_mlp, norm_final, a_w_qkv, a_w_o, a_sinks,
           b_w_in, b_q_norm, b_kv_norm, b_w_q_b, b_w_kv_b, b_w_o, mlp_w_up, mlp_w_down):
    batch, seq, _ = x_prompt.shape
    nb_s = x_sample.shape[0]
    past = page_table.shape[1] * cache_b_latent.shape[2]
    wb = cache_a_k.shape[2]

    xp = x_prompt.reshape(batch * seq, D_MODEL)
    xs = x_sample.reshape(nb_s, D_MODEL)
    row = lambda t: t.reshape(1, -1)

    order = jnp.array(SWA_HEAD_ORDER)
    inverse = jnp.argsort(order)
    w_qt = (a_w_qkv[0][:, :NQ_A].reshape(D_MODEL, A_HEADS, A_HEAD_DIM)[:, order]
            .reshape(D_MODEL, NQ_A).T.astype(BF16))
    w_kvt = a_w_qkv[0][:, NQ_A:].T.astype(BF16)
    w_o_a = a_w_o[0].reshape(A_HEADS, A_HEAD_DIM, D_MODEL)[order].reshape(NQ_A, D_MODEL).astype(BF16)
    w_up, w_dn = mlp_w_up.astype(BF16), mlp_w_down.astype(BF16)
    bias = _bias_table(rel_bias)
    sinks = a_sinks[0]
    wbp = min(WINDOW, seq)

    qtp, kp, vtp, kt_tail, vt_tail = _norm_qkv(xp, row(norm_mix[0]), w_qt, w_kvt, PROMPT_TM, batch, wbp)
    op = _swa_prompt(qtp, kp, vtp, bias, sinks, batch, seq)
    xp = _out_mlp(op, xp, w_o_a, row(norm_mlp[0]), w_up, w_dn, 0, None, PROMPT_TM, False)

    qts, kts, vts = _norm_qkv(xs, row(norm_mix[0]), w_qt, w_kvt, nb_s)
    qs3 = qts.T.astype(F32).reshape(nb_s, A_HEADS, A_HEAD_DIM)[:, inverse]
    bias_row = jnp.transpose(
        bias[0].reshape(N_KV_SLABS, 2 * WINDOW, HEADS_PER_KV_SLAB, WINDOW)[:, WINDOW:, :, WINDOW - 1],
        (0, 2, 1)).reshape(A_HEADS, WINDOW)
    to_t = lambda c: jnp.transpose(c[0], (0, 2, 3, 1)).reshape(nb_s, NKV_A, wb)
    from_t = lambda t, n, w: jnp.transpose(t.reshape(n, A_KV_HEADS, A_HEAD_DIM, w), (0, 3, 1, 2))[None]
    os3, a_k_st, a_v_st = _swa_sample(qs3, kts, vts, to_t(cache_a_k), to_t(cache_a_v), bias_row, sinks)
    os_ = os3[:, order].reshape(nb_s, NQ_A).astype(BF16)
    xs = _out_mlp(os_, xs, w_o_a, row(norm_mlp[0]), w_up, w_dn, 0, None, nb_s, False)

    w_in = b_w_in[0]
    zeros = lambda n: jnp.zeros((D_MODEL, n), F32)
    w_in_ext = jnp.concatenate([w_in[:, :Q_LORA + KV_LORA], zeros(ROPE_LANE0),
                                w_in[:, Q_LORA + KV_LORA:], zeros(LANES - ROPE_LANE0 - ROPE_DIM)],
                               axis=1).astype(BF16)
    qk_dim = NOPE_DIM + ROPE_DIM
    w_qb = jnp.pad(b_w_q_b[0].reshape(Q_LORA, B_HEADS, qk_dim),
                   ((0, 0), (0, 0), (0, LANES - qk_dim))).reshape(Q_LORA, B_HEADS * LANES).astype(BF16)
    w_kvb = b_w_kv_b[0]
    w_uk, w_uv = w_kvb[..., :NOPE_DIM], w_kvb[..., NOPE_DIM:]
    w_k = jnp.pad(w_uk, ((0, 0), (0, 0), (0, LANES - NOPE_DIM))).reshape(KV_LORA, B_HEADS * LANES).astype(BF16)
    w_vt = w_uv.reshape(KV_LORA, B_HEADS * V_DIM).T.astype(BF16)
    w_o_b = b_w_o[0].astype(BF16)
    proj_w = (row(norm_mix[1]), w_in_ext, row(b_q_norm[0]), row(b_kv_norm[0]), w_qb, w_k, w_vt)

    q_scale_p = (NOPE_DIM + ROPE_DIM) ** -0.5 * LOG2E
    tabs_p = _rope_tables(np.arange(seq))
    qcat, kcat, vt, lat_p, krt_p = _mla_proj(xp, *proj_w, tabs_p, batch, seq, PROMPT_TM, seq // PROMPT_TM,
                                            q_scale_p)
    o_pm = _mla_attn(qcat, kcat, vt, MLA_TQ)
    yp = _out_mlp(o_pm, xp, w_o_b, row(norm_mlp[1]), w_up, w_dn, 1, row(norm_final), PROMPT_TM, True)

    tabs_s = _rope_tables(np.full((nb_s,), past))
    qcat_s, _, _, lat_s, krt_s = _mla_proj(xs, *proj_w, tabs_s, 1, nb_s, nb_s, 1, 1.0)
    w_uk_ext = jnp.pad(jnp.transpose(w_uk, (1, 2, 0)), ((0, 0), (0, LANES - NOPE_DIM), (0, 0))).astype(BF16)
    qlat = jnp.transpose(_absorb(qcat_s[0], w_uk_ext), (1, 0, 2))
    qr = jnp.transpose(qcat_s[0, :, :, ROPE_LANE0:ROPE_LANE0 + ROPE_DIM], (1, 0, 2))
    kr_s = krt_s[0].T
    krt_pool = jnp.swapaxes(cache_b_krope[0], 1, 2)
    olat = _mla_decode(page_table, qlat, qr, lat_s.reshape(nb_s, 1, KV_LORA), kr_s.reshape(nb_s, 1, ROPE_DIM),
                       cache_b_latent[0], krt_pool, DECODE_PAGES, DECODE_CHAINS)
    o_s = _uv(jnp.transpose(olat, (1, 0, 2)), jnp.transpose(w_uv, (1, 0, 2)).astype(BF16))
    ys = _out_mlp(o_s, xs, w_o_b, row(norm_mlp[1]), w_up, w_dn, 1, row(norm_final), nb_s, False)

    k4 = from_t(kt_tail, batch, wbp)
    v4 = from_t(vt_tail, batch, wbp)
    kr_p = jnp.swapaxes(krt_p, 1, 2)
    return (
        yp.reshape(batch, seq, D_MODEL),
        ys.reshape(nb_s, 1, D_MODEL),
        k4, v4,
        lat_p.reshape(1, batch, seq, KV_LORA),
        kr_p.reshape(1, batch, seq, ROPE_DIM),
        from_t(a_k_st, nb_s, wb),
        from_t(a_v_st, nb_s, wb),
        lat_s.reshape(1, nb_s, 1, KV_LORA),
        kr_s.reshape(1, nb_s, 1, ROPE_DIM),
    )
```

```python
import functools
import math

import jax
import jax.numpy as jnp
import numpy as np
from jax import lax
from jax.experimental import pallas as pl
from jax.experimental.pallas import tpu as pltpu

F32 = jnp.float32
BF16 = jnp.bfloat16

D_MODEL = 1024
A_HEADS = 16
A_KV_HEADS = 4
A_HEAD_DIM = 64
A_GROUP = A_HEADS // A_KV_HEADS
WINDOW = 128
N_BUCKETS = 32
MAX_EXACT = N_BUCKETS // 2
MAX_DISTANCE = 128
B_HEADS = 16
Q_LORA = 768
KV_LORA = 256
NOPE_DIM = 64
ROPE_DIM = 32
HALF_ROPE = ROPE_DIM // 2
V_DIM = 64
ROPE_THETA = 10000.0
D_FF = 4 * D_MODEL
EPS = 1e-6

LANES = 128
VMEM_LIMIT_BYTES = 56 * 1024 * 1024

NEG = -1e30
LOG2E = math.log2(math.e)

NT_DIMS = (((1,), (1,)), ((), ()))


def _rms(x, g):
    return x * lax.rsqrt(jnp.mean(x * x, axis=-1, keepdims=True) + EPS) * g


def _params(n_axes):
    return pltpu.CompilerParams(
        dimension_semantics=("arbitrary",) * n_axes,
        vmem_limit_bytes=VMEM_LIMIT_BYTES,
    )


def _resident():
    return pl.BlockSpec(memory_space=pltpu.VMEM)


NQ_A = A_HEADS * A_HEAD_DIM
NKV_A = A_KV_HEADS * A_HEAD_DIM


def _qkv_kernel(x_ref, g_ref, wqt_ref, wkvt_ref, *out_refs, tail_rows, tiles_per_group):
    h = _rms(x_ref[...], g_ref[...]).astype(BF16)
    qt = lax.dot_general(wqt_ref[...], h, NT_DIMS, preferred_element_type=F32)
    out_refs[0][...] = (qt * (A_HEAD_DIM ** -0.5 * LOG2E)).astype(BF16)
    kvt = lax.dot_general(wkvt_ref[...], h, NT_DIMS, preferred_element_type=F32)
    if tail_rows:
        _, k_ref, vt_ref, ktail_ref, vtail_ref = out_refs
        k_ref[...] = kvt[:NKV_A].T
        vt_ref[...] = kvt[NKV_A:].astype(BF16)

        @pl.when(pl.program_id(0) % tiles_per_group == tiles_per_group - 1)
        def _():
            ktail_ref[...] = kvt[:NKV_A, kvt.shape[1] - tail_rows:]
            vtail_ref[...] = kvt[NKV_A:, kvt.shape[1] - tail_rows:]
    else:
        _, kt_ref, vt_ref = out_refs
        kt_ref[...] = kvt[:NKV_A]
        vt_ref[...] = kvt[NKV_A:]


def _norm_qkv(x, g, wqt_bf16, wkvt_bf16, tm, groups=1, tail_rows=0):
    m = x.shape[0]
    tiles_per_group = m // groups // tm
    col_blk = lambda i: (0, i)
    in_specs = [pl.BlockSpec((tm, D_MODEL), lambda i: (i, 0)), _resident(), _resident(), _resident()]
    qt_spec = pl.BlockSpec((NQ_A, tm), col_blk)
    qt_shape = jax.ShapeDtypeStruct((NQ_A, m), BF16)
    if tail_rows:
        tail_spec = pl.BlockSpec((None, NKV_A, tail_rows), lambda i: (i // tiles_per_group, 0, 0))
        tail_shape = jax.ShapeDtypeStruct((groups, NKV_A, tail_rows), F32)
        out_specs = [qt_spec, pl.BlockSpec((tm, NKV_A), lambda i: (i, 0)),
                     pl.BlockSpec((NKV_A, tm), col_blk), tail_spec, tail_spec]
        out_shape = [qt_shape, jax.ShapeDtypeStruct((m, NKV_A), F32),
                     jax.ShapeDtypeStruct((NKV_A, m), BF16), tail_shape, tail_shape]
    else:
        out_specs = [qt_spec, pl.BlockSpec((NKV_A, tm), col_blk), pl.BlockSpec((NKV_A, tm), col_blk)]
        out_shape = [qt_shape, jax.ShapeDtypeStruct((NKV_A, m), F32), jax.ShapeDtypeStruct((NKV_A, m), F32)]
    return pl.pallas_call(
        functools.partial(_qkv_kernel, tail_rows=tail_rows, tiles_per_group=tiles_per_group),
        grid=(m // tm,),
        in_specs=in_specs,
        out_specs=out_specs,
        out_shape=out_shape,
        compiler_params=_params(1),
        name="norm_qkv",
    )(x, g, wqt_bf16, wkvt_bf16)


def _t5_bucket(dist):
    d = np.maximum(dist, 0)
    df = np.maximum(d, 1).astype(np.float32)
    large = MAX_EXACT + (np.log(df / np.float32(MAX_EXACT)) / np.float32(math.log(MAX_DISTANCE / MAX_EXACT))
                         * np.float32(N_BUCKETS - MAX_EXACT)).astype(np.int32)
    large = np.minimum(large, N_BUCKETS - 1)
    return np.where(d < MAX_EXACT, d, large)


HEADS_PER_SLAB = LANES // A_HEAD_DIM
N_KV_SLABS = NKV_A // LANES
HEADS_PER_KV_SLAB = HEADS_PER_SLAB * A_GROUP
TILE_COLS = HEADS_PER_KV_SLAB * WINDOW


def _bias_kernel(bucket_ref, rb_ref, out_ref):
    bucket = bucket_ref[...]
    has_prev = lax.broadcasted_iota(jnp.int32, bucket.shape, 0) >= WINDOW
    for h in range(A_HEADS):
        acc = jnp.full(bucket.shape, NEG, F32)
        for b in range(N_BUCKETS):
            acc = jnp.where(bucket == b, rb_ref[b, h], acc)
        acc = acc * LOG2E
        t, hh = divmod(h, HEADS_PER_KV_SLAB)
        out_ref[0, t, :, hh * WINDOW:(hh + 1) * WINDOW] = acc
        out_ref[1, t, :, hh * WINDOW:(hh + 1) * WINDOW] = jnp.where(has_prev, acc, NEG)


def _bias_table(rel_bias):
    kj = np.arange(2 * WINDOW, dtype=np.int32)[:, None]
    qi = np.arange(WINDOW, dtype=np.int32)[None, :]
    dist = qi + WINDOW - kj
    valid = (dist >= 0) & (dist < WINDOW)
    bucket = jnp.asarray(np.where(valid, _t5_bucket(dist), -1).astype(np.int32))
    return pl.pallas_call(
        _bias_kernel,
        in_specs=[_resident(), pl.BlockSpec(memory_space=pltpu.SMEM)],
        out_specs=_resident(),
        out_shape=jax.ShapeDtypeStruct((2, N_KV_SLABS, 2 * WINDOW, TILE_COLS), F32),
        name="t5_bias_table",
    )(bucket, rel_bias)


SWA_BLOCKS = 2
SWA_HEAD_ORDER = tuple(
    HEADS_PER_KV_SLAB * t + A_GROUP * half + i
    for t in range(N_KV_SLABS) for i in range(A_GROUP) for half in range(HEADS_PER_SLAB))


def _swa_prompt_kernel(qt_ref, kp_ref, kc_ref, vtp_ref, vtc_ref, bias0_ref, bias_ref, sink_ref, o_ref):
    k = jnp.concatenate([kp_ref[...], kc_ref[...]], axis=0).astype(BF16)
    vt = jnp.concatenate([vtp_ref[...], vtc_ref[...]], axis=1)
    top = lax.broadcasted_iota(jnp.int32, (LANES, WINDOW), 0) < A_HEAD_DIM
    zero = jnp.zeros((LANES, WINDOW), BF16)
    units = [(blk, t) for blk in range(SWA_BLOCKS) for t in range(N_KV_SLABS)]

    scores = []
    for blk, t in units:
        cols = slice(blk * WINDOW, (blk + 1) * WINDOW)
        qs = [qt_ref[(A_GROUP * t + i) * LANES:(A_GROUP * t + i + 1) * LANES, cols] for i in range(A_GROUP)]
        rhs = jnp.concatenate([jnp.where(top, x, zero) for x in qs] + [jnp.where(top, zero, x) for x in qs],
                              axis=1)
        kwin = k[blk * WINDOW:(blk + 2) * WINDOW, t * LANES:(t + 1) * LANES]
        scores.append(jnp.dot(kwin, rhs, preferred_element_type=F32))

    probs, inv = [], []
    for (blk, t), s in zip(units, scores):
        s = s + (bias0_ref[t] if blk == 0 else bias_ref[t])
        sink = sink_ref[t] * LOG2E
        m = jnp.maximum(jnp.max(s, axis=0, keepdims=True), sink)
        e = jnp.exp2(s - m)
        inv.append(1.0 / (jnp.sum(e, axis=0, keepdims=True) + jnp.exp2(sink - m)))
        probs.append(e.astype(BF16))

    outs = []
    for (blk, t), p in zip(units, probs):
        vwin = vt[t * LANES:(t + 1) * LANES, blk * WINDOW:(blk + 2) * WINDOW]
        outs.append(jnp.dot(vwin, p, preferred_element_type=F32))

    for (blk, t), ot, r in zip(units, outs, inv):
        ot = ot * r
        for i in range(A_GROUP):
            first = ot[:, i * WINDOW:(i + 1) * WINDOW]
            second = ot[:, (A_GROUP + i) * WINDOW:(A_GROUP + i + 1) * WINDOW]
            o_ref[blk * WINDOW:(blk + 1) * WINDOW, (A_GROUP * t + i) * LANES:(A_GROUP * t + i + 1) * LANES] = (
                jnp.where(top, first, second).T.astype(BF16))


def _swa_prompt(qt, k, vt, bias, sinks, batch, seq):
    tq = SWA_BLOCKS * WINDOW
    steps = seq // tq
    blocks = seq // WINDOW
    sink_rows = jnp.repeat(sinks, WINDOW).reshape(N_KV_SLABS, 1, TILE_COLS)
    prev_blk = lambda b, j: b * blocks + jnp.maximum(SWA_BLOCKS * j - 1, 0)
    tile = (None, N_KV_SLABS, 2 * WINDOW, TILE_COLS)
    return pl.pallas_call(
        _swa_prompt_kernel,
        grid=(batch, steps),
        in_specs=[
            pl.BlockSpec((NQ_A, tq), lambda b, j: (0, b * steps + j)),
            pl.BlockSpec((WINDOW, NKV_A), lambda b, j: (prev_blk(b, j), 0)),
            pl.BlockSpec((tq, NKV_A), lambda b, j: (b * steps + j, 0)),
            pl.BlockSpec((NKV_A, WINDOW), lambda b, j: (0, prev_blk(b, j))),
            pl.BlockSpec((NKV_A, tq), lambda b, j: (0, b * steps + j)),
            pl.BlockSpec(tile, lambda b, j: (jnp.where(j == 0, 1, 0), 0, 0, 0)),
            _resident(),
            _resident(),
        ],
        out_specs=pl.BlockSpec((tq, NQ_A), lambda b, j: (b * steps + j, 0)),
        out_shape=jax.ShapeDtypeStruct((batch * seq, NQ_A), BF16),
        compiler_params=_params(2),
        name="swa_prompt_attn",
    )(qt, k, k, vt, vt, bias, bias[0], sink_rows)


def _swa_sample_kernel(q_ref, kbt_ref, vbt_ref, knt_ref, vnt_ref, bias_ref, sink_ref,
                       o_ref, kout_ref, vout_ref):
    sink_all = sink_ref[...] * LOG2E
    n_rows, _, wb = kbt_ref.shape
    newest = lax.broadcasted_iota(jnp.int32, (NKV_A, wb), 1) == wb - 1
    kb, vb = [], []
    for r in range(n_rows):
        kw = jnp.where(newest, knt_ref[:, r:r + 1], pltpu.roll(kbt_ref[r], wb - 1, 1))
        vw = jnp.where(newest, vnt_ref[:, r:r + 1], pltpu.roll(vbt_ref[r], wb - 1, 1))
        kout_ref[r] = kw
        vout_ref[r] = vw
        kb.append(kw.astype(BF16))
        vb.append(vw.astype(BF16))
    units = [(r, g) for r in range(n_rows) for g in range(A_KV_HEADS)]
    heads = lambda g: slice(g * A_GROUP, (g + 1) * A_GROUP)
    dims = lambda g: slice(g * A_HEAD_DIM, (g + 1) * A_HEAD_DIM)
    scores = [jnp.dot(q_ref[r, heads(g), :].astype(BF16), kb[r][dims(g), :], preferred_element_type=F32)
              for r, g in units]
    probs, inv = [], []
    for (r, g), s in zip(units, scores):
        s = s + bias_ref[heads(g), :]
        sink = sink_all[heads(g), :]
        m = jnp.maximum(jnp.max(s, axis=-1, keepdims=True), sink)
        e = jnp.exp2(s - m)
        inv.append(1.0 / (jnp.sum(e, axis=-1, keepdims=True) + jnp.exp2(sink - m)))
        probs.append(e.astype(BF16))
    outs = [lax.dot_general(p, vb[r][dims(g), :], NT_DIMS, preferred_element_type=F32)
            for (r, g), p in zip(units, probs)]
    for (r, g), pv, rcp in zip(units, outs, inv):
        o_ref[r, heads(g), :] = pv * rcp


def _swa_sample(q, kt_new, vt_new, kbuf_t, vbuf_t, bias_row, sinks):
    nb = kbuf_t.shape[0]
    return pl.pallas_call(
        _swa_sample_kernel,
        in_specs=[_resident()] * 7,
        out_specs=[_resident()] * 3,
        out_shape=[
            jax.ShapeDtypeStruct((nb, A_HEADS, A_HEAD_DIM), F32),
            jax.ShapeDtypeStruct(kbuf_t.shape, F32),
            jax.ShapeDtypeStruct(vbuf_t.shape, F32),
        ],
        compiler_params=pltpu.CompilerParams(vmem_limit_bytes=VMEM_LIMIT_BYTES),
        name="swa_sample_attn",
    )(q, kbuf_t, vbuf_t, kt_new, vt_new, bias_row, sinks.reshape(A_HEADS, 1))


FF_CHUNK = 1024


def _out_mlp_kernel(*refs, pair_major, final, layer):
    if final:
        o_ref, x_ref, wo_ref, g_ref, wup_hbm, wdn_hbm, gf_ref, out_ref, wup_ref, wdn_ref, sem = refs
    else:
        o_ref, x_ref, wo_ref, g_ref, wup_hbm, wdn_hbm, out_ref, wup_ref, wdn_ref, sem = refs

    @pl.when(pl.program_id(0) == 0)
    def _():
        copies = [pltpu.make_async_copy(wup_hbm.at[layer], wup_ref, sem.at[0]),
                  pltpu.make_async_copy(wdn_hbm.at[layer], wdn_ref, sem.at[1])]
        for cp in copies:
            cp.start()
        for cp in copies:
            cp.wait()

    if pair_major:
        o = jnp.concatenate([o_ref[j] for j in range(o_ref.shape[0])], axis=-1)
    else:
        o = o_ref[...]
    x1 = x_ref[...] + jnp.dot(o, wo_ref[...], preferred_element_type=F32)
    h = _rms(x1, g_ref[...]).astype(BF16)
    acc = x1
    for c in range(D_FF // FF_CHUNK):
        sl = slice(c * FF_CHUNK, (c + 1) * FF_CHUNK)
        u = jnp.dot(h, wup_ref[:, sl], preferred_element_type=F32)
        u = jnp.square(jnp.maximum(u, 0.0)).astype(BF16)
        acc = acc + jnp.dot(u, wdn_ref[sl, :], preferred_element_type=F32)
    if final:
        acc = _rms(acc, gf_ref[...])
    out_ref[...] = acc


def _out_mlp(o, x, wo, g, wup_all, wdn_all, layer, gf, tm, pair_major):
    m = x.shape[0]
    final = gf is not None
    if pair_major:
        tiles_per_group = o.shape[2] // tm
        o_spec = pl.BlockSpec((None, o.shape[1], tm, LANES),
                              lambda i: (i // tiles_per_group, 0, i % tiles_per_group, 0))
    else:
        o_spec = pl.BlockSpec((tm, D_MODEL), lambda i: (i, 0))
    row_spec = pl.BlockSpec((tm, D_MODEL), lambda i: (i, 0))
    in_hbm = pl.BlockSpec(memory_space=pl.ANY)
    in_specs = [o_spec, row_spec, _resident(), _resident(), in_hbm, in_hbm]
    args = [o, x, wo, g, wup_all, wdn_all]
    if final:
        in_specs.append(_resident())
        args.append(gf)
    return pl.pallas_call(
        functools.partial(_out_mlp_kernel, pair_major=pair_major, final=final, layer=layer),
        grid=(m // tm,),
        in_specs=in_specs,
        out_specs=row_spec,
        out_shape=jax.ShapeDtypeStruct((m, D_MODEL), F32),
        scratch_shapes=[
            pltpu.VMEM((D_MODEL, D_FF), BF16),
            pltpu.VMEM((D_FF, D_MODEL), BF16),
            pltpu.SemaphoreType.DMA((2,)),
        ],
        compiler_params=_params(1),
        name="out_proj_mlp",
    )(*args)


N_PAIRS = B_HEADS // 2
ROPE_LANE0 = NOPE_DIM
C_IN_EXT = Q_LORA + KV_LORA + LANES


def _rope_slab(t, c, s1, s2):
    return t * c + pltpu.roll(t, HALF_ROPE, 1) * s1 + pltpu.roll(t, LANES - HALF_ROPE, 1) * s2


def _mla_proj_kernel(x_ref, g_ref, win_ref, qn_ref, kvn_ref, wqb_ref, wk_ref, wvt_ref,
                     c_ref, s1_ref, s2_ref,
                     qcat_ref, kcat_ref, vt_ref, lat_ref, krt_ref, *, q_scale):
    h = _rms(x_ref[...], g_ref[...]).astype(BF16)
    c = jnp.dot(h, win_ref[...], preferred_element_type=F32)
    cq = _rms(c[:, :Q_LORA], qn_ref[...]).astype(BF16)
    lat = _rms(c[:, Q_LORA:Q_LORA + KV_LORA], kvn_ref[...])
    lat_ref[...] = lat
    cos, s1, s2 = c_ref[...], s1_ref[...], s2_ref[...]
    krp = _rope_slab(c[:, Q_LORA + KV_LORA:], cos, s1, s2)
    krt_ref[...] = krp.T[ROPE_LANE0:ROPE_LANE0 + ROPE_DIM, :]
    q = jnp.dot(cq, wqb_ref[...], preferred_element_type=F32)
    latb = lat.astype(BF16)
    kn = jnp.dot(latb, wk_ref[...], preferred_element_type=F32)
    vt = lax.dot_general(wvt_ref[...], latb, NT_DIMS, preferred_element_type=F32)
    for hh in range(B_HEADS):
        sl = slice(hh * LANES, (hh + 1) * LANES)
        qh = _rope_slab(q[:, sl], cos, s1, s2)
        if q_scale != 1.0:
            qh = qh * q_scale
        qcat_ref[hh] = qh.astype(BF16)
        kcat_ref[hh] = (kn[:, sl] + krp).astype(BF16)
    for j in range(N_PAIRS):
        vt_ref[j] = vt[j * LANES:(j + 1) * LANES, :].astype(BF16)


def _mla_proj(x, g, win, qn, kvn, wqb, wk, wvt, tabs, groups, rows, tm, tab_blocks, q_scale):
    m = x.shape[0]
    tiles = rows // tm
    head_map = lambda i: (i // tiles, 0, i % tiles, 0)
    tab_spec = pl.BlockSpec((tm, LANES), lambda i: (i % tab_blocks, 0))
    return pl.pallas_call(
        functools.partial(_mla_proj_kernel, q_scale=q_scale),
        grid=(m // tm,),
        in_specs=[pl.BlockSpec((tm, D_MODEL), lambda i: (i, 0))] + [_resident()] * 7 + [tab_spec] * 3,
        out_specs=[
            pl.BlockSpec((None, B_HEADS, tm, LANES), head_map),
            pl.BlockSpec((None, B_HEADS, tm, LANES), head_map),
            pl.BlockSpec((None, N_PAIRS, LANES, tm), lambda i: (i // tiles, 0, 0, i % tiles)),
            pl.BlockSpec((tm, KV_LORA), lambda i: (i, 0)),
            pl.BlockSpec((None, ROPE_DIM, tm), lambda i: (i // tiles, 0, i % tiles)),
        ],
        out_shape=[
            jax.ShapeDtypeStruct((groups, B_HEADS, rows, LANES), BF16),
            jax.ShapeDtypeStruct((groups, B_HEADS, rows, LANES), BF16),
            jax.ShapeDtypeStruct((groups, N_PAIRS, LANES, rows), BF16),
            jax.ShapeDtypeStruct((m, KV_LORA), F32),
            jax.ShapeDtypeStruct((groups, ROPE_DIM, rows), F32),
        ],
        compiler_params=_params(1),
        name="mla_proj",
    )(x, g, win, qn, kvn, wqb, wk, wvt, *tabs)


def _rope_tables(pos):
    pos = np.asarray(pos, np.float32)
    inv = np.float32(ROPE_THETA) ** (-np.arange(0, ROPE_DIM, 2, dtype=np.float32) / np.float32(ROPE_DIM))
    ang = pos[:, None] * inv[None, :]
    cos, sin = np.cos(ang), np.sin(ang)
    s = pos.shape[0]
    ones = np.ones((s, ROPE_LANE0), np.float32)
    z16 = np.zeros((s, HALF_ROPE), np.float32)
    z64 = np.zeros((s, ROPE_LANE0), np.float32)
    z32 = np.zeros((s, LANES - ROPE_LANE0 - ROPE_DIM), np.float32)
    c = np.concatenate([ones, cos, cos, z32], axis=1)
    s1 = np.concatenate([z64, z16, sin, z32], axis=1)
    s2 = np.concatenate([z64, -sin, z16, z32], axis=1)
    return jnp.asarray(c), jnp.asarray(s1), jnp.asarray(s2)


HEADS_PER_BODY = 8
REDUCE_ROWS = 64


def _col_reduce(x, op):
    n, w = x.shape
    part = op(x.reshape(n // REDUCE_ROWS, REDUCE_ROWS, w), axis=0)
    return op(part, axis=0, keepdims=True)


def _mla_attn_kernel(q_ref, k_ref, vt_ref, o_ref, *, tq, n_tiles):
    qi = pl.program_id(1)
    key = lax.broadcasted_iota(jnp.int32, (tq, tq), 0)
    qry = lax.broadcasted_iota(jnp.int32, (tq, tq), 1)
    causal = key <= qry

    def scores(h, n_keys):
        k = k_ref[h, pl.ds(0, n_keys), :]
        return lax.dot_general(k, q_ref[h], NT_DIMS, preferred_element_type=F32)

    def attend(st, j, n_keys):
        diag = jnp.where(causal, st[n_keys - tq:, :], NEG)
        if n_keys > tq:
            full = st[:n_keys - tq, :]
            m = jnp.maximum(_col_reduce(full, jnp.max), _col_reduce(diag, jnp.max))
            p = jnp.concatenate([jnp.exp2(full - m), jnp.exp2(diag - m)], axis=0)
        else:
            m = _col_reduce(diag, jnp.max)
            p = jnp.exp2(diag - m)
        l = _col_reduce(p, jnp.sum)
        ot = jnp.dot(vt_ref[j, :, pl.ds(0, n_keys)], p.astype(BF16), preferred_element_type=F32)
        return ot / l

    for c in range(n_tiles):
        @pl.when(qi == c)
        def _(c=c):
            n_keys = (c + 1) * tq

            def body(jj, carry):
                heads = [HEADS_PER_BODY * jj + u for u in range(HEADS_PER_BODY)]
                sts = [scores(h, n_keys) for h in heads]
                outs = [attend(st, (HEADS_PER_BODY // 2) * jj + u // 2, n_keys) for u, st in enumerate(sts)]
                for u in range(0, HEADS_PER_BODY, 2):
                    ot = jnp.concatenate([outs[u][:V_DIM], outs[u + 1][V_DIM:]], axis=0)
                    o_ref[(HEADS_PER_BODY // 2) * jj + u // 2] = ot.T.astype(BF16)
                return carry

            lax.fori_loop(0, B_HEADS // HEADS_PER_BODY, body, 0)


def _mla_attn(qcat, kcat, vt, tq):
    batch, _, seq, _ = qcat.shape
    n_tiles = seq // tq
    return pl.pallas_call(
        functools.partial(_mla_attn_kernel, tq=tq, n_tiles=n_tiles),
        grid=(batch, n_tiles),
        in_specs=[
            pl.BlockSpec((None, B_HEADS, tq, LANES), lambda b, i: (b, 0, i, 0)),
            pl.BlockSpec((None, B_HEADS, seq, LANES), lambda b, i: (b, 0, 0, 0)),
            pl.BlockSpec((None, N_PAIRS, LANES, seq), lambda b, i: (b, 0, 0, 0)),
        ],
        out_specs=pl.BlockSpec((None, N_PAIRS, tq, LANES), lambda b, i: (b, 0, i, 0)),
        out_shape=jax.ShapeDtypeStruct((batch, N_PAIRS, seq, LANES), BF16),
        compiler_params=_params(2),
        name="mla_prompt_attn",
    )(qcat, kcat, vt)


def _absorb_kernel(qcat_ref, wuk_ref, qlat_ref):
    for h in range(B_HEADS):
        qlat_ref[h] = jnp.dot(qcat_ref[h], wuk_ref[h], preferred_element_type=F32).astype(BF16)


def _absorb(qcat, wuk_ext):
    nb = qcat.shape[1]
    return pl.pallas_call(
        _absorb_kernel,
        in_specs=[_resident(), _resident()],
        out_specs=_resident(),
        out_shape=jax.ShapeDtypeStruct((B_HEADS, nb, KV_LORA), BF16),
        name="mla_absorb_q",
    )(qcat, wuk_ext)


def _decode_kernel(pt_ref, qlat_ref, qr_ref, latnew_ref, krnew_ref, lat_hbm, krt_hbm, o_ref,
                   latbuf, krbuf, sem, m_sc, l_sc, acc_sc,
                   *, pages_per_step, steps_per_batch, n_chains, scale):
    b = pl.program_id(0)
    c = pl.program_id(1)
    step = b * steps_per_batch + c
    last_step = step + 1 == pl.num_programs(0) * steps_per_batch
    slot = step % 2
    page_size = latbuf.shape[2]

    def start_chunk(chunk, sl):
        for p in range(pages_per_step):
            page = pt_ref[chunk * pages_per_step + p]
            pltpu.make_async_copy(lat_hbm.at[page], latbuf.at[sl, p], sem.at[0, sl]).start()
            pltpu.make_async_copy(krt_hbm.at[page], krbuf.at[sl, p], sem.at[1, sl]).start()

    def wait_chunk(sl):
        pltpu.make_async_copy(lat_hbm.at[pl.ds(0, pages_per_step)], latbuf.at[sl], sem.at[0, sl]).wait()
        pltpu.make_async_copy(krt_hbm.at[pl.ds(0, pages_per_step)], krbuf.at[sl], sem.at[1, sl]).wait()

    @pl.when(step == 0)
    def _():
        start_chunk(step, slot)

    start_chunk(jnp.where(last_step, step, step + 1), 1 - slot)

    qlat = qlat_ref[...]
    qr = qr_ref[...]

    @pl.when(c == 0)
    def _():
        ln = latnew_ref[...].astype(BF16).astype(F32)
        kn = krnew_ref[...].astype(BF16).astype(F32)
        s_new = (jnp.sum(qlat.astype(F32) * ln, axis=-1, keepdims=True)
                 + jnp.sum(qr.astype(F32) * kn, axis=-1, keepdims=True)) * scale
        m_sc[...] = jnp.full(m_sc.shape, NEG, F32)
        l_sc[...] = jnp.zeros(l_sc.shape, F32)
        acc_sc[...] = jnp.zeros(acc_sc.shape, F32)
        m_sc[0] = s_new
        l_sc[0] = jnp.ones_like(s_new)
        acc_sc[0] = jnp.broadcast_to(ln, acc_sc.shape[1:])

    wait_chunk(slot)

    ppc = pages_per_step // n_chains
    chains = range(n_chains)
    s_rope = [jnp.concatenate(
        [jnp.dot(qr, krbuf[slot, i * ppc + p].astype(BF16), preferred_element_type=F32)
         for p in range(ppc)], axis=1) for i in chains]
    lat = [latbuf[slot, pl.ds(i * ppc, ppc)].reshape(ppc * page_size, KV_LORA).astype(BF16) for i in chains]
    s = [(lax.dot_general(qlat, lat[i], NT_DIMS, preferred_element_type=F32) + s_rope[i]) * scale
         for i in chains]
    m_old = [m_sc[i] for i in chains]
    m_new = [jnp.maximum(m_old[i], jnp.max(s[i], axis=-1, keepdims=True)) for i in chains]
    alpha = [jnp.exp(m_old[i] - m_new[i]) for i in chains]
    p = [jnp.exp(s[i] - m_new[i]) for i in chains]
    pv = [jnp.dot(p[i].astype(BF16), lat[i], preferred_element_type=F32) for i in chains]
    for i in chains:
        l_sc[i] = alpha[i] * l_sc[i] + jnp.sum(p[i], axis=-1, keepdims=True)
        acc_sc[i] = alpha[i] * acc_sc[i] + pv[i]
        m_sc[i] = m_new[i]

    @pl.when(c == steps_per_batch - 1)
    def _():
        m = m_sc[0]
        for i in range(1, n_chains):
            m = jnp.maximum(m, m_sc[i])
        l = jnp.zeros_like(m)
        acc = jnp.zeros(acc_sc.shape[1:], F32)
        for i in range(n_chains):
            w = jnp.exp(m_sc[i] - m)
            l = l + w * l_sc[i]
            acc = acc + w * acc_sc[i]
        o_ref[...] = acc / l

    @pl.when(last_step)
    def _():
        wait_chunk(1 - slot)


def _mla_decode(page_table, qlat, qr, lat_new, kr_new, lat_pool, krt_pool, pages_per_step, n_chains):
    nb, n_pages = page_table.shape
    page_size = lat_pool.shape[1]
    steps_per_batch = n_pages // pages_per_step
    scale = (NOPE_DIM + ROPE_DIM) ** -0.5
    per_b = lambda b, c, pt: (b, 0, 0)
    grid_spec = pltpu.PrefetchScalarGridSpec(
        num_scalar_prefetch=1,
        grid=(nb, steps_per_batch),
        in_specs=[
            pl.BlockSpec((None, B_HEADS, KV_LORA), per_b),
            pl.BlockSpec((None, B_HEADS, ROPE_DIM), per_b),
            pl.BlockSpec((None, 1, KV_LORA), per_b),
            pl.BlockSpec((None, 1, ROPE_DIM), per_b),
            pl.BlockSpec(memory_space=pl.ANY),
            pl.BlockSpec(memory_space=pl.ANY),
        ],
        out_specs=pl.BlockSpec((None, B_HEADS, KV_LORA), per_b),
        scratch_shapes=[
            pltpu.VMEM((2, pages_per_step, page_size, KV_LORA), F32),
            pltpu.VMEM((2, pages_per_step, ROPE_DIM, page_size), F32),
            pltpu.SemaphoreType.DMA((2, 2)),
            pltpu.VMEM((n_chains, B_HEADS, 1), F32),
            pltpu.VMEM((n_chains, B_HEADS, 1), F32),
            pltpu.VMEM((n_chains, B_HEADS, KV_LORA), F32),
        ],
    )
    return pl.pallas_call(
        functools.partial(_decode_kernel, pages_per_step=pages_per_step,
                          steps_per_batch=steps_per_batch, n_chains=n_chains, scale=scale),
        grid_spec=grid_spec,
        out_shape=jax.ShapeDtypeStruct((nb, B_HEADS, KV_LORA), F32),
        compiler_params=_params(2),
        name="mla_decode_attn",
    )(page_table.reshape(-1), qlat, qr, lat_new, kr_new, lat_pool, krt_pool)


def _uv_kernel(olat_ref, wuv_ref, o_ref):
    for h in range(B_HEADS):
        o_ref[:, h * V_DIM:(h + 1) * V_DIM] = jnp.dot(
            olat_ref[h].astype(BF16), wuv_ref[h], preferred_element_type=F32).astype(BF16)


def _uv(olat_hm, wuv):
    nb = olat_hm.shape[1]
    return pl.pallas_call(
        _uv_kernel,
        in_specs=[_resident(), _resident()],
        out_specs=_resident(),
        out_shape=jax.ShapeDtypeStruct((nb, B_HEADS * V_DIM), BF16),
        name="mla_value_up",
    )(olat_hm, wuv)


PROMPT_TM = 256
MLP_TM = 512
MLA_TQ = 256
DECODE_PAGES = 32
DECODE_CHAINS = 4


def kernel(x_prompt, x_sample, cache_a_k, cache_a_v, cache_b_latent, cache_b_krope, page_table,
           rel_bias, norm_mix, norm_mlp, norm_final, a_w_qkv, a_w_o, a_sinks,
           b_w_in, b_q_norm, b_kv_norm, b_w_q_b, b_w_kv_b, b_w_o, mlp_w_up, mlp_w_down):
    batch, seq, _ = x_prompt.shape
    nb_s = x_sample.shape[0]
    past = page_table.shape[1] * cache_b_latent.shape[2]
    wb = cache_a_k.shape[2]

    xp = x_prompt.reshape(batch * seq, D_MODEL)
    xs = x_sample.reshape(nb_s, D_MODEL)
    row = lambda t: t.reshape(1, -1)

    order = jnp.array(SWA_HEAD_ORDER)
    inverse = jnp.argsort(order)
    w_qt = (a_w_qkv[0][:, :NQ_A].reshape(D_MODEL, A_HEADS, A_HEAD_DIM)[:, order]
            .reshape(D_MODEL, NQ_A).T.astype(BF16))
    w_kvt = a_w_qkv[0][:, NQ_A:].T.astype(BF16)
    w_o_a = a_w_o[0].reshape(A_HEADS, A_HEAD_DIM, D_MODEL)[order].reshape(NQ_A, D_MODEL).astype(BF16)
    w_up, w_dn = mlp_w_up.astype(BF16), mlp_w_down.astype(BF16)
    bias = _bias_table(rel_bias)
    sinks = a_sinks[0]
    wbp = min(WINDOW, seq)

    qtp, kp, vtp, kt_tail, vt_tail = _norm_qkv(xp, row(norm_mix[0]), w_qt, w_kvt, PROMPT_TM, batch, wbp)
    op = _swa_prompt(qtp, kp, vtp, bias, sinks, batch, seq)
    xp = _out_mlp(op, xp, w_o_a, row(norm_mlp[0]), w_up, w_dn, 0, None, MLP_TM, False)

    qts, kts, vts = _norm_qkv(xs, row(norm_mix[0]), w_qt, w_kvt, nb_s)
    qs3 = qts.T.astype(F32).reshape(nb_s, A_HEADS, A_HEAD_DIM)[:, inverse]
    bias_row = jnp.transpose(
        bias[0].reshape(N_KV_SLABS, 2 * WINDOW, HEADS_PER_KV_SLAB, WINDOW)[:, WINDOW:, :, WINDOW - 1],
        (0, 2, 1)).reshape(A_HEADS, WINDOW)
    to_t = lambda c: jnp.transpose(c[0], (0, 2, 3, 1)).reshape(nb_s, NKV_A, wb)
    from_t = lambda t, n, w: jnp.transpose(t.reshape(n, A_KV_HEADS, A_HEAD_DIM, w), (0, 3, 1, 2))[None]
    os3, a_k_st, a_v_st = _swa_sample(qs3, kts, vts, to_t(cache_a_k), to_t(cache_a_v), bias_row, sinks)
    os_ = os3[:, order].reshape(nb_s, NQ_A).astype(BF16)
    xs = _out_mlp(os_, xs, w_o_a, row(norm_mlp[0]), w_up, w_dn, 0, None, nb_s, False)

    w_in = b_w_in[0]
    zeros = lambda n: jnp.zeros((D_MODEL, n), F32)
    w_in_ext = jnp.concatenate([w_in[:, :Q_LORA + KV_LORA], zeros(ROPE_LANE0),
                                w_in[:, Q_LORA + KV_LORA:], zeros(LANES - ROPE_LANE0 - ROPE_DIM)],
                               axis=1).astype(BF16)
    qk_dim = NOPE_DIM + ROPE_DIM
    w_qb = jnp.pad(b_w_q_b[0].reshape(Q_LORA, B_HEADS, qk_dim),
                   ((0, 0), (0, 0), (0, LANES - qk_dim))).reshape(Q_LORA, B_HEADS * LANES).astype(BF16)
    w_kvb = b_w_kv_b[0]
    w_uk, w_uv = w_kvb[..., :NOPE_DIM], w_kvb[..., NOPE_DIM:]
    w_k = jnp.pad(w_uk, ((0, 0), (0, 0), (0, LANES - NOPE_DIM))).reshape(KV_LORA, B_HEADS * LANES).astype(BF16)
    w_vt = w_uv.reshape(KV_LORA, B_HEADS * V_DIM).T.astype(BF16)
    w_o_b = b_w_o[0].astype(BF16)
    proj_w = (row(norm_mix[1]), w_in_ext, row(b_q_norm[0]), row(b_kv_norm[0]), w_qb, w_k, w_vt)

    q_scale_p = (NOPE_DIM + ROPE_DIM) ** -0.5 * LOG2E
    tabs_p = _rope_tables(np.arange(seq))
    qcat, kcat, vt, lat_p, krt_p = _mla_proj(xp, *proj_w, tabs_p, batch, seq, PROMPT_TM, seq // PROMPT_TM,
                                            q_scale_p)
    o_pm = _mla_attn(qcat, kcat, vt, MLA_TQ)
    yp = _out_mlp(o_pm, xp, w_o_b, row(norm_mlp[1]), w_up, w_dn, 1, row(norm_final), MLP_TM, True)

    tabs_s = _rope_tables(np.full((nb_s,), past))
    qcat_s, _, _, lat_s, krt_s = _mla_proj(xs, *proj_w, tabs_s, 1, nb_s, nb_s, 1, 1.0)
    w_uk_ext = jnp.pad(jnp.transpose(w_uk, (1, 2, 0)), ((0, 0), (0, LANES - NOPE_DIM), (0, 0))).astype(BF16)
    qlat = jnp.transpose(_absorb(qcat_s[0], w_uk_ext), (1, 0, 2))
    qr = jnp.transpose(qcat_s[0, :, :, ROPE_LANE0:ROPE_LANE0 + ROPE_DIM], (1, 0, 2))
    kr_s = krt_s[0].T
    krt_pool = jnp.swapaxes(cache_b_krope[0], 1, 2)
    olat = _mla_decode(page_table, qlat, qr, lat_s.reshape(nb_s, 1, KV_LORA), kr_s.reshape(nb_s, 1, ROPE_DIM),
                       cache_b_latent[0], krt_pool, DECODE_PAGES, DECODE_CHAINS)
    o_s = _uv(jnp.transpose(olat, (1, 0, 2)), jnp.transpose(w_uv, (1, 0, 2)).astype(BF16))
    ys = _out_mlp(o_s, xs, w_o_b, row(norm_mlp[1]), w_up, w_dn, 1, row(norm_final), nb_s, False)

    k4 = from_t(kt_tail, batch, wbp)
    v4 = from_t(vt_tail, batch, wbp)
    kr_p = jnp.swapaxes(krt_p, 1, 2)
    return (
        yp.reshape(batch, seq, D_MODEL),
        ys.reshape(nb_s, 1, D_MODEL),
        k4, v4,
        lat_p.reshape(1, batch, seq, KV_LORA),
        kr_p.reshape(1, batch, seq, ROPE_DIM),
        from_t(a_k_st, nb_s, wb),
        from_t(a_v_st, nb_s, wb),
        lat_s.reshape(1, nb_s, 1, KV_LORA),
        kr_s.reshape(1, nb_s, 1, ROPE_DIM),
    )
```

```python
import functools
import math

import jax
import jax.numpy as jnp
import numpy as np
from jax import lax
from jax.experimental import pallas as pl
from jax.experimental.pallas import tpu as pltpu

F32 = jnp.float32
BF16 = jnp.bfloat16

D_MODEL = 1024
A_HEADS = 16
A_KV_HEADS = 4
A_HEAD_DIM = 64
A_GROUP = A_HEADS // A_KV_HEADS
WINDOW = 128
N_BUCKETS = 32
MAX_EXACT = N_BUCKETS // 2
MAX_DISTANCE = 128
B_HEADS = 16
Q_LORA = 768
KV_LORA = 256
NOPE_DIM = 64
ROPE_DIM = 32
HALF_ROPE = ROPE_DIM // 2
V_DIM = 64
ROPE_THETA = 10000.0
D_FF = 4 * D_MODEL
EPS = 1e-6

LANES = 128
VMEM_LIMIT_BYTES = 56 * 1024 * 1024

NEG = -1e30
LOG2E = math.log2(math.e)

NT_DIMS = (((1,), (1,)), ((), ()))


def _rms(x, g):
    return x * lax.rsqrt(jnp.mean(x * x, axis=-1, keepdims=True) + EPS) * g


def _params(n_axes):
    return pltpu.CompilerParams(
        dimension_semantics=("arbitrary",) * n_axes,
        vmem_limit_bytes=VMEM_LIMIT_BYTES,
    )


def _resident():
    return pl.BlockSpec(memory_space=pltpu.VMEM)


NQ_A = A_HEADS * A_HEAD_DIM
NKV_A = A_KV_HEADS * A_HEAD_DIM


def _qkv_kernel(x_ref, g_ref, wt_ref, *out_refs, tail_rows, tiles_per_group):
    h = _rms(x_ref[...], g_ref[...]).astype(BF16)
    qkvt = lax.dot_general(wt_ref[...], h, NT_DIMS, preferred_element_type=F32)
    out_refs[0][...] = (qkvt[:NQ_A] * (A_HEAD_DIM ** -0.5 * LOG2E)).astype(BF16)
    kvt = qkvt[NQ_A:]
    if tail_rows:
        _, k_ref, vt_ref, ktail_ref, vtail_ref = out_refs
        k_ref[...] = kvt[:NKV_A].T
        vt_ref[...] = kvt[NKV_A:].astype(BF16)

        @pl.when(pl.program_id(0) % tiles_per_group == tiles_per_group - 1)
        def _():
            ktail_ref[...] = kvt[:NKV_A, kvt.shape[1] - tail_rows:]
            vtail_ref[...] = kvt[NKV_A:, kvt.shape[1] - tail_rows:]
    else:
        _, kt_ref, vt_ref = out_refs
        kt_ref[...] = kvt[:NKV_A]
        vt_ref[...] = kvt[NKV_A:]


def _norm_qkv(x, g, wt_bf16, tm, groups=1, tail_rows=0):
    m = x.shape[0]
    tiles_per_group = m // groups // tm
    col_blk = lambda i: (0, i)
    in_specs = [pl.BlockSpec((tm, D_MODEL), lambda i: (i, 0)), _resident(), _resident()]
    qt_spec = pl.BlockSpec((NQ_A, tm), col_blk)
    qt_shape = jax.ShapeDtypeStruct((NQ_A, m), BF16)
    if tail_rows:
        tail_spec = pl.BlockSpec((None, NKV_A, tail_rows), lambda i: (i // tiles_per_group, 0, 0))
        tail_shape = jax.ShapeDtypeStruct((groups, NKV_A, tail_rows), F32)
        out_specs = [qt_spec, pl.BlockSpec((tm, NKV_A), lambda i: (i, 0)),
                     pl.BlockSpec((NKV_A, tm), col_blk), tail_spec, tail_spec]
        out_shape = [qt_shape, jax.ShapeDtypeStruct((m, NKV_A), F32),
                     jax.ShapeDtypeStruct((NKV_A, m), BF16), tail_shape, tail_shape]
    else:
        out_specs = [qt_spec, pl.BlockSpec((NKV_A, tm), col_blk), pl.BlockSpec((NKV_A, tm), col_blk)]
        out_shape = [qt_shape, jax.ShapeDtypeStruct((NKV_A, m), F32), jax.ShapeDtypeStruct((NKV_A, m), F32)]
    return pl.pallas_call(
        functools.partial(_qkv_kernel, tail_rows=tail_rows, tiles_per_group=tiles_per_group),
        grid=(m // tm,),
        in_specs=in_specs,
        out_specs=out_specs,
        out_shape=out_shape,
        compiler_params=_params(1),
        name="norm_qkv",
    )(x, g, wt_bf16)


def _t5_bucket(dist):
    d = np.maximum(dist, 0)
    df = np.maximum(d, 1).astype(np.float32)
    large = MAX_EXACT + (np.log(df / np.float32(MAX_EXACT)) / np.float32(math.log(MAX_DISTANCE / MAX_EXACT))
                         * np.float32(N_BUCKETS - MAX_EXACT)).astype(np.int32)
    large = np.minimum(large, N_BUCKETS - 1)
    return np.where(d < MAX_EXACT, d, large)


HEADS_PER_SLAB = LANES // A_HEAD_DIM
N_KV_SLABS = NKV_A // LANES
HEADS_PER_KV_SLAB = HEADS_PER_SLAB * A_GROUP
TILE_COLS = HEADS_PER_KV_SLAB * WINDOW


def _bias_kernel(bucket_ref, rb_ref, out_ref):
    bucket = bucket_ref[...]
    has_prev = lax.broadcasted_iota(jnp.int32, bucket.shape, 0) >= WINDOW
    for h in range(A_HEADS):
        acc = jnp.full(bucket.shape, NEG, F32)
        for b in range(N_BUCKETS):
            acc = jnp.where(bucket == b, rb_ref[b, h], acc)
        acc = acc * LOG2E
        t, hh = divmod(h, HEADS_PER_KV_SLAB)
        out_ref[0, t, :, hh * WINDOW:(hh + 1) * WINDOW] = acc
        out_ref[1, t, :, hh * WINDOW:(hh + 1) * WINDOW] = jnp.where(has_prev, acc, NEG)


def _bias_table(rel_bias):
    kj = np.arange(2 * WINDOW, dtype=np.int32)[:, None]
    qi = np.arange(WINDOW, dtype=np.int32)[None, :]
    dist = qi + WINDOW - kj
    valid = (dist >= 0) & (dist < WINDOW)
    bucket = jnp.asarray(np.where(valid, _t5_bucket(dist), -1).astype(np.int32))
    return pl.pallas_call(
        _bias_kernel,
        in_specs=[_resident(), pl.BlockSpec(memory_space=pltpu.SMEM)],
        out_specs=_resident(),
        out_shape=jax.ShapeDtypeStruct((2, N_KV_SLABS, 2 * WINDOW, TILE_COLS), F32),
        name="t5_bias_table",
    )(bucket, rel_bias)


SWA_BLOCKS = 8
SWA_HEAD_ORDER = tuple(
    HEADS_PER_KV_SLAB * t + A_GROUP * half + i
    for t in range(N_KV_SLABS) for i in range(A_GROUP) for half in range(HEADS_PER_SLAB))


def _swa_prompt_kernel(qt_ref, kp_ref, kc_ref, vtp_ref, vtc_ref, bias0_ref, bias_ref, sink_ref, o_ref):
    k = jnp.concatenate([kp_ref[...], kc_ref[...]], axis=0).astype(BF16)
    vt = jnp.concatenate([vtp_ref[...], vtc_ref[...]], axis=1)
    top = lax.broadcasted_iota(jnp.int32, (LANES, WINDOW), 0) < A_HEAD_DIM
    zero = jnp.zeros((LANES, WINDOW), BF16)
    units = [(blk, t) for blk in range(SWA_BLOCKS) for t in range(N_KV_SLABS)]

    scores = []
    for blk, t in units:
        cols = slice(blk * WINDOW, (blk + 1) * WINDOW)
        qs = [qt_ref[(A_GROUP * t + i) * LANES:(A_GROUP * t + i + 1) * LANES, cols] for i in range(A_GROUP)]
        rhs = jnp.concatenate([jnp.where(top, x, zero) for x in qs] + [jnp.where(top, zero, x) for x in qs],
                              axis=1)
        kwin = k[blk * WINDOW:(blk + 2) * WINDOW, t * LANES:(t + 1) * LANES]
        scores.append(jnp.dot(kwin, rhs, preferred_element_type=F32))

    probs, inv = [], []
    for (blk, t), s in zip(units, scores):
        s = s + (bias0_ref[t] if blk == 0 else bias_ref[t])
        sink = sink_ref[t] * LOG2E
        m = jnp.maximum(jnp.max(s, axis=0, keepdims=True), sink)
        e = jnp.exp2(s - m)
        inv.append(1.0 / (jnp.sum(e, axis=0, keepdims=True) + jnp.exp2(sink - m)))
        probs.append(e.astype(BF16))

    outs = []
    for (blk, t), p in zip(units, probs):
        vwin = vt[t * LANES:(t + 1) * LANES, blk * WINDOW:(blk + 2) * WINDOW]
        outs.append(jnp.dot(vwin, p, preferred_element_type=F32))

    for (blk, t), ot, r in zip(units, outs, inv):
        ot = ot * r
        for i in range(A_GROUP):
            first = ot[:, i * WINDOW:(i + 1) * WINDOW]
            second = ot[:, (A_GROUP + i) * WINDOW:(A_GROUP + i + 1) * WINDOW]
            o_ref[blk * WINDOW:(blk + 1) * WINDOW, (A_GROUP * t + i) * LANES:(A_GROUP * t + i + 1) * LANES] = (
                jnp.where(top, first, second).T.astype(BF16))


def _swa_prompt(qt, k, vt, bias, sinks, batch, seq):
    tq = SWA_BLOCKS * WINDOW
    steps = seq // tq
    blocks = seq // WINDOW
    sink_rows = jnp.repeat(sinks, WINDOW).reshape(N_KV_SLABS, 1, TILE_COLS)
    prev_blk = lambda b, j: b * blocks + jnp.maximum(SWA_BLOCKS * j - 1, 0)
    tile = (None, N_KV_SLABS, 2 * WINDOW, TILE_COLS)
    return pl.pallas_call(
        _swa_prompt_kernel,
        grid=(batch, steps),
        in_specs=[
            pl.BlockSpec((NQ_A, tq), lambda b, j: (0, b * steps + j)),
            pl.BlockSpec((WINDOW, NKV_A), lambda b, j: (prev_blk(b, j), 0)),
            pl.BlockSpec((tq, NKV_A), lambda b, j: (b * steps + j, 0)),
            pl.BlockSpec((NKV_A, WINDOW), lambda b, j: (0, prev_blk(b, j))),
            pl.BlockSpec((NKV_A, tq), lambda b, j: (0, b * steps + j)),
            pl.BlockSpec(tile, lambda b, j: (jnp.where(j == 0, 1, 0), 0, 0, 0)),
            _resident(),
            _resident(),
        ],
        out_specs=pl.BlockSpec((tq, NQ_A), lambda b, j: (b * steps + j, 0)),
        out_shape=jax.ShapeDtypeStruct((batch * seq, NQ_A), BF16),
        compiler_params=_params(2),
        name="swa_prompt_attn",
    )(qt, k, k, vt, vt, bias, bias[0], sink_rows)


def _swa_sample_kernel(q_ref, kbt_ref, vbt_ref, knt_ref, vnt_ref, bias_ref, sink_ref,
                       o_ref, kout_ref, vout_ref):
    sink_all = sink_ref[...] * LOG2E
    n_rows, _, wb = kbt_ref.shape
    newest = lax.broadcasted_iota(jnp.int32, (NKV_A, wb), 1) == wb - 1
    kb, vb = [], []
    for r in range(n_rows):
        kw = jnp.where(newest, knt_ref[:, r:r + 1], pltpu.roll(kbt_ref[r], wb - 1, 1))
        vw = jnp.where(newest, vnt_ref[:, r:r + 1], pltpu.roll(vbt_ref[r], wb - 1, 1))
        kout_ref[r] = kw
        vout_ref[r] = vw
        kb.append(kw.astype(BF16))
        vb.append(vw.astype(BF16))
    units = [(r, g) for r in range(n_rows) for g in range(A_KV_HEADS)]
    heads = lambda g: slice(g * A_GROUP, (g + 1) * A_GROUP)
    dims = lambda g: slice(g * A_HEAD_DIM, (g + 1) * A_HEAD_DIM)
    scores = [jnp.dot(q_ref[r, heads(g), :].astype(BF16), kb[r][dims(g), :], preferred_element_type=F32)
              for r, g in units]
    probs, inv = [], []
    for (r, g), s in zip(units, scores):
        s = s + bias_ref[heads(g), :]
        sink = sink_all[heads(g), :]
        m = jnp.maximum(jnp.max(s, axis=-1, keepdims=True), sink)
        e = jnp.exp2(s - m)
        inv.append(1.0 / (jnp.sum(e, axis=-1, keepdims=True) + jnp.exp2(sink - m)))
        probs.append(e.astype(BF16))
    outs = [lax.dot_general(p, vb[r][dims(g), :], NT_DIMS, preferred_element_type=F32)
            for (r, g), p in zip(units, probs)]
    for (r, g), pv, rcp in zip(units, outs, inv):
        o_ref[r, heads(g), :] = pv * rcp


def _swa_sample(q, kt_new, vt_new, kbuf_t, vbuf_t, bias_row, sinks):
    nb = kbuf_t.shape[0]
    return pl.pallas_call(
        _swa_sample_kernel,
        in_specs=[_resident()] * 7,
        out_specs=[_resident()] * 3,
        out_shape=[
            jax.ShapeDtypeStruct((nb, A_HEADS, A_HEAD_DIM), F32),
            jax.ShapeDtypeStruct(kbuf_t.shape, F32),
            jax.ShapeDtypeStruct(vbuf_t.shape, F32),
        ],
        compiler_params=pltpu.CompilerParams(vmem_limit_bytes=VMEM_LIMIT_BYTES),
        name="swa_sample_attn",
    )(q, kbuf_t, vbuf_t, kt_new, vt_new, bias_row, sinks.reshape(A_HEADS, 1))


FF_CHUNK = 1024


def _out_mlp_kernel(*refs, pair_major, final, layer):
    if final:
        o_ref, x_ref, wo_ref, g_ref, wup_hbm, wdn_hbm, gf_ref, out_ref, wup_ref, wdn_ref, sem = refs
    else:
        o_ref, x_ref, wo_ref, g_ref, wup_hbm, wdn_hbm, out_ref, wup_ref, wdn_ref, sem = refs

    @pl.when(pl.program_id(0) == 0)
    def _():
        copies = [pltpu.make_async_copy(wup_hbm.at[layer], wup_ref, sem.at[0]),
                  pltpu.make_async_copy(wdn_hbm.at[layer], wdn_ref, sem.at[1])]
        for cp in copies:
            cp.start()
        for cp in copies:
            cp.wait()

    if pair_major:
        o = jnp.concatenate([o_ref[j] for j in range(o_ref.shape[0])], axis=-1)
    else:
        o = o_ref[...]
    x1 = x_ref[...] + jnp.dot(o, wo_ref[...], preferred_element_type=F32)
    h = _rms(x1, g_ref[...]).astype(BF16)
    acc = x1
    for c in range(D_FF // FF_CHUNK):
        sl = slice(c * FF_CHUNK, (c + 1) * FF_CHUNK)
        u = jnp.dot(h, wup_ref[:, sl], preferred_element_type=F32)
        u = jnp.square(jnp.maximum(u, 0.0)).astype(BF16)
        acc = acc + jnp.dot(u, wdn_ref[sl, :], preferred_element_type=F32)
    if final:
        acc = _rms(acc, gf_ref[...])
    out_ref[...] = acc


def _out_mlp(o, x, wo, g, wup_all, wdn_all, layer, gf, tm, pair_major):
    m = x.shape[0]
    final = gf is not None
    if pair_major:
        tiles_per_group = o.shape[2] // tm
        o_spec = pl.BlockSpec((None, o.shape[1], tm, LANES),
                              lambda i: (i // tiles_per_group, 0, i % tiles_per_group, 0))
    else:
        o_spec = pl.BlockSpec((tm, D_MODEL), lambda i: (i, 0))
    row_spec = pl.BlockSpec((tm, D_MODEL), lambda i: (i, 0))
    in_hbm = pl.BlockSpec(memory_space=pl.ANY)
    in_specs = [o_spec, row_spec, _resident(), _resident(), in_hbm, in_hbm]
    args = [o, x, wo, g, wup_all, wdn_all]
    if final:
        in_specs.append(_resident())
        args.append(gf)
    return pl.pallas_call(
        functools.partial(_out_mlp_kernel, pair_major=pair_major, final=final, layer=layer),
        grid=(m // tm,),
        in_specs=in_specs,
        out_specs=row_spec,
        out_shape=jax.ShapeDtypeStruct((m, D_MODEL), F32),
        scratch_shapes=[
            pltpu.VMEM((D_MODEL, D_FF), BF16),
            pltpu.VMEM((D_FF, D_MODEL), BF16),
            pltpu.SemaphoreType.DMA((2,)),
        ],
        compiler_params=_params(1),
        name="out_proj_mlp",
    )(*args)


N_PAIRS = B_HEADS // 2
ROPE_LANE0 = NOPE_DIM
C_IN_EXT = Q_LORA + KV_LORA + LANES


def _rope_slab(t, c, s1, s2):
    return t * c + pltpu.roll(t, HALF_ROPE, 1) * s1 + pltpu.roll(t, LANES - HALF_ROPE, 1) * s2


def _mla_proj_kernel(x_ref, g_ref, win_ref, qn_ref, kvn_ref, wqb_ref, wk_ref, wvt_ref,
                     c_ref, s1_ref, s2_ref,
                     qcat_ref, kcat_ref, vt_ref, lat_ref, krt_ref, *, q_scale):
    h = _rms(x_ref[...], g_ref[...]).astype(BF16)
    c = jnp.dot(h, win_ref[...], preferred_element_type=F32)
    cq = _rms(c[:, :Q_LORA], qn_ref[...]).astype(BF16)
    lat = _rms(c[:, Q_LORA:Q_LORA + KV_LORA], kvn_ref[...])
    lat_ref[...] = lat
    cos, s1, s2 = c_ref[...], s1_ref[...], s2_ref[...]
    krp = _rope_slab(c[:, Q_LORA + KV_LORA:], cos, s1, s2)
    krt_ref[...] = krp.T[ROPE_LANE0:ROPE_LANE0 + ROPE_DIM, :]
    q = jnp.dot(cq, wqb_ref[...], preferred_element_type=F32)
    latb = lat.astype(BF16)
    kn = jnp.dot(latb, wk_ref[...], preferred_element_type=F32)
    vt = lax.dot_general(wvt_ref[...], latb, NT_DIMS, preferred_element_type=F32)
    for hh in range(B_HEADS):
        sl = slice(hh * LANES, (hh + 1) * LANES)
        qh = _rope_slab(q[:, sl], cos, s1, s2)
        if q_scale != 1.0:
            qh = qh * q_scale
        qcat_ref[hh] = qh.astype(BF16)
        kcat_ref[hh] = (kn[:, sl] + krp).astype(BF16)
    for j in range(N_PAIRS):
        vt_ref[j] = vt[j * LANES:(j + 1) * LANES, :].astype(BF16)


def _mla_proj(x, g, win, qn, kvn, wqb, wk, wvt, tabs, groups, rows, tm, tab_blocks, q_scale):
    m = x.shape[0]
    tiles = rows // tm
    head_map = lambda i: (i // tiles, 0, i % tiles, 0)
    tab_spec = pl.BlockSpec((tm, LANES), lambda i: (i % tab_blocks, 0))
    return pl.pallas_call(
        functools.partial(_mla_proj_kernel, q_scale=q_scale),
        grid=(m // tm,),
        in_specs=[pl.BlockSpec((tm, D_MODEL), lambda i: (i, 0))] + [_resident()] * 7 + [tab_spec] * 3,
        out_specs=[
            pl.BlockSpec((None, B_HEADS, tm, LANES), head_map),
            pl.BlockSpec((None, B_HEADS, tm, LANES), head_map),
            pl.BlockSpec((None, N_PAIRS, LANES, tm), lambda i: (i // tiles, 0, 0, i % tiles)),
            pl.BlockSpec((tm, KV_LORA), lambda i: (i, 0)),
            pl.BlockSpec((None, ROPE_DIM, tm), lambda i: (i // tiles, 0, i % tiles)),
        ],
        out_shape=[
            jax.ShapeDtypeStruct((groups, B_HEADS, rows, LANES), BF16),
            jax.ShapeDtypeStruct((groups, B_HEADS, rows, LANES), BF16),
            jax.ShapeDtypeStruct((groups, N_PAIRS, LANES, rows), BF16),
            jax.ShapeDtypeStruct((m, KV_LORA), F32),
            jax.ShapeDtypeStruct((groups, ROPE_DIM, rows), F32),
        ],
        compiler_params=_params(1),
        name="mla_proj",
    )(x, g, win, qn, kvn, wqb, wk, wvt, *tabs)


def _rope_tables(pos):
    pos = np.asarray(pos, np.float64)
    inv = ROPE_THETA ** (-np.arange(0, ROPE_DIM, 2, dtype=np.float64) / ROPE_DIM)
    ang = pos[:, None] * inv[None, :]
    cos, sin = np.cos(ang).astype(np.float32), np.sin(ang).astype(np.float32)
    s = pos.shape[0]
    ones = np.ones((s, ROPE_LANE0), np.float32)
    z16 = np.zeros((s, HALF_ROPE), np.float32)
    z64 = np.zeros((s, ROPE_LANE0), np.float32)
    z32 = np.zeros((s, LANES - ROPE_LANE0 - ROPE_DIM), np.float32)
    c = np.concatenate([ones, cos, cos, z32], axis=1)
    s1 = np.concatenate([z64, z16, sin, z32], axis=1)
    s2 = np.concatenate([z64, -sin, z16, z32], axis=1)
    return jnp.asarray(c), jnp.asarray(s1), jnp.asarray(s2)


HEADS_PER_BODY = 8
REDUCE_ROWS = 64


def _col_reduce(x, op):
    n, w = x.shape
    part = op(x.reshape(n // REDUCE_ROWS, REDUCE_ROWS, w), axis=0)
    return op(part, axis=0, keepdims=True)


def _mla_attn_kernel(q_ref, k_ref, vt_ref, o_ref, *, tq, n_tiles):
    qi = pl.program_id(1)
    key = lax.broadcasted_iota(jnp.int32, (tq, tq), 0)
    qry = lax.broadcasted_iota(jnp.int32, (tq, tq), 1)
    causal = key <= qry

    def scores(h, n_keys):
        k = k_ref[h, pl.ds(0, n_keys), :]
        return lax.dot_general(k, q_ref[h], NT_DIMS, preferred_element_type=F32)

    def attend(st, j, n_keys):
        diag = jnp.where(causal, st[n_keys - tq:, :], NEG)
        if n_keys > tq:
            full = st[:n_keys - tq, :]
            m = jnp.maximum(_col_reduce(full, jnp.max), _col_reduce(diag, jnp.max))
            p = jnp.concatenate([jnp.exp2(full - m), jnp.exp2(diag - m)], axis=0)
        else:
            m = _col_reduce(diag, jnp.max)
            p = jnp.exp2(diag - m)
        l = _col_reduce(p, jnp.sum)
        ot = jnp.dot(vt_ref[j, :, pl.ds(0, n_keys)], p.astype(BF16), preferred_element_type=F32)
        return ot / l

    for c in range(n_tiles):
        @pl.when(qi == c)
        def _(c=c):
            n_keys = (c + 1) * tq

            def body(jj, carry):
                heads = [HEADS_PER_BODY * jj + u for u in range(HEADS_PER_BODY)]
                sts = [scores(h, n_keys) for h in heads]
                outs = [attend(st, (HEADS_PER_BODY // 2) * jj + u // 2, n_keys) for u, st in enumerate(sts)]
                for u in range(0, HEADS_PER_BODY, 2):
                    ot = jnp.concatenate([outs[u][:V_DIM], outs[u + 1][V_DIM:]], axis=0)
                    o_ref[(HEADS_PER_BODY // 2) * jj + u // 2] = ot.T.astype(BF16)
                return carry

            lax.fori_loop(0, B_HEADS // HEADS_PER_BODY, body, 0)


def _mla_attn(qcat, kcat, vt, tq):
    batch, _, seq, _ = qcat.shape
    n_tiles = seq // tq
    return pl.pallas_call(
        functools.partial(_mla_attn_kernel, tq=tq, n_tiles=n_tiles),
        grid=(batch, n_tiles),
        in_specs=[
            pl.BlockSpec((None, B_HEADS, tq, LANES), lambda b, i: (b, 0, i, 0)),
            pl.BlockSpec((None, B_HEADS, seq, LANES), lambda b, i: (b, 0, 0, 0)),
            pl.BlockSpec((None, N_PAIRS, LANES, seq), lambda b, i: (b, 0, 0, 0)),
        ],
        out_specs=pl.BlockSpec((None, N_PAIRS, tq, LANES), lambda b, i: (b, 0, i, 0)),
        out_shape=jax.ShapeDtypeStruct((batch, N_PAIRS, seq, LANES), BF16),
        compiler_params=_params(2),
        name="mla_prompt_attn",
    )(qcat, kcat, vt)


def _absorb_kernel(qcat_ref, wuk_ref, qlat_ref):
    for h in range(B_HEADS):
        qlat_ref[h] = jnp.dot(qcat_ref[h], wuk_ref[h], preferred_element_type=F32).astype(BF16)


def _absorb(qcat, wuk_ext):
    nb = qcat.shape[1]
    return pl.pallas_call(
        _absorb_kernel,
        in_specs=[_resident(), _resident()],
        out_specs=_resident(),
        out_shape=jax.ShapeDtypeStruct((B_HEADS, nb, KV_LORA), BF16),
        name="mla_absorb_q",
    )(qcat, wuk_ext)


def _decode_kernel(pt_ref, qlat_ref, qr_ref, latnew_ref, krnew_ref, lat_hbm, krt_hbm, o_ref,
                   latbuf, krbuf, sem, m_sc, l_sc, acc_sc,
                   *, pages_per_step, steps_per_batch, n_chains, scale):
    b = pl.program_id(0)
    c = pl.program_id(1)
    step = b * steps_per_batch + c
    n_steps = pl.num_programs(0) * steps_per_batch
    last_step = step + 1 == n_steps
    n_slots = latbuf.shape[0]
    slot = step % n_slots
    page_size = latbuf.shape[2]

    def start_chunk(chunk, sl):
        for p in range(pages_per_step):
            page = pt_ref[chunk * pages_per_step + p]
            pltpu.make_async_copy(lat_hbm.at[page], latbuf.at[sl, p], sem.at[0, sl]).start()
            pltpu.make_async_copy(krt_hbm.at[page], krbuf.at[sl, p], sem.at[1, sl]).start()

    def wait_chunk(sl):
        pltpu.make_async_copy(lat_hbm.at[pl.ds(0, pages_per_step)], latbuf.at[sl], sem.at[0, sl]).wait()
        pltpu.make_async_copy(krt_hbm.at[pl.ds(0, pages_per_step)], krbuf.at[sl], sem.at[1, sl]).wait()

    @pl.when(step == 0)
    def _():
        for k in range(n_slots - 1):
            start_chunk(k, k)

    ahead = n_slots - 1
    start_chunk(jnp.minimum(step + ahead, n_steps - 1), (step + ahead) % n_slots)

    qlat = qlat_ref[...]
    qr = qr_ref[...]

    @pl.when(c == 0)
    def _():
        ln = latnew_ref[...].astype(BF16).astype(F32)
        kn = krnew_ref[...].astype(BF16).astype(F32)
        s_new = (jnp.sum(qlat.astype(F32) * ln, axis=-1, keepdims=True)
                 + jnp.sum(qr.astype(F32) * kn, axis=-1, keepdims=True)) * scale
        m_sc[...] = jnp.full(m_sc.shape, NEG, F32)
        l_sc[...] = jnp.zeros(l_sc.shape, F32)
        acc_sc[...] = jnp.zeros(acc_sc.shape, F32)
        m_sc[0] = s_new
        l_sc[0] = jnp.ones_like(s_new)
        acc_sc[0] = jnp.broadcast_to(ln, acc_sc.shape[1:])

    wait_chunk(slot)

    ppc = pages_per_step // n_chains
    chains = range(n_chains)
    s_rope = [jnp.concatenate(
        [jnp.dot(qr, krbuf[slot, i * ppc + p].astype(BF16), preferred_element_type=F32)
         for p in range(ppc)], axis=1) for i in chains]
    lat = [latbuf[slot, pl.ds(i * ppc, ppc)].reshape(ppc * page_size, KV_LORA).astype(BF16) for i in chains]
    s = [(lax.dot_general(qlat, lat[i], NT_DIMS, preferred_element_type=F32) + s_rope[i]) * scale
         for i in chains]
    m_old = [m_sc[i] for i in chains]
    m_new = [jnp.maximum(m_old[i], jnp.max(s[i], axis=-1, keepdims=True)) for i in chains]
    alpha = [jnp.exp(m_old[i] - m_new[i]) for i in chains]
    p = [jnp.exp(s[i] - m_new[i]) for i in chains]
    pv = [jnp.dot(p[i].astype(BF16), lat[i], preferred_element_type=F32) for i in chains]
    for i in chains:
        l_sc[i] = alpha[i] * l_sc[i] + jnp.sum(p[i], axis=-1, keepdims=True)
        acc_sc[i] = alpha[i] * acc_sc[i] + pv[i]
        m_sc[i] = m_new[i]

    @pl.when(c == steps_per_batch - 1)
    def _():
        m = m_sc[0]
        for i in range(1, n_chains):
            m = jnp.maximum(m, m_sc[i])
        l = jnp.zeros_like(m)
        acc = jnp.zeros(acc_sc.shape[1:], F32)
        for i in range(n_chains):
            w = jnp.exp(m_sc[i] - m)
            l = l + w * l_sc[i]
            acc = acc + w * acc_sc[i]
        o_ref[...] = acc / l

    @pl.when(last_step)
    def _():
        for k in range(1, n_slots):
            wait_chunk((step + k) % n_slots)


def _mla_decode(page_table, qlat, qr, lat_new, kr_new, lat_pool, krt_pool, pages_per_step, n_chains):
    nb, n_pages = page_table.shape
    page_size = lat_pool.shape[1]
    steps_per_batch = n_pages // pages_per_step
    scale = (NOPE_DIM + ROPE_DIM) ** -0.5
    per_b = lambda b, c, pt: (b, 0, 0)
    grid_spec = pltpu.PrefetchScalarGridSpec(
        num_scalar_prefetch=1,
        grid=(nb, steps_per_batch),
        in_specs=[
            pl.BlockSpec((None, B_HEADS, KV_LORA), per_b),
            pl.BlockSpec((None, B_HEADS, ROPE_DIM), per_b),
            pl.BlockSpec((None, 1, KV_LORA), per_b),
            pl.BlockSpec((None, 1, ROPE_DIM), per_b),
            pl.BlockSpec(memory_space=pl.ANY),
            pl.BlockSpec(memory_space=pl.ANY),
        ],
        out_specs=pl.BlockSpec((None, B_HEADS, KV_LORA), per_b),
        scratch_shapes=[
            pltpu.VMEM((DECODE_SLOTS, pages_per_step, page_size, KV_LORA), F32),
            pltpu.VMEM((DECODE_SLOTS, pages_per_step, ROPE_DIM, page_size), F32),
            pltpu.SemaphoreType.DMA((2, DECODE_SLOTS)),
            pltpu.VMEM((n_chains, B_HEADS, 1), F32),
            pltpu.VMEM((n_chains, B_HEADS, 1), F32),
            pltpu.VMEM((n_chains, B_HEADS, KV_LORA), F32),
        ],
    )
    return pl.pallas_call(
        functools.partial(_decode_kernel, pages_per_step=pages_per_step,
                          steps_per_batch=steps_per_batch, n_chains=n_chains, scale=scale),
        grid_spec=grid_spec,
        out_shape=jax.ShapeDtypeStruct((nb, B_HEADS, KV_LORA), F32),
        compiler_params=_params(2),
        name="mla_decode_attn",
    )(page_table.reshape(-1), qlat, qr, lat_new, kr_new, lat_pool, krt_pool)


def _uv_kernel(olat_ref, wuv_ref, o_ref):
    for h in range(B_HEADS):
        o_ref[:, h * V_DIM:(h + 1) * V_DIM] = jnp.dot(
            olat_ref[h].astype(BF16), wuv_ref[h], preferred_element_type=F32).astype(BF16)


def _uv(olat_hm, wuv):
    nb = olat_hm.shape[1]
    return pl.pallas_call(
        _uv_kernel,
        in_specs=[_resident(), _resident()],
        out_specs=_resident(),
        out_shape=jax.ShapeDtypeStruct((nb, B_HEADS * V_DIM), BF16),
        name="mla_value_up",
    )(olat_hm, wuv)


QKV_TM = 1024
PROMPT_TM = 1024
MLP_TM = 1024
MLA_TQ = 256
DECODE_PAGES = 32
DECODE_CHAINS = 4
DECODE_SLOTS = 3


def kernel(x_prompt, x_sample, cache_a_k, cache_a_v, cache_b_latent, cache_b_krope, page_table,
           rel_bias, norm_mix, norm_mlp, norm_final, a_w_qkv, a_w_o, a_sinks,
           b_w_in, b_q_norm, b_kv_norm, b_w_q_b, b_w_kv_b, b_w_o, mlp_w_up, mlp_w_down):
    batch, seq, _ = x_prompt.shape
    nb_s = x_sample.shape[0]
    past = page_table.shape[1] * cache_b_latent.shape[2]
    wb = cache_a_k.shape[2]

    xp = x_prompt.reshape(batch * seq, D_MODEL)
    xs = x_sample.reshape(nb_s, D_MODEL)
    row = lambda t: t.reshape(1, -1)

    order = jnp.array(SWA_HEAD_ORDER)
    inverse = jnp.argsort(order)
    w_q = a_w_qkv[0][:, :NQ_A].reshape(D_MODEL, A_HEADS, A_HEAD_DIM)[:, order].reshape(D_MODEL, NQ_A)
    w_qkvt = jnp.concatenate([w_q, a_w_qkv[0][:, NQ_A:]], axis=1).T.astype(BF16)
    w_o_a = a_w_o[0].reshape(A_HEADS, A_HEAD_DIM, D_MODEL)[order].reshape(NQ_A, D_MODEL).astype(BF16)
    w_up, w_dn = mlp_w_up.astype(BF16), mlp_w_down.astype(BF16)
    bias = _bias_table(rel_bias)
    sinks = a_sinks[0]
    wbp = min(WINDOW, seq)

    qtp, kp, vtp, kt_tail, vt_tail = _norm_qkv(xp, row(norm_mix[0]), w_qkvt, QKV_TM, batch, wbp)
    op = _swa_prompt(qtp, kp, vtp, bias, sinks, batch, seq)
    xp = _out_mlp(op, xp, w_o_a, row(norm_mlp[0]), w_up, w_dn, 0, None, MLP_TM, False)

    qts, kts, vts = _norm_qkv(xs, row(norm_mix[0]), w_qkvt, nb_s)
    qs3 = qts.T.astype(F32).reshape(nb_s, A_HEADS, A_HEAD_DIM)[:, inverse]
    bias_row = jnp.transpose(
        bias[0].reshape(N_KV_SLABS, 2 * WINDOW, HEADS_PER_KV_SLAB, WINDOW)[:, WINDOW:, :, WINDOW - 1],
        (0, 2, 1)).reshape(A_HEADS, WINDOW)
    to_t = lambda c: jnp.transpose(c[0], (0, 2, 3, 1)).reshape(nb_s, NKV_A, wb)
    from_t = lambda t, n, w: jnp.transpose(t.reshape(n, A_KV_HEADS, A_HEAD_DIM, w), (0, 3, 1, 2))[None]
    os3, a_k_st, a_v_st = _swa_sample(qs3, kts, vts, to_t(cache_a_k), to_t(cache_a_v), bias_row, sinks)
    os_ = os3[:, order].reshape(nb_s, NQ_A).astype(BF16)
    xs = _out_mlp(os_, xs, w_o_a, row(norm_mlp[0]), w_up, w_dn, 0, None, nb_s, False)

    w_in = b_w_in[0]
    zeros = lambda n: jnp.zeros((D_MODEL, n), F32)
    w_in_ext = jnp.concatenate([w_in[:, :Q_LORA + KV_LORA], zeros(ROPE_LANE0),
                                w_in[:, Q_LORA + KV_LORA:], zeros(LANES - ROPE_LANE0 - ROPE_DIM)],
                               axis=1).astype(BF16)
    qk_dim = NOPE_DIM + ROPE_DIM
    w_qb = jnp.pad(b_w_q_b[0].reshape(Q_LORA, B_HEADS, qk_dim),
                   ((0, 0), (0, 0), (0, LANES - qk_dim))).reshape(Q_LORA, B_HEADS * LANES).astype(BF16)
    w_kvb = b_w_kv_b[0]
    w_uk, w_uv = w_kvb[..., :NOPE_DIM], w_kvb[..., NOPE_DIM:]
    w_k = jnp.pad(w_uk, ((0, 0), (0, 0), (0, LANES - NOPE_DIM))).reshape(KV_LORA, B_HEADS * LANES).astype(BF16)
    w_vt = w_uv.reshape(KV_LORA, B_HEADS * V_DIM).T.astype(BF16)
    w_o_b = b_w_o[0].astype(BF16)
    proj_w = (row(norm_mix[1]), w_in_ext, row(b_q_norm[0]), row(b_kv_norm[0]), w_qb, w_k, w_vt)

    q_scale_p = (NOPE_DIM + ROPE_DIM) ** -0.5 * LOG2E
    tabs_p = _rope_tables(np.arange(seq))
    qcat, kcat, vt, lat_p, krt_p = _mla_proj(xp, *proj_w, tabs_p, batch, seq, PROMPT_TM, seq // PROMPT_TM,
                                            q_scale_p)
    o_pm = _mla_attn(qcat, kcat, vt, MLA_TQ)
    yp = _out_mlp(o_pm, xp, w_o_b, row(norm_mlp[1]), w_up, w_dn, 1, row(norm_final), MLP_TM, True)

    tabs_s = _rope_tables(np.full((nb_s,), past))
    qcat_s, _, _, lat_s, krt_s = _mla_proj(xs, *proj_w, tabs_s, 1, nb_s, nb_s, 1, 1.0)
    w_uk_ext = jnp.pad(jnp.transpose(w_uk, (1, 2, 0)), ((0, 0), (0, LANES - NOPE_DIM), (0, 0))).astype(BF16)
    qlat = jnp.transpose(_absorb(qcat_s[0], w_uk_ext), (1, 0, 2))
    qr = jnp.transpose(qcat_s[0, :, :, ROPE_LANE0:ROPE_LANE0 + ROPE_DIM], (1, 0, 2))
    kr_s = krt_s[0].T
    krt_pool = jnp.swapaxes(cache_b_krope[0], 1, 2)
    olat = _mla_decode(page_table, qlat, qr, lat_s.reshape(nb_s, 1, KV_LORA), kr_s.reshape(nb_s, 1, ROPE_DIM),
                       cache_b_latent[0], krt_pool, DECODE_PAGES, DECODE_CHAINS)
    o_s = _uv(jnp.transpose(olat, (1, 0, 2)), jnp.transpose(w_uv, (1, 0, 2)).astype(BF16))
    ys = _out_mlp(o_s, xs, w_o_b, row(norm_mlp[1]), w_up, w_dn, 1, row(norm_final), nb_s, False)

    k4 = from_t(kt_tail, batch, wbp)
    v4 = from_t(vt_tail, batch, wbp)
    kr_p = jnp.swapaxes(krt_p, 1, 2)
    return (
        yp.reshape(batch, seq, D_MODEL),
        ys.reshape(nb_s, 1, D_MODEL),
        k4, v4,
        lat_p.reshape(1, batch, seq, KV_LORA),
        kr_p.reshape(1, batch, seq, ROPE_DIM),
        from_t(a_k_st, nb_s, wb),
        from_t(a_v_st, nb_s, wb),
        lat_s.reshape(1, nb_s, 1, KV_LORA),
        kr_s.reshape(1, nb_s, 1, ROPE_DIM),
    )
```

```python
import functools
import math

import jax
import jax.numpy as jnp
import numpy as np
from jax import lax
from jax.experimental import pallas as pl
from jax.experimental.pallas import tpu as pltpu

F32 = jnp.float32
BF16 = jnp.bfloat16

D_MODEL = 1024
A_HEADS = 16
A_KV_HEADS = 4
A_HEAD_DIM = 64
A_GROUP = A_HEADS // A_KV_HEADS
WINDOW = 128
N_BUCKETS = 32
MAX_EXACT = N_BUCKETS // 2
MAX_DISTANCE = 128
B_HEADS = 16
Q_LORA = 768
KV_LORA = 256
NOPE_DIM = 64
ROPE_DIM = 32
HALF_ROPE = ROPE_DIM // 2
V_DIM = 64
ROPE_THETA = 10000.0
D_FF = 4 * D_MODEL
EPS = 1e-6

LANES = 128
VMEM_LIMIT_BYTES = 56 * 1024 * 1024

NEG = -1e30
LOG2E = math.log2(math.e)

NT_DIMS = (((1,), (1,)), ((), ()))


def _rms(x, g):
    return x * lax.rsqrt(jnp.mean(x * x, axis=-1, keepdims=True) + EPS) * g


def _params(n_axes):
    return pltpu.CompilerParams(
        dimension_semantics=("arbitrary",) * n_axes,
        vmem_limit_bytes=VMEM_LIMIT_BYTES,
    )


def _resident():
    return pl.BlockSpec(memory_space=pltpu.VMEM)


NQ_A = A_HEADS * A_HEAD_DIM
NKV_A = A_KV_HEADS * A_HEAD_DIM


def _qkv_kernel(x_ref, g_ref, wt_ref, *out_refs, tail_rows, tiles_per_group):
    h = _rms(x_ref[...], g_ref[...]).astype(BF16)
    qkvt = lax.dot_general(wt_ref[...], h, NT_DIMS, preferred_element_type=F32)
    out_refs[0][...] = (qkvt[:NQ_A] * (A_HEAD_DIM ** -0.5 * LOG2E)).astype(BF16)
    kvt = qkvt[NQ_A:]
    if tail_rows:
        _, k_ref, vt_ref, ktail_ref, vtail_ref = out_refs
        k_ref[...] = kvt[:NKV_A].T
        vt_ref[...] = kvt[NKV_A:].astype(BF16)

        @pl.when(pl.program_id(0) % tiles_per_group == tiles_per_group - 1)
        def _():
            ktail_ref[...] = kvt[:NKV_A, kvt.shape[1] - tail_rows:]
            vtail_ref[...] = kvt[NKV_A:, kvt.shape[1] - tail_rows:]
    else:
        _, kt_ref, vt_ref = out_refs
        kt_ref[...] = kvt[:NKV_A]
        vt_ref[...] = kvt[NKV_A:]


def _norm_qkv(x, g, wt_bf16, tm, groups=1, tail_rows=0):
    m = x.shape[0]
    tiles_per_group = m // groups // tm
    col_blk = lambda i: (0, i)
    in_specs = [pl.BlockSpec((tm, D_MODEL), lambda i: (i, 0)), _resident(), _resident()]
    qt_spec = pl.BlockSpec((NQ_A, tm), col_blk)
    qt_shape = jax.ShapeDtypeStruct((NQ_A, m), BF16)
    if tail_rows:
        tail_spec = pl.BlockSpec((None, NKV_A, tail_rows), lambda i: (i // tiles_per_group, 0, 0))
        tail_shape = jax.ShapeDtypeStruct((groups, NKV_A, tail_rows), F32)
        out_specs = [qt_spec, pl.BlockSpec((tm, NKV_A), lambda i: (i, 0)),
                     pl.BlockSpec((NKV_A, tm), col_blk), tail_spec, tail_spec]
        out_shape = [qt_shape, jax.ShapeDtypeStruct((m, NKV_A), F32),
                     jax.ShapeDtypeStruct((NKV_A, m), BF16), tail_shape, tail_shape]
    else:
        out_specs = [qt_spec, pl.BlockSpec((NKV_A, tm), col_blk), pl.BlockSpec((NKV_A, tm), col_blk)]
        out_shape = [qt_shape, jax.ShapeDtypeStruct((NKV_A, m), F32), jax.ShapeDtypeStruct((NKV_A, m), F32)]
    return pl.pallas_call(
        functools.partial(_qkv_kernel, tail_rows=tail_rows, tiles_per_group=tiles_per_group),
        grid=(m // tm,),
        in_specs=in_specs,
        out_specs=out_specs,
        out_shape=out_shape,
        compiler_params=_params(1),
        name="norm_qkv",
    )(x, g, wt_bf16)


def _t5_bucket(dist):
    d = np.maximum(dist, 0)
    df = np.maximum(d, 1).astype(np.float32)
    large = MAX_EXACT + (np.log(df / np.float32(MAX_EXACT)) / np.float32(math.log(MAX_DISTANCE / MAX_EXACT))
                         * np.float32(N_BUCKETS - MAX_EXACT)).astype(np.int32)
    large = np.minimum(large, N_BUCKETS - 1)
    return np.where(d < MAX_EXACT, d, large)


HEADS_PER_SLAB = LANES // A_HEAD_DIM
N_KV_SLABS = NKV_A // LANES
HEADS_PER_KV_SLAB = HEADS_PER_SLAB * A_GROUP
TILE_COLS = HEADS_PER_KV_SLAB * WINDOW


def _bias_kernel(bucket_ref, rb_ref, out_ref):
    bucket = bucket_ref[...]
    has_prev = lax.broadcasted_iota(jnp.int32, bucket.shape, 0) >= WINDOW
    for h in range(A_HEADS):
        acc = jnp.full(bucket.shape, NEG, F32)
        for b in range(N_BUCKETS):
            acc = jnp.where(bucket == b, rb_ref[b, h], acc)
        acc = acc * LOG2E
        t, hh = divmod(h, HEADS_PER_KV_SLAB)
        out_ref[0, t, :, hh * WINDOW:(hh + 1) * WINDOW] = acc
        out_ref[1, t, :, hh * WINDOW:(hh + 1) * WINDOW] = jnp.where(has_prev, acc, NEG)


def _bias_table(rel_bias):
    kj = np.arange(2 * WINDOW, dtype=np.int32)[:, None]
    qi = np.arange(WINDOW, dtype=np.int32)[None, :]
    dist = qi + WINDOW - kj
    valid = (dist >= 0) & (dist < WINDOW)
    bucket = jnp.asarray(np.where(valid, _t5_bucket(dist), -1).astype(np.int32))
    return pl.pallas_call(
        _bias_kernel,
        in_specs=[_resident(), pl.BlockSpec(memory_space=pltpu.SMEM)],
        out_specs=_resident(),
        out_shape=jax.ShapeDtypeStruct((2, N_KV_SLABS, 2 * WINDOW, TILE_COLS), F32),
        name="t5_bias_table",
    )(bucket, rel_bias)


SWA_BLOCKS = 8
SWA_HEAD_ORDER = tuple(
    HEADS_PER_KV_SLAB * t + A_GROUP * half + i
    for t in range(N_KV_SLABS) for i in range(A_GROUP) for half in range(HEADS_PER_SLAB))


def _swa_prompt_kernel(qt_ref, kp_ref, kc_ref, vtp_ref, vtc_ref, bias0_ref, bias_ref, sink_ref, o_ref):
    k = jnp.concatenate([kp_ref[...], kc_ref[...]], axis=0).astype(BF16)
    vt = jnp.concatenate([vtp_ref[...], vtc_ref[...]], axis=1)
    top = lax.broadcasted_iota(jnp.int32, (LANES, WINDOW), 0) < A_HEAD_DIM
    zero = jnp.zeros((LANES, WINDOW), BF16)
    units = [(blk, t) for blk in range(SWA_BLOCKS) for t in range(N_KV_SLABS)]

    scores = []
    for blk, t in units:
        cols = slice(blk * WINDOW, (blk + 1) * WINDOW)
        qs = [qt_ref[(A_GROUP * t + i) * LANES:(A_GROUP * t + i + 1) * LANES, cols] for i in range(A_GROUP)]
        rhs = jnp.concatenate([jnp.where(top, x, zero) for x in qs] + [jnp.where(top, zero, x) for x in qs],
                              axis=1)
        kwin = k[blk * WINDOW:(blk + 2) * WINDOW, t * LANES:(t + 1) * LANES]
        scores.append(jnp.dot(kwin, rhs, preferred_element_type=F32))

    probs, inv = [], []
    for (blk, t), s in zip(units, scores):
        s = s + (bias0_ref[t] if blk == 0 else bias_ref[t])
        sink = sink_ref[t] * LOG2E
        m = jnp.maximum(jnp.max(s, axis=0, keepdims=True), sink)
        e = jnp.exp2(s - m)
        inv.append(1.0 / (jnp.sum(e, axis=0, keepdims=True) + jnp.exp2(sink - m)))
        probs.append(e.astype(BF16))

    outs = []
    for (blk, t), p in zip(units, probs):
        vwin = vt[t * LANES:(t + 1) * LANES, blk * WINDOW:(blk + 2) * WINDOW]
        outs.append(jnp.dot(vwin, p, preferred_element_type=F32))

    for (blk, t), ot, r in zip(units, outs, inv):
        ot = ot * r
        for i in range(A_GROUP):
            first = ot[:, i * WINDOW:(i + 1) * WINDOW]
            second = ot[:, (A_GROUP + i) * WINDOW:(A_GROUP + i + 1) * WINDOW]
            o_ref[blk * WINDOW:(blk + 1) * WINDOW, (A_GROUP * t + i) * LANES:(A_GROUP * t + i + 1) * LANES] = (
                jnp.where(top, first, second).T.astype(BF16))


def _swa_prompt(qt, k, vt, bias, sinks, batch, seq):
    tq = SWA_BLOCKS * WINDOW
    steps = seq // tq
    blocks = seq // WINDOW
    sink_rows = jnp.repeat(sinks, WINDOW).reshape(N_KV_SLABS, 1, TILE_COLS)
    prev_blk = lambda b, j: b * blocks + jnp.maximum(SWA_BLOCKS * j - 1, 0)
    tile = (None, N_KV_SLABS, 2 * WINDOW, TILE_COLS)
    return pl.pallas_call(
        _swa_prompt_kernel,
        grid=(batch, steps),
        in_specs=[
            pl.BlockSpec((NQ_A, tq), lambda b, j: (0, b * steps + j)),
            pl.BlockSpec((WINDOW, NKV_A), lambda b, j: (prev_blk(b, j), 0)),
            pl.BlockSpec((tq, NKV_A), lambda b, j: (b * steps + j, 0)),
            pl.BlockSpec((NKV_A, WINDOW), lambda b, j: (0, prev_blk(b, j))),
            pl.BlockSpec((NKV_A, tq), lambda b, j: (0, b * steps + j)),
            pl.BlockSpec(tile, lambda b, j: (jnp.where(j == 0, 1, 0), 0, 0, 0)),
            _resident(),
            _resident(),
        ],
        out_specs=pl.BlockSpec((tq, NQ_A), lambda b, j: (b * steps + j, 0)),
        out_shape=jax.ShapeDtypeStruct((batch * seq, NQ_A), BF16),
        compiler_params=_params(2),
        name="swa_prompt_attn",
    )(qt, k, k, vt, vt, bias, bias[0], sink_rows)


def _swa_sample_kernel(q_ref, kbt_ref, vbt_ref, knt_ref, vnt_ref, bias_ref, sink_ref,
                       o_ref, kout_ref, vout_ref):
    sink_all = sink_ref[...] * LOG2E
    n_rows, _, wb = kbt_ref.shape
    newest = lax.broadcasted_iota(jnp.int32, (NKV_A, wb), 1) == wb - 1
    kb, vb = [], []
    for r in range(n_rows):
        kw = jnp.where(newest, knt_ref[:, r:r + 1], pltpu.roll(kbt_ref[r], wb - 1, 1))
        vw = jnp.where(newest, vnt_ref[:, r:r + 1], pltpu.roll(vbt_ref[r], wb - 1, 1))
        kout_ref[r] = kw
        vout_ref[r] = vw
        kb.append(kw.astype(BF16))
        vb.append(vw.astype(BF16))
    units = [(r, g) for r in range(n_rows) for g in range(A_KV_HEADS)]
    heads = lambda g: slice(g * A_GROUP, (g + 1) * A_GROUP)
    dims = lambda g: slice(g * A_HEAD_DIM, (g + 1) * A_HEAD_DIM)
    scores = [jnp.dot(q_ref[r, heads(g), :].astype(BF16), kb[r][dims(g), :], preferred_element_type=F32)
              for r, g in units]
    probs, inv = [], []
    for (r, g), s in zip(units, scores):
        s = s + bias_ref[heads(g), :]
        sink = sink_all[heads(g), :]
        m = jnp.maximum(jnp.max(s, axis=-1, keepdims=True), sink)
        e = jnp.exp2(s - m)
        inv.append(1.0 / (jnp.sum(e, axis=-1, keepdims=True) + jnp.exp2(sink - m)))
        probs.append(e.astype(BF16))
    outs = [lax.dot_general(p, vb[r][dims(g), :], NT_DIMS, preferred_element_type=F32)
            for (r, g), p in zip(units, probs)]
    for (r, g), pv, rcp in zip(units, outs, inv):
        o_ref[r, heads(g), :] = pv * rcp


def _swa_sample(q, kt_new, vt_new, kbuf_t, vbuf_t, bias_row, sinks):
    nb = kbuf_t.shape[0]
    return pl.pallas_call(
        _swa_sample_kernel,
        in_specs=[_resident()] * 7,
        out_specs=[_resident()] * 3,
        out_shape=[
            jax.ShapeDtypeStruct((nb, A_HEADS, A_HEAD_DIM), F32),
            jax.ShapeDtypeStruct(kbuf_t.shape, F32),
            jax.ShapeDtypeStruct(vbuf_t.shape, F32),
        ],
        compiler_params=pltpu.CompilerParams(vmem_limit_bytes=VMEM_LIMIT_BYTES),
        name="swa_sample_attn",
    )(q, kbuf_t, vbuf_t, kt_new, vt_new, bias_row, sinks.reshape(A_HEADS, 1))


FF_CHUNK = 1024


def _out_mlp_kernel(*refs, pair_major, final, layer):
    if final:
        o_ref, x_ref, wo_ref, g_ref, wup_hbm, wdn_hbm, gf_ref, out_ref, wup_ref, wdn_ref, sem = refs
    else:
        o_ref, x_ref, wo_ref, g_ref, wup_hbm, wdn_hbm, out_ref, wup_ref, wdn_ref, sem = refs

    @pl.when(pl.program_id(0) == 0)
    def _():
        copies = [pltpu.make_async_copy(wup_hbm.at[layer], wup_ref, sem.at[0]),
                  pltpu.make_async_copy(wdn_hbm.at[layer], wdn_ref, sem.at[1])]
        for cp in copies:
            cp.start()
        for cp in copies:
            cp.wait()

    if pair_major:
        o = jnp.concatenate([o_ref[j] for j in range(o_ref.shape[0])], axis=-1)
    else:
        o = o_ref[...]
    x1 = x_ref[...] + jnp.dot(o, wo_ref[...], preferred_element_type=F32)
    h = _rms(x1, g_ref[...]).astype(BF16)
    acc = x1
    for c in range(D_FF // FF_CHUNK):
        sl = slice(c * FF_CHUNK, (c + 1) * FF_CHUNK)
        u = jnp.dot(h, wup_ref[:, sl], preferred_element_type=F32)
        u = jnp.square(jnp.maximum(u, 0.0)).astype(BF16)
        acc = acc + jnp.dot(u, wdn_ref[sl, :], preferred_element_type=F32)
    if final:
        acc = _rms(acc, gf_ref[...])
    out_ref[...] = acc


def _out_mlp(o, x, wo, g, wup_all, wdn_all, layer, gf, tm, pair_major):
    m = x.shape[0]
    final = gf is not None
    if pair_major:
        tiles_per_group = o.shape[2] // tm
        o_spec = pl.BlockSpec((None, o.shape[1], tm, LANES),
                              lambda i: (i // tiles_per_group, 0, i % tiles_per_group, 0))
    else:
        o_spec = pl.BlockSpec((tm, D_MODEL), lambda i: (i, 0))
    row_spec = pl.BlockSpec((tm, D_MODEL), lambda i: (i, 0))
    in_hbm = pl.BlockSpec(memory_space=pl.ANY)
    in_specs = [o_spec, row_spec, _resident(), _resident(), in_hbm, in_hbm]
    args = [o, x, wo, g, wup_all, wdn_all]
    if final:
        in_specs.append(_resident())
        args.append(gf)
    return pl.pallas_call(
        functools.partial(_out_mlp_kernel, pair_major=pair_major, final=final, layer=layer),
        grid=(m // tm,),
        in_specs=in_specs,
        out_specs=row_spec,
        out_shape=jax.ShapeDtypeStruct((m, D_MODEL), F32),
        scratch_shapes=[
            pltpu.VMEM((D_MODEL, D_FF), BF16),
            pltpu.VMEM((D_FF, D_MODEL), BF16),
            pltpu.SemaphoreType.DMA((2,)),
        ],
        compiler_params=_params(1),
        name="out_proj_mlp",
    )(*args)


N_PAIRS = B_HEADS // 2
BF16_SUBLANES = 16
VT_ROWS = V_DIM + BF16_SUBLANES
ROPE_LANE0 = NOPE_DIM
C_IN_EXT = Q_LORA + KV_LORA + LANES


def _rope_slab(t, c, s1, s2):
    return t * c + pltpu.roll(t, HALF_ROPE, 1) * s1 + pltpu.roll(t, LANES - HALF_ROPE, 1) * s2


def _mla_proj_kernel(x_ref, g_ref, win_ref, qn_ref, kvn_ref, wqb_ref, wk_ref, wvt_ref,
                     c_ref, s1_ref, s2_ref,
                     qcat_ref, kcat_ref, vt_ref, lat_ref, krt_ref, *, q_scale):
    h = _rms(x_ref[...], g_ref[...]).astype(BF16)
    c = jnp.dot(h, win_ref[...], preferred_element_type=F32)
    cq = _rms(c[:, :Q_LORA], qn_ref[...]).astype(BF16)
    lat = _rms(c[:, Q_LORA:Q_LORA + KV_LORA], kvn_ref[...])
    lat_ref[...] = lat
    cos, s1, s2 = c_ref[...], s1_ref[...], s2_ref[...]
    krp = _rope_slab(c[:, Q_LORA + KV_LORA:], cos, s1, s2)
    krt_ref[...] = krp.T[ROPE_LANE0:ROPE_LANE0 + ROPE_DIM, :]
    q = jnp.dot(cq, wqb_ref[...], preferred_element_type=F32)
    latb = lat.astype(BF16)
    kn = jnp.dot(latb, wk_ref[...], preferred_element_type=F32)
    vt = lax.dot_general(wvt_ref[...], latb, NT_DIMS, preferred_element_type=F32)
    for hh in range(B_HEADS):
        sl = slice(hh * LANES, (hh + 1) * LANES)
        qh = _rope_slab(q[:, sl], cos, s1, s2)
        if q_scale != 1.0:
            qh = qh * q_scale
        qcat_ref[hh] = qh.astype(BF16)
        kcat_ref[hh] = (kn[:, sl] + krp).astype(BF16)
        vt_ref[hh, :V_DIM, :] = vt[hh * V_DIM:(hh + 1) * V_DIM, :].astype(BF16)
        vt_ref[hh, V_DIM:, :] = jnp.ones((VT_ROWS - V_DIM, vt.shape[1]), BF16)


def _mla_proj(x, g, win, qn, kvn, wqb, wk, wvt, tabs, groups, rows, tm, tab_blocks, q_scale):
    m = x.shape[0]
    tiles = rows // tm
    head_map = lambda i: (i // tiles, 0, i % tiles, 0)
    tab_spec = pl.BlockSpec((tm, LANES), lambda i: (i % tab_blocks, 0))
    return pl.pallas_call(
        functools.partial(_mla_proj_kernel, q_scale=q_scale),
        grid=(m // tm,),
        in_specs=[pl.BlockSpec((tm, D_MODEL), lambda i: (i, 0))] + [_resident()] * 7 + [tab_spec] * 3,
        out_specs=[
            pl.BlockSpec((None, B_HEADS, tm, LANES), head_map),
            pl.BlockSpec((None, B_HEADS, tm, LANES), head_map),
            pl.BlockSpec((None, B_HEADS, VT_ROWS, tm), lambda i: (i // tiles, 0, 0, i % tiles)),
            pl.BlockSpec((tm, KV_LORA), lambda i: (i, 0)),
            pl.BlockSpec((None, ROPE_DIM, tm), lambda i: (i // tiles, 0, i % tiles)),
        ],
        out_shape=[
            jax.ShapeDtypeStruct((groups, B_HEADS, rows, LANES), BF16),
            jax.ShapeDtypeStruct((groups, B_HEADS, rows, LANES), BF16),
            jax.ShapeDtypeStruct((groups, B_HEADS, VT_ROWS, rows), BF16),
            jax.ShapeDtypeStruct((m, KV_LORA), F32),
            jax.ShapeDtypeStruct((groups, ROPE_DIM, rows), F32),
        ],
        compiler_params=_params(1),
        name="mla_proj",
    )(x, g, win, qn, kvn, wqb, wk, wvt, *tabs)


def _rope_tables(pos):
    pos = np.asarray(pos, np.float64)
    inv = ROPE_THETA ** (-np.arange(0, ROPE_DIM, 2, dtype=np.float64) / ROPE_DIM)
    ang = pos[:, None] * inv[None, :]
    cos, sin = np.cos(ang).astype(np.float32), np.sin(ang).astype(np.float32)
    s = pos.shape[0]
    ones = np.ones((s, ROPE_LANE0), np.float32)
    z16 = np.zeros((s, HALF_ROPE), np.float32)
    z64 = np.zeros((s, ROPE_LANE0), np.float32)
    z32 = np.zeros((s, LANES - ROPE_LANE0 - ROPE_DIM), np.float32)
    c = np.concatenate([ones, cos, cos, z32], axis=1)
    s1 = np.concatenate([z64, z16, sin, z32], axis=1)
    s2 = np.concatenate([z64, -sin, z16, z32], axis=1)
    return jnp.asarray(c), jnp.asarray(s1), jnp.asarray(s2)


HEADS_PER_BODY = 8
REDUCE_ROWS = 64


def _col_reduce(x, op):
    n, w = x.shape
    part = op(x.reshape(n // REDUCE_ROWS, REDUCE_ROWS, w), axis=0)
    return op(part, axis=0, keepdims=True)


def _mla_attn_kernel(q_ref, k_ref, vt_ref, o_ref, *, tq, n_tiles):
    qi = pl.program_id(1)
    key = lax.broadcasted_iota(jnp.int32, (tq, tq), 0)
    qry = lax.broadcasted_iota(jnp.int32, (tq, tq), 1)
    causal = key <= qry

    def scores(h, n_keys):
        k = k_ref[h, pl.ds(0, n_keys), :]
        return lax.dot_general(k, q_ref[h], NT_DIMS, preferred_element_type=F32)

    def attend(st, h, n_keys):
        diag = jnp.where(causal, st[n_keys - tq:, :], NEG)
        if n_keys > tq:
            full = st[:n_keys - tq, :]
            m = jnp.maximum(_col_reduce(full, jnp.max), _col_reduce(diag, jnp.max))
            p = jnp.concatenate([jnp.exp2(full - m), jnp.exp2(diag - m)], axis=0)
        else:
            m = _col_reduce(diag, jnp.max)
            p = jnp.exp2(diag - m)
        ot = jnp.dot(vt_ref[h, :, pl.ds(0, n_keys)], p.astype(BF16), preferred_element_type=F32)
        return ot[:V_DIM] / ot[V_DIM:V_DIM + 1]

    for c in range(n_tiles):
        @pl.when(qi == c)
        def _(c=c):
            n_keys = (c + 1) * tq

            def body(jj, carry):
                heads = [HEADS_PER_BODY * jj + u for u in range(HEADS_PER_BODY)]
                sts = [scores(h, n_keys) for h in heads]
                outs = [attend(st, h, n_keys) for h, st in zip(heads, sts)]
                for u in range(0, HEADS_PER_BODY, 2):
                    ot = jnp.concatenate([outs[u], outs[u + 1]], axis=0)
                    o_ref[(HEADS_PER_BODY // 2) * jj + u // 2] = ot.T.astype(BF16)
                return carry

            lax.fori_loop(0, B_HEADS // HEADS_PER_BODY, body, 0)


def _mla_attn(qcat, kcat, vt, tq):
    batch, _, seq, _ = qcat.shape
    n_tiles = seq // tq
    return pl.pallas_call(
        functools.partial(_mla_attn_kernel, tq=tq, n_tiles=n_tiles),
        grid=(batch, n_tiles),
        in_specs=[
            pl.BlockSpec((None, B_HEADS, tq, LANES), lambda b, i: (b, 0, i, 0)),
            pl.BlockSpec((None, B_HEADS, seq, LANES), lambda b, i: (b, 0, 0, 0)),
            pl.BlockSpec((None, B_HEADS, VT_ROWS, seq), lambda b, i: (b, 0, 0, 0)),
        ],
        out_specs=pl.BlockSpec((None, N_PAIRS, tq, LANES), lambda b, i: (b, 0, i, 0)),
        out_shape=jax.ShapeDtypeStruct((batch, N_PAIRS, seq, LANES), BF16),
        compiler_params=_params(2),
        name="mla_prompt_attn",
    )(qcat, kcat, vt)


def _absorb_kernel(qcat_ref, wuk_ref, qlat_ref):
    for h in range(B_HEADS):
        qlat_ref[h] = jnp.dot(qcat_ref[h], wuk_ref[h], preferred_element_type=F32).astype(BF16)


def _absorb(qcat, wuk_ext):
    nb = qcat.shape[1]
    return pl.pallas_call(
        _absorb_kernel,
        in_specs=[_resident(), _resident()],
        out_specs=_resident(),
        out_shape=jax.ShapeDtypeStruct((B_HEADS, nb, KV_LORA), BF16),
        name="mla_absorb_q",
    )(qcat, wuk_ext)


def _decode_kernel(pt_ref, qlat_ref, qr_ref, latnew_ref, krnew_ref, lat_hbm, krt_hbm, o_ref,
                   latbuf, krbuf, sem, m_sc, l_sc, acc_sc,
                   *, pages_per_step, steps_per_batch, n_chains, scale):
    b = pl.program_id(0)
    c = pl.program_id(1)
    step = b * steps_per_batch + c
    n_steps = pl.num_programs(0) * steps_per_batch
    last_step = step + 1 == n_steps
    n_slots = latbuf.shape[0]
    slot = step % n_slots
    page_size = latbuf.shape[2]

    def start_chunk(chunk, sl):
        for p in range(pages_per_step):
            page = pt_ref[chunk * pages_per_step + p]
            pltpu.make_async_copy(lat_hbm.at[page], latbuf.at[sl, p], sem.at[0, sl]).start()
            pltpu.make_async_copy(krt_hbm.at[page], krbuf.at[sl, p], sem.at[1, sl]).start()

    def wait_chunk(sl):
        pltpu.make_async_copy(lat_hbm.at[pl.ds(0, pages_per_step)], latbuf.at[sl], sem.at[0, sl]).wait()
        pltpu.make_async_copy(krt_hbm.at[pl.ds(0, pages_per_step)], krbuf.at[sl], sem.at[1, sl]).wait()

    @pl.when(step == 0)
    def _():
        for k in range(n_slots - 1):
            start_chunk(k, k)

    ahead = n_slots - 1
    start_chunk(jnp.minimum(step + ahead, n_steps - 1), (step + ahead) % n_slots)

    qlat = qlat_ref[...]
    qr = qr_ref[...]

    @pl.when(c == 0)
    def _():
        ln = latnew_ref[...].astype(BF16).astype(F32)
        kn = krnew_ref[...].astype(BF16).astype(F32)
        s_new = (jnp.sum(qlat.astype(F32) * ln, axis=-1, keepdims=True)
                 + jnp.sum(qr.astype(F32) * kn, axis=-1, keepdims=True)) * scale
        m_sc[...] = jnp.full(m_sc.shape, NEG, F32)
        l_sc[...] = jnp.zeros(l_sc.shape, F32)
        acc_sc[...] = jnp.zeros(acc_sc.shape, F32)
        m_sc[0] = s_new
        l_sc[0] = jnp.ones_like(s_new)
        acc_sc[0] = jnp.broadcast_to(ln, acc_sc.shape[1:])

    wait_chunk(slot)

    ppc = pages_per_step // n_chains
    chains = range(n_chains)
    s_rope = [jnp.concatenate(
        [jnp.dot(qr, krbuf[slot, i * ppc + p].astype(BF16), preferred_element_type=F32)
         for p in range(ppc)], axis=1) for i in chains]
    lat = [latbuf[slot, pl.ds(i * ppc, ppc)].reshape(ppc * page_size, KV_LORA).astype(BF16) for i in chains]
    s = [(lax.dot_general(qlat, lat[i], NT_DIMS, preferred_element_type=F32) + s_rope[i]) * scale
         for i in chains]
    m_old = [m_sc[i] for i in chains]
    m_new = [jnp.maximum(m_old[i], jnp.max(s[i], axis=-1, keepdims=True)) for i in chains]
    alpha = [jnp.exp(m_old[i] - m_new[i]) for i in chains]
    p = [jnp.exp(s[i] - m_new[i]) for i in chains]
    pv = [jnp.dot(p[i].astype(BF16), lat[i], preferred_element_type=F32) for i in chains]
    for i in chains:
        l_sc[i] = alpha[i] * l_sc[i] + jnp.sum(p[i], axis=-1, keepdims=True)
        acc_sc[i] = alpha[i] * acc_sc[i] + pv[i]
        m_sc[i] = m_new[i]

    @pl.when(c == steps_per_batch - 1)
    def _():
        m = m_sc[0]
        for i in range(1, n_chains):
            m = jnp.maximum(m, m_sc[i])
        l = jnp.zeros_like(m)
        acc = jnp.zeros(acc_sc.shape[1:], F32)
        for i in range(n_chains):
            w = jnp.exp(m_sc[i] - m)
            l = l + w * l_sc[i]
            acc = acc + w * acc_sc[i]
        o_ref[...] = acc / l

    @pl.when(last_step)
    def _():
        for k in range(1, n_slots):
            wait_chunk((step + k) % n_slots)


def _mla_decode(page_table, qlat, qr, lat_new, kr_new, lat_pool, krt_pool, pages_per_step, n_chains):
    nb, n_pages = page_table.shape
    page_size = lat_pool.shape[1]
    steps_per_batch = n_pages // pages_per_step
    scale = (NOPE_DIM + ROPE_DIM) ** -0.5
    per_b = lambda b, c, pt: (b, 0, 0)
    grid_spec = pltpu.PrefetchScalarGridSpec(
        num_scalar_prefetch=1,
        grid=(nb, steps_per_batch),
        in_specs=[
            pl.BlockSpec((None, B_HEADS, KV_LORA), per_b),
            pl.BlockSpec((None, B_HEADS, ROPE_DIM), per_b),
            pl.BlockSpec((None, 1, KV_LORA), per_b),
            pl.BlockSpec((None, 1, ROPE_DIM), per_b),
            pl.BlockSpec(memory_space=pl.ANY),
            pl.BlockSpec(memory_space=pl.ANY),
        ],
        out_specs=pl.BlockSpec((None, B_HEADS, KV_LORA), per_b),
        scratch_shapes=[
            pltpu.VMEM((DECODE_SLOTS, pages_per_step, page_size, KV_LORA), F32),
            pltpu.VMEM((DECODE_SLOTS, pages_per_step, ROPE_DIM, page_size), F32),
            pltpu.SemaphoreType.DMA((2, DECODE_SLOTS)),
            pltpu.VMEM((n_chains, B_HEADS, 1), F32),
            pltpu.VMEM((n_chains, B_HEADS, 1), F32),
            pltpu.VMEM((n_chains, B_HEADS, KV_LORA), F32),
        ],
    )
    return pl.pallas_call(
        functools.partial(_decode_kernel, pages_per_step=pages_per_step,
                          steps_per_batch=steps_per_batch, n_chains=n_chains, scale=scale),
        grid_spec=grid_spec,
        out_shape=jax.ShapeDtypeStruct((nb, B_HEADS, KV_LORA), F32),
        compiler_params=_params(2),
        name="mla_decode_attn",
    )(page_table.reshape(-1), qlat, qr, lat_new, kr_new, lat_pool, krt_pool)


def _uv_kernel(olat_ref, wuv_ref, o_ref):
    for h in range(B_HEADS):
        o_ref[:, h * V_DIM:(h + 1) * V_DIM] = jnp.dot(
            olat_ref[h].astype(BF16), wuv_ref[h], preferred_element_type=F32).astype(BF16)


def _uv(olat_hm, wuv):
    nb = olat_hm.shape[1]
    return pl.pallas_call(
        _uv_kernel,
        in_specs=[_resident(), _resident()],
        out_specs=_resident(),
        out_shape=jax.ShapeDtypeStruct((nb, B_HEADS * V_DIM), BF16),
        name="mla_value_up",
    )(olat_hm, wuv)


QKV_TM = 1024
PROMPT_TM = 1024
MLP_TM = 1024
MLA_TQ = 256
DECODE_PAGES = 32
DECODE_CHAINS = 4
DECODE_SLOTS = 3


def kernel(x_prompt, x_sample, cache_a_k, cache_a_v, cache_b_latent, cache_b_krope, page_table,
           rel_bias, norm_mix, norm_mlp, norm_final, a_w_qkv, a_w_o, a_sinks,
           b_w_in, b_q_norm, b_kv_norm, b_w_q_b, b_w_kv_b, b_w_o, mlp_w_up, mlp_w_down):
    batch, seq, _ = x_prompt.shape
    nb_s = x_sample.shape[0]
    past = page_table.shape[1] * cache_b_latent.shape[2]
    wb = cache_a_k.shape[2]

    xp = x_prompt.reshape(batch * seq, D_MODEL)
    xs = x_sample.reshape(nb_s, D_MODEL)
    row = lambda t: t.reshape(1, -1)

    order = jnp.array(SWA_HEAD_ORDER)
    inverse = jnp.argsort(order)
    w_q = a_w_qkv[0][:, :NQ_A].reshape(D_MODEL, A_HEADS, A_HEAD_DIM)[:, order].reshape(D_MODEL, NQ_A)
    w_qkvt = jnp.concatenate([w_q, a_w_qkv[0][:, NQ_A:]], axis=1).T.astype(BF16)
    w_o_a = a_w_o[0].reshape(A_HEADS, A_HEAD_DIM, D_MODEL)[order].reshape(NQ_A, D_MODEL).astype(BF16)
    w_up, w_dn = mlp_w_up.astype(BF16), mlp_w_down.astype(BF16)
    bias = _bias_table(rel_bias)
    sinks = a_sinks[0]
    wbp = min(WINDOW, seq)

    qtp, kp, vtp, kt_tail, vt_tail = _norm_qkv(xp, row(norm_mix[0]), w_qkvt, QKV_TM, batch, wbp)
    op = _swa_prompt(qtp, kp, vtp, bias, sinks, batch, seq)
    xp = _out_mlp(op, xp, w_o_a, row(norm_mlp[0]), w_up, w_dn, 0, None, MLP_TM, False)

    qts, kts, vts = _norm_qkv(xs, row(norm_mix[0]), w_qkvt, nb_s)
    qs3 = qts.T.astype(F32).reshape(nb_s, A_HEADS, A_HEAD_DIM)[:, inverse]
    bias_row = jnp.transpose(
        bias[0].reshape(N_KV_SLABS, 2 * WINDOW, HEADS_PER_KV_SLAB, WINDOW)[:, WINDOW:, :, WINDOW - 1],
        (0, 2, 1)).reshape(A_HEADS, WINDOW)
    to_t = lambda c: jnp.transpose(c[0], (0, 2, 3, 1)).reshape(nb_s, NKV_A, wb)
    from_t = lambda t, n, w: jnp.transpose(t.reshape(n, A_KV_HEADS, A_HEAD_DIM, w), (0, 3, 1, 2))[None]
    os3, a_k_st, a_v_st = _swa_sample(qs3, kts, vts, to_t(cache_a_k), to_t(cache_a_v), bias_row, sinks)
    os_ = os3[:, order].reshape(nb_s, NQ_A).astype(BF16)
    xs = _out_mlp(os_, xs, w_o_a, row(norm_mlp[0]), w_up, w_dn, 0, None, nb_s, False)

    w_in = b_w_in[0]
    zeros = lambda n: jnp.zeros((D_MODEL, n), F32)
    w_in_ext = jnp.concatenate([w_in[:, :Q_LORA + KV_LORA], zeros(ROPE_LANE0),
                                w_in[:, Q_LORA + KV_LORA:], zeros(LANES - ROPE_LANE0 - ROPE_DIM)],
                               axis=1).astype(BF16)
    qk_dim = NOPE_DIM + ROPE_DIM
    w_qb = jnp.pad(b_w_q_b[0].reshape(Q_LORA, B_HEADS, qk_dim),
                   ((0, 0), (0, 0), (0, LANES - qk_dim))).reshape(Q_LORA, B_HEADS * LANES).astype(BF16)
    w_kvb = b_w_kv_b[0]
    w_uk, w_uv = w_kvb[..., :NOPE_DIM], w_kvb[..., NOPE_DIM:]
    w_k = jnp.pad(w_uk, ((0, 0), (0, 0), (0, LANES - NOPE_DIM))).reshape(KV_LORA, B_HEADS * LANES).astype(BF16)
    w_vt = w_uv.reshape(KV_LORA, B_HEADS * V_DIM).T.astype(BF16)
    w_o_b = b_w_o[0].astype(BF16)
    proj_w = (row(norm_mix[1]), w_in_ext, row(b_q_norm[0]), row(b_kv_norm[0]), w_qb, w_k, w_vt)

    q_scale_p = (NOPE_DIM + ROPE_DIM) ** -0.5 * LOG2E
    tabs_p = _rope_tables(np.arange(seq))
    qcat, kcat, vt, lat_p, krt_p = _mla_proj(xp, *proj_w, tabs_p, batch, seq, PROMPT_TM, seq // PROMPT_TM,
                                            q_scale_p)
    o_pm = _mla_attn(qcat, kcat, vt, MLA_TQ)
    yp = _out_mlp(o_pm, xp, w_o_b, row(norm_mlp[1]), w_up, w_dn, 1, row(norm_final), MLP_TM, True)

    tabs_s = _rope_tables(np.full((nb_s,), past))
    qcat_s, _, _, lat_s, krt_s = _mla_proj(xs, *proj_w, tabs_s, 1, nb_s, nb_s, 1, 1.0)
    w_uk_ext = jnp.pad(jnp.transpose(w_uk, (1, 2, 0)), ((0, 0), (0, LANES - NOPE_DIM), (0, 0))).astype(BF16)
    qlat = jnp.transpose(_absorb(qcat_s[0], w_uk_ext), (1, 0, 2))
    qr = jnp.transpose(qcat_s[0, :, :, ROPE_LANE0:ROPE_LANE0 + ROPE_DIM], (1, 0, 2))
    kr_s = krt_s[0].T
    krt_pool = jnp.swapaxes(cache_b_krope[0], 1, 2)
    olat = _mla_decode(page_table, qlat, qr, lat_s.reshape(nb_s, 1, KV_LORA), kr_s.reshape(nb_s, 1, ROPE_DIM),
                       cache_b_latent[0], krt_pool, DECODE_PAGES, DECODE_CHAINS)
    o_s = _uv(jnp.transpose(olat, (1, 0, 2)), jnp.transpose(w_uv, (1, 0, 2)).astype(BF16))
    ys = _out_mlp(o_s, xs, w_o_b, row(norm_mlp[1]), w_up, w_dn, 1, row(norm_final), nb_s, False)

    k4 = from_t(kt_tail, batch, wbp)
    v4 = from_t(vt_tail, batch, wbp)
    kr_p = jnp.swapaxes(krt_p, 1, 2)
    return (
        yp.reshape(batch, seq, D_MODEL),
        ys.reshape(nb_s, 1, D_MODEL),
        k4, v4,
        lat_p.reshape(1, batch, seq, KV_LORA),
        kr_p.reshape(1, batch, seq, ROPE_DIM),
        from_t(a_k_st, nb_s, wb),
        from_t(a_v_st, nb_s, wb),
        lat_s.reshape(1, nb_s, 1, KV_LORA),
        kr_s.reshape(1, nb_s, 1, ROPE_DIM),
    )
```

```python
import functools
import math

import jax
import jax.numpy as jnp
import numpy as np
from jax import lax
from jax.experimental import pallas as pl
from jax.experimental.pallas import tpu as pltpu

F32 = jnp.float32
BF16 = jnp.bfloat16

D_MODEL = 1024
A_HEADS = 16
A_KV_HEADS = 4
A_HEAD_DIM = 64
A_GROUP = A_HEADS // A_KV_HEADS
WINDOW = 128
N_BUCKETS = 32
MAX_EXACT = N_BUCKETS // 2
MAX_DISTANCE = 128
B_HEADS = 16
Q_LORA = 768
KV_LORA = 256
NOPE_DIM = 64
ROPE_DIM = 32
HALF_ROPE = ROPE_DIM // 2
V_DIM = 64
ROPE_THETA = 10000.0
D_FF = 4 * D_MODEL
EPS = 1e-6

LANES = 128
BF16_SUBLANES = 16
VMEM_LIMIT_BYTES = 56 * 1024 * 1024

NEG = -1e30
LOG2E = math.log2(math.e)

NT_DIMS = (((1,), (1,)), ((), ()))


def _rms(x, g):
    return x * lax.rsqrt(jnp.mean(x * x, axis=-1, keepdims=True) + EPS) * g


def _params(n_axes):
    return pltpu.CompilerParams(
        dimension_semantics=("arbitrary",) * n_axes,
        vmem_limit_bytes=VMEM_LIMIT_BYTES,
    )


def _resident():
    return pl.BlockSpec(memory_space=pltpu.VMEM)


NQ_A = A_HEADS * A_HEAD_DIM
NKV_A = A_KV_HEADS * A_HEAD_DIM


def _qkv_kernel(x_ref, g_ref, wt_ref, *out_refs, tail_rows, tiles_per_group):
    h = _rms(x_ref[...], g_ref[...]).astype(BF16)
    qkvt = lax.dot_general(wt_ref[...], h, NT_DIMS, preferred_element_type=F32)
    out_refs[0][...] = (qkvt[:NQ_A] * (A_HEAD_DIM ** -0.5 * LOG2E)).astype(BF16)
    kvt = qkvt[NQ_A:]
    if tail_rows:
        _, k_ref, vt_ref, ktail_ref, vtail_ref = out_refs
        k_ref[...] = kvt[:NKV_A].T
        vt_ref[...] = kvt[NKV_A:].astype(BF16)

        @pl.when(pl.program_id(0) % tiles_per_group == tiles_per_group - 1)
        def _():
            ktail_ref[...] = kvt[:NKV_A, kvt.shape[1] - tail_rows:]
            vtail_ref[...] = kvt[NKV_A:, kvt.shape[1] - tail_rows:]
    else:
        _, kt_ref, vt_ref = out_refs
        kt_ref[...] = kvt[:NKV_A]
        vt_ref[...] = kvt[NKV_A:]


def _norm_qkv(x, g, wt_bf16, tm, groups=1, tail_rows=0):
    m = x.shape[0]
    tiles_per_group = m // groups // tm
    col_blk = lambda i: (0, i)
    in_specs = [pl.BlockSpec((tm, D_MODEL), lambda i: (i, 0)), _resident(), _resident()]
    qt_spec = pl.BlockSpec((NQ_A, tm), col_blk)
    qt_shape = jax.ShapeDtypeStruct((NQ_A, m), BF16)
    if tail_rows:
        tail_spec = pl.BlockSpec((None, NKV_A, tail_rows), lambda i: (i // tiles_per_group, 0, 0))
        tail_shape = jax.ShapeDtypeStruct((groups, NKV_A, tail_rows), F32)
        out_specs = [qt_spec, pl.BlockSpec((tm, NKV_A), lambda i: (i, 0)),
                     pl.BlockSpec((NKV_A, tm), col_blk), tail_spec, tail_spec]
        out_shape = [qt_shape, jax.ShapeDtypeStruct((m, NKV_A), F32),
                     jax.ShapeDtypeStruct((NKV_A, m), BF16), tail_shape, tail_shape]
    else:
        out_specs = [qt_spec, pl.BlockSpec((NKV_A, tm), col_blk), pl.BlockSpec((NKV_A, tm), col_blk)]
        out_shape = [qt_shape, jax.ShapeDtypeStruct((NKV_A, m), F32), jax.ShapeDtypeStruct((NKV_A, m), F32)]
    return pl.pallas_call(
        functools.partial(_qkv_kernel, tail_rows=tail_rows, tiles_per_group=tiles_per_group),
        grid=(m // tm,),
        in_specs=in_specs,
        out_specs=out_specs,
        out_shape=out_shape,
        compiler_params=_params(1),
        name="norm_qkv",
    )(x, g, wt_bf16)


def _t5_bucket(dist):
    d = np.maximum(dist, 0)
    df = np.maximum(d, 1).astype(np.float32)
    large = MAX_EXACT + (np.log(df / np.float32(MAX_EXACT)) / np.float32(math.log(MAX_DISTANCE / MAX_EXACT))
                         * np.float32(N_BUCKETS - MAX_EXACT)).astype(np.int32)
    large = np.minimum(large, N_BUCKETS - 1)
    return np.where(d < MAX_EXACT, d, large)


HEADS_PER_SLAB = LANES // A_HEAD_DIM
N_KV_SLABS = NKV_A // LANES
HEADS_PER_KV_SLAB = HEADS_PER_SLAB * A_GROUP
TILE_COLS = HEADS_PER_KV_SLAB * WINDOW


def _bias_kernel(bucket_ref, rb_ref, out_ref):
    bucket = bucket_ref[...]
    has_prev = lax.broadcasted_iota(jnp.int32, bucket.shape, 0) >= WINDOW
    for h in range(A_HEADS):
        acc = jnp.full(bucket.shape, NEG, F32)
        for b in range(N_BUCKETS):
            acc = jnp.where(bucket == b, rb_ref[b, h], acc)
        acc = acc * LOG2E
        t, hh = divmod(h, HEADS_PER_KV_SLAB)
        out_ref[0, t, :, hh * WINDOW:(hh + 1) * WINDOW] = acc
        out_ref[1, t, :, hh * WINDOW:(hh + 1) * WINDOW] = jnp.where(has_prev, acc, NEG)


def _bias_table(rel_bias):
    kj = np.arange(2 * WINDOW, dtype=np.int32)[:, None]
    qi = np.arange(WINDOW, dtype=np.int32)[None, :]
    dist = qi + WINDOW - kj
    valid = (dist >= 0) & (dist < WINDOW)
    bucket = jnp.asarray(np.where(valid, _t5_bucket(dist), -1).astype(np.int32))
    return pl.pallas_call(
        _bias_kernel,
        in_specs=[_resident(), pl.BlockSpec(memory_space=pltpu.SMEM)],
        out_specs=_resident(),
        out_shape=jax.ShapeDtypeStruct((2, N_KV_SLABS, 2 * WINDOW, TILE_COLS), F32),
        name="t5_bias_table",
    )(bucket, rel_bias)


SWA_BLOCKS = 8
SWA_HEAD_ORDER = tuple(
    HEADS_PER_KV_SLAB * t + A_GROUP * half + i
    for t in range(N_KV_SLABS) for i in range(A_GROUP) for half in range(HEADS_PER_SLAB))


def _swa_prompt_kernel(qt_ref, kp_ref, kc_ref, vtp_ref, vtc_ref, bias0_ref, bias_ref, sink_ref, o_ref):
    k = jnp.concatenate([kp_ref[...], kc_ref[...]], axis=0).astype(BF16)
    vt = jnp.concatenate([vtp_ref[...], vtc_ref[...]], axis=1)
    top = lax.broadcasted_iota(jnp.int32, (LANES, WINDOW), 0) < A_HEAD_DIM
    zero = jnp.zeros((LANES, WINDOW), BF16)
    units = [(blk, t) for blk in range(SWA_BLOCKS) for t in range(N_KV_SLABS)]

    scores = []
    for blk, t in units:
        cols = slice(blk * WINDOW, (blk + 1) * WINDOW)
        qs = [qt_ref[(A_GROUP * t + i) * LANES:(A_GROUP * t + i + 1) * LANES, cols] for i in range(A_GROUP)]
        rhs = jnp.concatenate([jnp.where(top, x, zero) for x in qs] + [jnp.where(top, zero, x) for x in qs],
                              axis=1)
        kwin = k[blk * WINDOW:(blk + 2) * WINDOW, t * LANES:(t + 1) * LANES]
        scores.append(jnp.dot(kwin, rhs, preferred_element_type=F32))

    probs, sink_terms = [], []
    for (blk, t), s in zip(units, scores):
        s = s + (bias0_ref[t] if blk == 0 else bias_ref[t])
        sink = sink_ref[t] * LOG2E
        m = jnp.maximum(jnp.max(s, axis=0, keepdims=True), sink)
        sink_terms.append(jnp.exp2(sink - m))
        probs.append(jnp.exp2(s - m).astype(BF16))

    ones = jnp.ones((BF16_SUBLANES, 2 * WINDOW), BF16)
    outs = []
    for (blk, t), p in zip(units, probs):
        vwin = vt[t * LANES:(t + 1) * LANES, blk * WINDOW:(blk + 2) * WINDOW]
        outs.append(jnp.dot(jnp.concatenate([vwin, ones], axis=0), p, preferred_element_type=F32))

    for (blk, t), ot, sink_term in zip(units, outs, sink_terms):
        ot = ot[:LANES] * (1.0 / (ot[LANES:LANES + 1] + sink_term))
        for i in range(A_GROUP):
            first = ot[:, i * WINDOW:(i + 1) * WINDOW]
            second = ot[:, (A_GROUP + i) * WINDOW:(A_GROUP + i + 1) * WINDOW]
            o_ref[blk * WINDOW:(blk + 1) * WINDOW, (A_GROUP * t + i) * LANES:(A_GROUP * t + i + 1) * LANES] = (
                jnp.where(top, first, second).T.astype(BF16))


def _swa_prompt(qt, k, vt, bias, sinks, batch, seq):
    tq = SWA_BLOCKS * WINDOW
    steps = seq // tq
    blocks = seq // WINDOW
    sink_rows = jnp.repeat(sinks, WINDOW).reshape(N_KV_SLABS, 1, TILE_COLS)
    prev_blk = lambda b, j: b * blocks + jnp.maximum(SWA_BLOCKS * j - 1, 0)
    tile = (None, N_KV_SLABS, 2 * WINDOW, TILE_COLS)
    return pl.pallas_call(
        _swa_prompt_kernel,
        grid=(batch, steps),
        in_specs=[
            pl.BlockSpec((NQ_A, tq), lambda b, j: (0, b * steps + j)),
            pl.BlockSpec((WINDOW, NKV_A), lambda b, j: (prev_blk(b, j), 0)),
            pl.BlockSpec((tq, NKV_A), lambda b, j: (b * steps + j, 0)),
            pl.BlockSpec((NKV_A, WINDOW), lambda b, j: (0, prev_blk(b, j))),
            pl.BlockSpec((NKV_A, tq), lambda b, j: (0, b * steps + j)),
            pl.BlockSpec(tile, lambda b, j: (jnp.where(j == 0, 1, 0), 0, 0, 0)),
            _resident(),
            _resident(),
        ],
        out_specs=pl.BlockSpec((tq, NQ_A), lambda b, j: (b * steps + j, 0)),
        out_shape=jax.ShapeDtypeStruct((batch * seq, NQ_A), BF16),
        compiler_params=_params(2),
        name="swa_prompt_attn",
    )(qt, k, k, vt, vt, bias, bias[0], sink_rows)


def _swa_sample_kernel(q_ref, kbt_ref, vbt_ref, knt_ref, vnt_ref, bias_ref, sink_ref,
                       o_ref, kout_ref, vout_ref):
    sink_all = sink_ref[...] * LOG2E
    n_rows, _, wb = kbt_ref.shape
    newest = lax.broadcasted_iota(jnp.int32, (NKV_A, wb), 1) == wb - 1
    kb, vb = [], []
    for r in range(n_rows):
        kw = jnp.where(newest, knt_ref[:, r:r + 1], pltpu.roll(kbt_ref[r], wb - 1, 1))
        vw = jnp.where(newest, vnt_ref[:, r:r + 1], pltpu.roll(vbt_ref[r], wb - 1, 1))
        kout_ref[r] = kw
        vout_ref[r] = vw
        kb.append(kw.astype(BF16))
        vb.append(vw.astype(BF16))
    units = [(r, g) for r in range(n_rows) for g in range(A_KV_HEADS)]
    heads = lambda g: slice(g * A_GROUP, (g + 1) * A_GROUP)
    dims = lambda g: slice(g * A_HEAD_DIM, (g + 1) * A_HEAD_DIM)
    scores = [jnp.dot(q_ref[r, heads(g), :].astype(BF16), kb[r][dims(g), :], preferred_element_type=F32)
              for r, g in units]
    probs, inv = [], []
    for (r, g), s in zip(units, scores):
        s = s + bias_ref[heads(g), :]
        sink = sink_all[heads(g), :]
        m = jnp.maximum(jnp.max(s, axis=-1, keepdims=True), sink)
        e = jnp.exp2(s - m)
        inv.append(1.0 / (jnp.sum(e, axis=-1, keepdims=True) + jnp.exp2(sink - m)))
        probs.append(e.astype(BF16))
    outs = [lax.dot_general(p, vb[r][dims(g), :], NT_DIMS, preferred_element_type=F32)
            for (r, g), p in zip(units, probs)]
    for (r, g), pv, rcp in zip(units, outs, inv):
        o_ref[r, heads(g), :] = pv * rcp


def _swa_sample(q, kt_new, vt_new, kbuf_t, vbuf_t, bias_row, sinks):
    nb = kbuf_t.shape[0]
    return pl.pallas_call(
        _swa_sample_kernel,
        in_specs=[_resident()] * 7,
        out_specs=[_resident()] * 3,
        out_shape=[
            jax.ShapeDtypeStruct((nb, A_HEADS, A_HEAD_DIM), F32),
            jax.ShapeDtypeStruct(kbuf_t.shape, F32),
            jax.ShapeDtypeStruct(vbuf_t.shape, F32),
        ],
        compiler_params=pltpu.CompilerParams(vmem_limit_bytes=VMEM_LIMIT_BYTES),
        name="swa_sample_attn",
    )(q, kbuf_t, vbuf_t, kt_new, vt_new, bias_row, sinks.reshape(A_HEADS, 1))


FF_CHUNK = 1024


def _out_mlp_kernel(*refs, pair_major, final, layer):
    if final:
        o_ref, x_ref, wo_ref, g_ref, wup_hbm, wdn_hbm, gf_ref, out_ref, wup_ref, wdn_ref, sem = refs
    else:
        o_ref, x_ref, wo_ref, g_ref, wup_hbm, wdn_hbm, out_ref, wup_ref, wdn_ref, sem = refs

    @pl.when(pl.program_id(0) == 0)
    def _():
        copies = [pltpu.make_async_copy(wup_hbm.at[layer], wup_ref, sem.at[0]),
                  pltpu.make_async_copy(wdn_hbm.at[layer], wdn_ref, sem.at[1])]
        for cp in copies:
            cp.start()
        for cp in copies:
            cp.wait()

    if pair_major:
        o = jnp.concatenate([o_ref[j] for j in range(o_ref.shape[0])], axis=-1)
    else:
        o = o_ref[...]
    x1 = x_ref[...] + jnp.dot(o, wo_ref[...], preferred_element_type=F32)
    h = _rms(x1, g_ref[...]).astype(BF16)
    acc = x1
    for c in range(D_FF // FF_CHUNK):
        sl = slice(c * FF_CHUNK, (c + 1) * FF_CHUNK)
        u = jnp.dot(h, wup_ref[:, sl], preferred_element_type=F32)
        u = jnp.square(jnp.maximum(u, 0.0)).astype(BF16)
        acc = acc + jnp.dot(u, wdn_ref[sl, :], preferred_element_type=F32)
    if final:
        acc = _rms(acc, gf_ref[...])
    out_ref[...] = acc


def _out_mlp(o, x, wo, g, wup_all, wdn_all, layer, gf, tm, pair_major):
    m = x.shape[0]
    final = gf is not None
    if pair_major:
        tiles_per_group = o.shape[2] // tm
        o_spec = pl.BlockSpec((None, o.shape[1], tm, LANES),
                              lambda i: (i // tiles_per_group, 0, i % tiles_per_group, 0))
    else:
        o_spec = pl.BlockSpec((tm, D_MODEL), lambda i: (i, 0))
    row_spec = pl.BlockSpec((tm, D_MODEL), lambda i: (i, 0))
    in_hbm = pl.BlockSpec(memory_space=pl.ANY)
    in_specs = [o_spec, row_spec, _resident(), _resident(), in_hbm, in_hbm]
    args = [o, x, wo, g, wup_all, wdn_all]
    if final:
        in_specs.append(_resident())
        args.append(gf)
    return pl.pallas_call(
        functools.partial(_out_mlp_kernel, pair_major=pair_major, final=final, layer=layer),
        grid=(m // tm,),
        in_specs=in_specs,
        out_specs=row_spec,
        out_shape=jax.ShapeDtypeStruct((m, D_MODEL), F32),
        scratch_shapes=[
            pltpu.VMEM((D_MODEL, D_FF), BF16),
            pltpu.VMEM((D_FF, D_MODEL), BF16),
            pltpu.SemaphoreType.DMA((2,)),
        ],
        compiler_params=_params(1),
        name="out_proj_mlp",
    )(*args)


N_PAIRS = B_HEADS // 2
VT_ROWS = V_DIM + BF16_SUBLANES
ROPE_LANE0 = NOPE_DIM
C_IN_EXT = Q_LORA + KV_LORA + LANES


def _rope_slab(t, c, s1, s2):
    return t * c + pltpu.roll(t, HALF_ROPE, 1) * s1 + pltpu.roll(t, LANES - HALF_ROPE, 1) * s2


def _mla_proj_kernel(x_ref, g_ref, win_ref, qn_ref, kvn_ref, wqb_ref, wk_ref, wvt_ref,
                     c_ref, s1_ref, s2_ref,
                     qcat_ref, kcat_ref, vt_ref, lat_ref, krt_ref, *, q_scale):
    h = _rms(x_ref[...], g_ref[...]).astype(BF16)
    c = jnp.dot(h, win_ref[...], preferred_element_type=F32)
    cq = _rms(c[:, :Q_LORA], qn_ref[...]).astype(BF16)
    lat = _rms(c[:, Q_LORA:Q_LORA + KV_LORA], kvn_ref[...])
    lat_ref[...] = lat
    cos, s1, s2 = c_ref[...], s1_ref[...], s2_ref[...]
    krp = _rope_slab(c[:, Q_LORA + KV_LORA:], cos, s1, s2)
    krt_ref[...] = krp.T[ROPE_LANE0:ROPE_LANE0 + ROPE_DIM, :]
    q = jnp.dot(cq, wqb_ref[...], preferred_element_type=F32)
    latb = lat.astype(BF16)
    kn = jnp.dot(latb, wk_ref[...], preferred_element_type=F32)
    vt = lax.dot_general(wvt_ref[...], latb, NT_DIMS, preferred_element_type=F32)
    for hh in range(B_HEADS):
        sl = slice(hh * LANES, (hh + 1) * LANES)
        qh = _rope_slab(q[:, sl], cos, s1, s2)
        if q_scale != 1.0:
            qh = qh * q_scale
        qcat_ref[hh] = qh.astype(BF16)
        kcat_ref[hh] = (kn[:, sl] + krp).astype(BF16)
        vt_ref[hh, :V_DIM, :] = vt[hh * V_DIM:(hh + 1) * V_DIM, :].astype(BF16)
        vt_ref[hh, V_DIM:, :] = jnp.ones((VT_ROWS - V_DIM, vt.shape[1]), BF16)


def _mla_proj(x, g, win, qn, kvn, wqb, wk, wvt, tabs, groups, rows, tm, tab_blocks, q_scale):
    m = x.shape[0]
    tiles = rows // tm
    head_map = lambda i: (i // tiles, 0, i % tiles, 0)
    tab_spec = pl.BlockSpec((tm, LANES), lambda i: (i % tab_blocks, 0))
    return pl.pallas_call(
        functools.partial(_mla_proj_kernel, q_scale=q_scale),
        grid=(m // tm,),
        in_specs=[pl.BlockSpec((tm, D_MODEL), lambda i: (i, 0))] + [_resident()] * 7 + [tab_spec] * 3,
        out_specs=[
            pl.BlockSpec((None, B_HEADS, tm, LANES), head_map),
            pl.BlockSpec((None, B_HEADS, tm, LANES), head_map),
            pl.BlockSpec((None, B_HEADS, VT_ROWS, tm), lambda i: (i // tiles, 0, 0, i % tiles)),
            pl.BlockSpec((tm, KV_LORA), lambda i: (i, 0)),
            pl.BlockSpec((None, ROPE_DIM, tm), lambda i: (i // tiles, 0, i % tiles)),
        ],
        out_shape=[
            jax.ShapeDtypeStruct((groups, B_HEADS, rows, LANES), BF16),
            jax.ShapeDtypeStruct((groups, B_HEADS, rows, LANES), BF16),
            jax.ShapeDtypeStruct((groups, B_HEADS, VT_ROWS, rows), BF16),
            jax.ShapeDtypeStruct((m, KV_LORA), F32),
            jax.ShapeDtypeStruct((groups, ROPE_DIM, rows), F32),
        ],
        compiler_params=_params(1),
        name="mla_proj",
    )(x, g, win, qn, kvn, wqb, wk, wvt, *tabs)


def _rope_tables(pos):
    pos = np.asarray(pos, np.float64)
    inv = ROPE_THETA ** (-np.arange(0, ROPE_DIM, 2, dtype=np.float64) / ROPE_DIM)
    ang = pos[:, None] * inv[None, :]
    cos, sin = np.cos(ang).astype(np.float32), np.sin(ang).astype(np.float32)
    s = pos.shape[0]
    ones = np.ones((s, ROPE_LANE0), np.float32)
    z16 = np.zeros((s, HALF_ROPE), np.float32)
    z64 = np.zeros((s, ROPE_LANE0), np.float32)
    z32 = np.zeros((s, LANES - ROPE_LANE0 - ROPE_DIM), np.float32)
    c = np.concatenate([ones, cos, cos, z32], axis=1)
    s1 = np.concatenate([z64, z16, sin, z32], axis=1)
    s2 = np.concatenate([z64, -sin, z16, z32], axis=1)
    return jnp.asarray(c), jnp.asarray(s1), jnp.asarray(s2)


HEADS_PER_BODY = 8
REDUCE_ROWS = 64


def _col_reduce(x, op):
    n, w = x.shape
    part = op(x.reshape(n // REDUCE_ROWS, REDUCE_ROWS, w), axis=0)
    return op(part, axis=0, keepdims=True)


def _mla_attn_kernel(q_ref, k_ref, vt_ref, o_ref, *, tq, n_tiles):
    qi = pl.program_id(1)
    key = lax.broadcasted_iota(jnp.int32, (tq, tq), 0)
    qry = lax.broadcasted_iota(jnp.int32, (tq, tq), 1)
    causal = key <= qry

    def scores(h, n_keys):
        k = k_ref[h, pl.ds(0, n_keys), :]
        return lax.dot_general(k, q_ref[h], NT_DIMS, preferred_element_type=F32)

    def attend(st, h, n_keys):
        diag = jnp.where(causal, st[n_keys - tq:, :], NEG)
        if n_keys > tq:
            full = st[:n_keys - tq, :]
            m = jnp.maximum(_col_reduce(full, jnp.max), _col_reduce(diag, jnp.max))
            p = jnp.concatenate([jnp.exp2(full - m), jnp.exp2(diag - m)], axis=0)
        else:
            m = _col_reduce(diag, jnp.max)
            p = jnp.exp2(diag - m)
        ot = jnp.dot(vt_ref[h, :, pl.ds(0, n_keys)], p.astype(BF16), preferred_element_type=F32)
        return ot[:V_DIM] / ot[V_DIM:V_DIM + 1]

    for c in range(n_tiles):
        @pl.when(qi == c)
        def _(c=c):
            n_keys = (c + 1) * tq

            def body(jj, carry):
                heads = [HEADS_PER_BODY * jj + u for u in range(HEADS_PER_BODY)]
                sts = [scores(h, n_keys) for h in heads]
                outs = [attend(st, h, n_keys) for h, st in zip(heads, sts)]
                for u in range(0, HEADS_PER_BODY, 2):
                    ot = jnp.concatenate([outs[u], outs[u + 1]], axis=0)
                    o_ref[(HEADS_PER_BODY // 2) * jj + u // 2] = ot.T.astype(BF16)
                return carry

            lax.fori_loop(0, B_HEADS // HEADS_PER_BODY, body, 0)


def _mla_attn(qcat, kcat, vt, tq):
    batch, _, seq, _ = qcat.shape
    n_tiles = seq // tq
    return pl.pallas_call(
        functools.partial(_mla_attn_kernel, tq=tq, n_tiles=n_tiles),
        grid=(batch, n_tiles),
        in_specs=[
            pl.BlockSpec((None, B_HEADS, tq, LANES), lambda b, i: (b, 0, i, 0)),
            pl.BlockSpec((None, B_HEADS, seq, LANES), lambda b, i: (b, 0, 0, 0)),
            pl.BlockSpec((None, B_HEADS, VT_ROWS, seq), lambda b, i: (b, 0, 0, 0)),
        ],
        out_specs=pl.BlockSpec((None, N_PAIRS, tq, LANES), lambda b, i: (b, 0, i, 0)),
        out_shape=jax.ShapeDtypeStruct((batch, N_PAIRS, seq, LANES), BF16),
        compiler_params=_params(2),
        name="mla_prompt_attn",
    )(qcat, kcat, vt)


def _absorb_kernel(qcat_ref, wuk_ref, qlat_ref):
    for h in range(B_HEADS):
        qlat_ref[h] = jnp.dot(qcat_ref[h], wuk_ref[h], preferred_element_type=F32).astype(BF16)


def _absorb(qcat, wuk_ext):
    nb = qcat.shape[1]
    return pl.pallas_call(
        _absorb_kernel,
        in_specs=[_resident(), _resident()],
        out_specs=_resident(),
        out_shape=jax.ShapeDtypeStruct((B_HEADS, nb, KV_LORA), BF16),
        name="mla_absorb_q",
    )(qcat, wuk_ext)


def _decode_kernel(pt_ref, qlat_ref, qr_ref, latnew_ref, krnew_ref, lat_hbm, krt_hbm, o_ref,
                   latbuf, krbuf, sem, m_sc, l_sc, acc_sc,
                   *, pages_per_step, steps_per_batch, n_chains, scale):
    b = pl.program_id(0)
    c = pl.program_id(1)
    step = b * steps_per_batch + c
    n_steps = pl.num_programs(0) * steps_per_batch
    last_step = step + 1 == n_steps
    n_slots = latbuf.shape[0]
    slot = step % n_slots
    page_size = latbuf.shape[2]

    def start_chunk(chunk, sl):
        for p in range(pages_per_step):
            page = pt_ref[chunk * pages_per_step + p]
            pltpu.make_async_copy(lat_hbm.at[page], latbuf.at[sl, p], sem.at[0, sl]).start()
            pltpu.make_async_copy(krt_hbm.at[page], krbuf.at[sl, p], sem.at[1, sl]).start()

    def wait_chunk(sl):
        pltpu.make_async_copy(lat_hbm.at[pl.ds(0, pages_per_step)], latbuf.at[sl], sem.at[0, sl]).wait()
        pltpu.make_async_copy(krt_hbm.at[pl.ds(0, pages_per_step)], krbuf.at[sl], sem.at[1, sl]).wait()

    @pl.when(step == 0)
    def _():
        for k in range(n_slots - 1):
            start_chunk(k, k)

    ahead = n_slots - 1
    start_chunk(jnp.minimum(step + ahead, n_steps - 1), (step + ahead) % n_slots)

    qlat = qlat_ref[...]
    qr = qr_ref[...]

    @pl.when(c == 0)
    def _():
        ln = latnew_ref[...].astype(BF16).astype(F32)
        kn = krnew_ref[...].astype(BF16).astype(F32)
        s_new = (jnp.sum(qlat.astype(F32) * ln, axis=-1, keepdims=True)
                 + jnp.sum(qr.astype(F32) * kn, axis=-1, keepdims=True)) * scale
        m_sc[...] = jnp.full(m_sc.shape, NEG, F32)
        l_sc[...] = jnp.zeros(l_sc.shape, F32)
        acc_sc[...] = jnp.zeros(acc_sc.shape, F32)
        m_sc[0] = s_new
        l_sc[0] = jnp.ones_like(s_new)
        acc_sc[0] = jnp.broadcast_to(ln, acc_sc.shape[1:])

    wait_chunk(slot)

    ppc = pages_per_step // n_chains
    chains = range(n_chains)
    s_rope = [jnp.concatenate(
        [jnp.dot(qr, krbuf[slot, i * ppc + p].astype(BF16), preferred_element_type=F32)
         for p in range(ppc)], axis=1) for i in chains]
    lat = [latbuf[slot, pl.ds(i * ppc, ppc)].reshape(ppc * page_size, KV_LORA).astype(BF16) for i in chains]
    s = [(lax.dot_general(qlat, lat[i], NT_DIMS, preferred_element_type=F32) + s_rope[i]) * scale
         for i in chains]
    m_old = [m_sc[i] for i in chains]
    m_new = [jnp.maximum(m_old[i], jnp.max(s[i], axis=-1, keepdims=True)) for i in chains]
    alpha = [jnp.exp(m_old[i] - m_new[i]) for i in chains]
    p = [jnp.exp(s[i] - m_new[i]) for i in chains]
    pv = [jnp.dot(p[i].astype(BF16), lat[i], preferred_element_type=F32) for i in chains]
    for i in chains:
        l_sc[i] = alpha[i] * l_sc[i] + jnp.sum(p[i], axis=-1, keepdims=True)
        acc_sc[i] = alpha[i] * acc_sc[i] + pv[i]
        m_sc[i] = m_new[i]

    @pl.when(c == steps_per_batch - 1)
    def _():
        m = m_sc[0]
        for i in range(1, n_chains):
            m = jnp.maximum(m, m_sc[i])
        l = jnp.zeros_like(m)
        acc = jnp.zeros(acc_sc.shape[1:], F32)
        for i in range(n_chains):
            w = jnp.exp(m_sc[i] - m)
            l = l + w * l_sc[i]
            acc = acc + w * acc_sc[i]
        o_ref[...] = acc / l

    @pl.when(last_step)
    def _():
        for k in range(1, n_slots):
            wait_chunk((step + k) % n_slots)


def _mla_decode(page_table, qlat, qr, lat_new, kr_new, lat_pool, krt_pool, pages_per_step, n_chains):
    nb, n_pages = page_table.shape
    page_size = lat_pool.shape[1]
    steps_per_batch = n_pages // pages_per_step
    scale = (NOPE_DIM + ROPE_DIM) ** -0.5
    per_b = lambda b, c, pt: (b, 0, 0)
    grid_spec = pltpu.PrefetchScalarGridSpec(
        num_scalar_prefetch=1,
        grid=(nb, steps_per_batch),
        in_specs=[
            pl.BlockSpec((None, B_HEADS, KV_LORA), per_b),
            pl.BlockSpec((None, B_HEADS, ROPE_DIM), per_b),
            pl.BlockSpec((None, 1, KV_LORA), per_b),
            pl.BlockSpec((None, 1, ROPE_DIM), per_b),
            pl.BlockSpec(memory_space=pl.ANY),
            pl.BlockSpec(memory_space=pl.ANY),
        ],
        out_specs=pl.BlockSpec((None, B_HEADS, KV_LORA), per_b),
        scratch_shapes=[
            pltpu.VMEM((DECODE_SLOTS, pages_per_step, page_size, KV_LORA), F32),
            pltpu.VMEM((DECODE_SLOTS, pages_per_step, ROPE_DIM, page_size), F32),
            pltpu.SemaphoreType.DMA((2, DECODE_SLOTS)),
            pltpu.VMEM((n_chains, B_HEADS, 1), F32),
            pltpu.VMEM((n_chains, B_HEADS, 1), F32),
            pltpu.VMEM((n_chains, B_HEADS, KV_LORA), F32),
        ],
    )
    return pl.pallas_call(
        functools.partial(_decode_kernel, pages_per_step=pages_per_step,
                          steps_per_batch=steps_per_batch, n_chains=n_chains, scale=scale),
        grid_spec=grid_spec,
        out_shape=jax.ShapeDtypeStruct((nb, B_HEADS, KV_LORA), F32),
        compiler_params=_params(2),
        name="mla_decode_attn",
    )(page_table.reshape(-1), qlat, qr, lat_new, kr_new, lat_pool, krt_pool)


def _uv_kernel(olat_ref, wuv_ref, o_ref):
    for h in range(B_HEADS):
        o_ref[:, h * V_DIM:(h + 1) * V_DIM] = jnp.dot(
            olat_ref[h].astype(BF16), wuv_ref[h], preferred_element_type=F32).astype(BF16)


def _uv(olat_hm, wuv):
    nb = olat_hm.shape[1]
    return pl.pallas_call(
        _uv_kernel,
        in_specs=[_resident(), _resident()],
        out_specs=_resident(),
        out_shape=jax.ShapeDtypeStruct((nb, B_HEADS * V_DIM), BF16),
        name="mla_value_up",
    )(olat_hm, wuv)


QKV_TM = 1024
PROMPT_TM = 1024
MLP_TM = 1024
MLA_TQ = 256
DECODE_PAGES = 32
DECODE_CHAINS = 4
DECODE_SLOTS = 3


def kernel(x_prompt, x_sample, cache_a_k, cache_a_v, cache_b_latent, cache_b_krope, page_table,
           rel_bias, norm_mix, norm_mlp, norm_final, a_w_qkv, a_w_o, a_sinks,
           b_w_in, b_q_norm, b_kv_norm, b_w_q_b, b_w_kv_b, b_w_o, mlp_w_up, mlp_w_down):
    batch, seq, _ = x_prompt.shape
    nb_s = x_sample.shape[0]
    past = page_table.shape[1] * cache_b_latent.shape[2]
    wb = cache_a_k.shape[2]

    xp = x_prompt.reshape(batch * seq, D_MODEL)
    xs = x_sample.reshape(nb_s, D_MODEL)
    row = lambda t: t.reshape(1, -1)

    order = jnp.array(SWA_HEAD_ORDER)
    inverse = jnp.argsort(order)
    w_q = a_w_qkv[0][:, :NQ_A].reshape(D_MODEL, A_HEADS, A_HEAD_DIM)[:, order].reshape(D_MODEL, NQ_A)
    w_qkvt = jnp.concatenate([w_q, a_w_qkv[0][:, NQ_A:]], axis=1).T.astype(BF16)
    w_o_a = a_w_o[0].reshape(A_HEADS, A_HEAD_DIM, D_MODEL)[order].reshape(NQ_A, D_MODEL).astype(BF16)
    w_up, w_dn = mlp_w_up.astype(BF16), mlp_w_down.astype(BF16)
    bias = _bias_table(rel_bias)
    sinks = a_sinks[0]
    wbp = min(WINDOW, seq)

    qtp, kp, vtp, kt_tail, vt_tail = _norm_qkv(xp, row(norm_mix[0]), w_qkvt, QKV_TM, batch, wbp)
    op = _swa_prompt(qtp, kp, vtp, bias, sinks, batch, seq)
    xp = _out_mlp(op, xp, w_o_a, row(norm_mlp[0]), w_up, w_dn, 0, None, MLP_TM, False)

    qts, kts, vts = _norm_qkv(xs, row(norm_mix[0]), w_qkvt, nb_s)
    qs3 = qts.T.astype(F32).reshape(nb_s, A_HEADS, A_HEAD_DIM)[:, inverse]
    bias_row = jnp.transpose(
        bias[0].reshape(N_KV_SLABS, 2 * WINDOW, HEADS_PER_KV_SLAB, WINDOW)[:, WINDOW:, :, WINDOW - 1],
        (0, 2, 1)).reshape(A_HEADS, WINDOW)
    to_t = lambda c: jnp.transpose(c[0], (0, 2, 3, 1)).reshape(nb_s, NKV_A, wb)
    from_t = lambda t, n, w: jnp.transpose(t.reshape(n, A_KV_HEADS, A_HEAD_DIM, w), (0, 3, 1, 2))[None]
    os3, a_k_st, a_v_st = _swa_sample(qs3, kts, vts, to_t(cache_a_k), to_t(cache_a_v), bias_row, sinks)
    os_ = os3[:, order].reshape(nb_s, NQ_A).astype(BF16)
    xs = _out_mlp(os_, xs, w_o_a, row(norm_mlp[0]), w_up, w_dn, 0, None, nb_s, False)

    w_in = b_w_in[0]
    zeros = lambda n: jnp.zeros((D_MODEL, n), F32)
    w_in_ext = jnp.concatenate([w_in[:, :Q_LORA + KV_LORA], zeros(ROPE_LANE0),
                                w_in[:, Q_LORA + KV_LORA:], zeros(LANES - ROPE_LANE0 - ROPE_DIM)],
                               axis=1).astype(BF16)
    qk_dim = NOPE_DIM + ROPE_DIM
    w_qb = jnp.pad(b_w_q_b[0].reshape(Q_LORA, B_HEADS, qk_dim),
                   ((0, 0), (0, 0), (0, LANES - qk_dim))).reshape(Q_LORA, B_HEADS * LANES).astype(BF16)
    w_kvb = b_w_kv_b[0]
    w_uk, w_uv = w_kvb[..., :NOPE_DIM], w_kvb[..., NOPE_DIM:]
    w_k = jnp.pad(w_uk, ((0, 0), (0, 0), (0, LANES - NOPE_DIM))).reshape(KV_LORA, B_HEADS * LANES).astype(BF16)
    w_vt = w_uv.reshape(KV_LORA, B_HEADS * V_DIM).T.astype(BF16)
    w_o_b = b_w_o[0].astype(BF16)
    proj_w = (row(norm_mix[1]), w_in_ext, row(b_q_norm[0]), row(b_kv_norm[0]), w_qb, w_k, w_vt)

    q_scale_p = (NOPE_DIM + ROPE_DIM) ** -0.5 * LOG2E
    tabs_p = _rope_tables(np.arange(seq))
    qcat, kcat, vt, lat_p, krt_p = _mla_proj(xp, *proj_w, tabs_p, batch, seq, PROMPT_TM, seq // PROMPT_TM,
                                            q_scale_p)
    o_pm = _mla_attn(qcat, kcat, vt, MLA_TQ)
    yp = _out_mlp(o_pm, xp, w_o_b, row(norm_mlp[1]), w_up, w_dn, 1, row(norm_final), MLP_TM, True)

    tabs_s = _rope_tables(np.full((nb_s,), past))
    qcat_s, _, _, lat_s, krt_s = _mla_proj(xs, *proj_w, tabs_s, 1, nb_s, nb_s, 1, 1.0)
    w_uk_ext = jnp.pad(jnp.transpose(w_uk, (1, 2, 0)), ((0, 0), (0, LANES - NOPE_DIM), (0, 0))).astype(BF16)
    qlat = jnp.transpose(_absorb(qcat_s[0], w_uk_ext), (1, 0, 2))
    qr = jnp.transpose(qcat_s[0, :, :, ROPE_LANE0:ROPE_LANE0 + ROPE_DIM], (1, 0, 2))
    kr_s = krt_s[0].T
    krt_pool = jnp.swapaxes(cache_b_krope[0], 1, 2)
    olat = _mla_decode(page_table, qlat, qr, lat_s.reshape(nb_s, 1, KV_LORA), kr_s.reshape(nb_s, 1, ROPE_DIM),
                       cache_b_latent[0], krt_pool, DECODE_PAGES, DECODE_CHAINS)
    o_s = _uv(jnp.transpose(olat, (1, 0, 2)), jnp.transpose(w_uv, (1, 0, 2)).astype(BF16))
    ys = _out_mlp(o_s, xs, w_o_b, row(norm_mlp[1]), w_up, w_dn, 1, row(norm_final), nb_s, False)

    k4 = from_t(kt_tail, batch, wbp)
    v4 = from_t(vt_tail, batch, wbp)
    kr_p = jnp.swapaxes(krt_p, 1, 2)
    return (
        yp.reshape(batch, seq, D_MODEL),
        ys.reshape(nb_s, 1, D_MODEL),
        k4, v4,
        lat_p.reshape(1, batch, seq, KV_LORA),
        kr_p.reshape(1, batch, seq, ROPE_DIM),
        from_t(a_k_st, nb_s, wb),
        from_t(a_v_st, nb_s, wb),
        lat_s.reshape(1, nb_s, 1, KV_LORA),
        kr_s.reshape(1, nb_s, 1, ROPE_DIM),
    )
```

```python
import functools
import math

import jax
import jax.numpy as jnp
import numpy as np
from jax import lax
from jax.experimental import pallas as pl
from jax.experimental.pallas import tpu as pltpu

F32 = jnp.float32
BF16 = jnp.bfloat16

D_MODEL = 1024
A_HEADS = 16
A_KV_HEADS = 4
A_HEAD_DIM = 64
A_GROUP = A_HEADS // A_KV_HEADS
WINDOW = 128
N_BUCKETS = 32
MAX_EXACT = N_BUCKETS // 2
MAX_DISTANCE = 128
B_HEADS = 16
Q_LORA = 768
KV_LORA = 256
NOPE_DIM = 64
ROPE_DIM = 32
HALF_ROPE = ROPE_DIM // 2
V_DIM = 64
ROPE_THETA = 10000.0
D_FF = 4 * D_MODEL
EPS = 1e-6

LANES = 128
BF16_SUBLANES = 16
VMEM_LIMIT_BYTES = 56 * 1024 * 1024

NEG = -1e30
LOG2E = math.log2(math.e)

NT_DIMS = (((1,), (1,)), ((), ()))


def _rms(x, g):
    return x * lax.rsqrt(jnp.mean(x * x, axis=-1, keepdims=True) + EPS) * g


def _params(n_axes):
    return pltpu.CompilerParams(
        dimension_semantics=("arbitrary",) * n_axes,
        vmem_limit_bytes=VMEM_LIMIT_BYTES,
    )


def _resident():
    return pl.BlockSpec(memory_space=pltpu.VMEM)


NQ_A = A_HEADS * A_HEAD_DIM
NKV_A = A_KV_HEADS * A_HEAD_DIM


def _qkv_kernel(x_ref, g_ref, wt_ref, *out_refs, tail_rows, tiles_per_group):
    h = _rms(x_ref[...], g_ref[...]).astype(BF16)
    qkvt = lax.dot_general(wt_ref[...], h, NT_DIMS, preferred_element_type=F32)
    out_refs[0][...] = (qkvt[:NQ_A] * (A_HEAD_DIM ** -0.5 * LOG2E)).astype(BF16)
    kvt = qkvt[NQ_A:]
    if tail_rows:
        _, k_ref, vt_ref, ktail_ref, vtail_ref = out_refs
        k_ref[...] = kvt[:NKV_A].T
        vt_ref[...] = kvt[NKV_A:].astype(BF16)

        @pl.when(pl.program_id(0) % tiles_per_group == tiles_per_group - 1)
        def _():
            ktail_ref[...] = kvt[:NKV_A, kvt.shape[1] - tail_rows:]
            vtail_ref[...] = kvt[NKV_A:, kvt.shape[1] - tail_rows:]
    else:
        _, kt_ref, vt_ref = out_refs
        kt_ref[...] = kvt[:NKV_A]
        vt_ref[...] = kvt[NKV_A:]


def _norm_qkv(x, g, wt_bf16, tm, groups=1, tail_rows=0):
    m = x.shape[0]
    tiles_per_group = m // groups // tm
    col_blk = lambda i: (0, i)
    in_specs = [pl.BlockSpec((tm, D_MODEL), lambda i: (i, 0)), _resident(), _resident()]
    qt_spec = pl.BlockSpec((NQ_A, tm), col_blk)
    qt_shape = jax.ShapeDtypeStruct((NQ_A, m), BF16)
    if tail_rows:
        tail_spec = pl.BlockSpec((None, NKV_A, tail_rows), lambda i: (i // tiles_per_group, 0, 0))
        tail_shape = jax.ShapeDtypeStruct((groups, NKV_A, tail_rows), F32)
        out_specs = [qt_spec, pl.BlockSpec((tm, NKV_A), lambda i: (i, 0)),
                     pl.BlockSpec((NKV_A, tm), col_blk), tail_spec, tail_spec]
        out_shape = [qt_shape, jax.ShapeDtypeStruct((m, NKV_A), F32),
                     jax.ShapeDtypeStruct((NKV_A, m), BF16), tail_shape, tail_shape]
    else:
        out_specs = [qt_spec, pl.BlockSpec((NKV_A, tm), col_blk), pl.BlockSpec((NKV_A, tm), col_blk)]
        out_shape = [qt_shape, jax.ShapeDtypeStruct((NKV_A, m), F32), jax.ShapeDtypeStruct((NKV_A, m), F32)]
    return pl.pallas_call(
        functools.partial(_qkv_kernel, tail_rows=tail_rows, tiles_per_group=tiles_per_group),
        grid=(m // tm,),
        in_specs=in_specs,
        out_specs=out_specs,
        out_shape=out_shape,
        compiler_params=_params(1),
        name="norm_qkv",
    )(x, g, wt_bf16)


def _t5_bucket(dist):
    d = np.maximum(dist, 0)
    df = np.maximum(d, 1).astype(np.float32)
    large = MAX_EXACT + (np.log(df / np.float32(MAX_EXACT)) / np.float32(math.log(MAX_DISTANCE / MAX_EXACT))
                         * np.float32(N_BUCKETS - MAX_EXACT)).astype(np.int32)
    large = np.minimum(large, N_BUCKETS - 1)
    return np.where(d < MAX_EXACT, d, large)


HEADS_PER_SLAB = LANES // A_HEAD_DIM
N_KV_SLABS = NKV_A // LANES
HEADS_PER_KV_SLAB = HEADS_PER_SLAB * A_GROUP
TILE_COLS = HEADS_PER_KV_SLAB * WINDOW


def _bias_kernel(bucket_ref, rb_ref, out_ref):
    bucket = bucket_ref[...]
    has_prev = lax.broadcasted_iota(jnp.int32, bucket.shape, 0) >= WINDOW
    for h in range(A_HEADS):
        acc = jnp.full(bucket.shape, NEG, F32)
        for b in range(N_BUCKETS):
            acc = jnp.where(bucket == b, rb_ref[b, h], acc)
        acc = acc * LOG2E
        t, hh = divmod(h, HEADS_PER_KV_SLAB)
        out_ref[0, t, :, hh * WINDOW:(hh + 1) * WINDOW] = acc
        out_ref[1, t, :, hh * WINDOW:(hh + 1) * WINDOW] = jnp.where(has_prev, acc, NEG)


def _bias_table(rel_bias):
    kj = np.arange(2 * WINDOW, dtype=np.int32)[:, None]
    qi = np.arange(WINDOW, dtype=np.int32)[None, :]
    dist = qi + WINDOW - kj
    valid = (dist >= 0) & (dist < WINDOW)
    bucket = jnp.asarray(np.where(valid, _t5_bucket(dist), -1).astype(np.int32))
    return pl.pallas_call(
        _bias_kernel,
        in_specs=[_resident(), pl.BlockSpec(memory_space=pltpu.SMEM)],
        out_specs=_resident(),
        out_shape=jax.ShapeDtypeStruct((2, N_KV_SLABS, 2 * WINDOW, TILE_COLS), F32),
        name="t5_bias_table",
    )(bucket, rel_bias)


SWA_BLOCKS = 8
SWA_HEAD_ORDER = tuple(
    HEADS_PER_KV_SLAB * t + A_GROUP * half + i
    for t in range(N_KV_SLABS) for i in range(A_GROUP) for half in range(HEADS_PER_SLAB))


def _swa_prompt_kernel(qt_ref, kp_ref, kc_ref, vtp_ref, vtc_ref, bias0_ref, bias_ref, sink_ref, o_ref):
    k = jnp.concatenate([kp_ref[...], kc_ref[...]], axis=0).astype(BF16)
    vt = jnp.concatenate([vtp_ref[...], vtc_ref[...]], axis=1)
    top = lax.broadcasted_iota(jnp.int32, (LANES, WINDOW), 0) < A_HEAD_DIM
    zero = jnp.zeros((LANES, WINDOW), BF16)
    units = [(blk, t) for blk in range(SWA_BLOCKS) for t in range(N_KV_SLABS)]

    scores = []
    for blk, t in units:
        cols = slice(blk * WINDOW, (blk + 1) * WINDOW)
        qs = [qt_ref[(A_GROUP * t + i) * LANES:(A_GROUP * t + i + 1) * LANES, cols] for i in range(A_GROUP)]
        rhs = jnp.concatenate([jnp.where(top, x, zero) for x in qs] + [jnp.where(top, zero, x) for x in qs],
                              axis=1)
        kwin = k[blk * WINDOW:(blk + 2) * WINDOW, t * LANES:(t + 1) * LANES]
        scores.append(jnp.dot(kwin, rhs, preferred_element_type=F32))

    probs, sink_terms = [], []
    for (blk, t), s in zip(units, scores):
        s = s + (bias0_ref[t] if blk == 0 else bias_ref[t])
        sink = sink_ref[t] * LOG2E
        m = jnp.maximum(jnp.max(s, axis=0, keepdims=True), sink)
        sink_terms.append(jnp.exp2(sink - m))
        probs.append(jnp.exp2(s - m).astype(BF16))

    ones = jnp.ones((BF16_SUBLANES, 2 * WINDOW), BF16)
    outs = []
    for (blk, t), p in zip(units, probs):
        vwin = vt[t * LANES:(t + 1) * LANES, blk * WINDOW:(blk + 2) * WINDOW]
        outs.append(jnp.dot(jnp.concatenate([vwin, ones], axis=0), p, preferred_element_type=F32))

    for (blk, t), ot, sink_term in zip(units, outs, sink_terms):
        ot = ot[:LANES] * (1.0 / (ot[LANES:LANES + 1] + sink_term))
        for i in range(A_GROUP):
            first = ot[:, i * WINDOW:(i + 1) * WINDOW]
            second = ot[:, (A_GROUP + i) * WINDOW:(A_GROUP + i + 1) * WINDOW]
            o_ref[blk * WINDOW:(blk + 1) * WINDOW, (A_GROUP * t + i) * LANES:(A_GROUP * t + i + 1) * LANES] = (
                jnp.where(top, first, second).T.astype(BF16))


def _swa_prompt(qt, k, vt, bias, sinks, batch, seq):
    tq = SWA_BLOCKS * WINDOW
    steps = seq // tq
    blocks = seq // WINDOW
    sink_rows = jnp.repeat(sinks, WINDOW).reshape(N_KV_SLABS, 1, TILE_COLS)
    prev_blk = lambda b, j: b * blocks + jnp.maximum(SWA_BLOCKS * j - 1, 0)
    tile = (None, N_KV_SLABS, 2 * WINDOW, TILE_COLS)
    return pl.pallas_call(
        _swa_prompt_kernel,
        grid=(batch, steps),
        in_specs=[
            pl.BlockSpec((NQ_A, tq), lambda b, j: (0, b * steps + j)),
            pl.BlockSpec((WINDOW, NKV_A), lambda b, j: (prev_blk(b, j), 0)),
            pl.BlockSpec((tq, NKV_A), lambda b, j: (b * steps + j, 0)),
            pl.BlockSpec((NKV_A, WINDOW), lambda b, j: (0, prev_blk(b, j))),
            pl.BlockSpec((NKV_A, tq), lambda b, j: (0, b * steps + j)),
            pl.BlockSpec(tile, lambda b, j: (jnp.where(j == 0, 1, 0), 0, 0, 0)),
            _resident(),
            _resident(),
        ],
        out_specs=pl.BlockSpec((tq, NQ_A), lambda b, j: (b * steps + j, 0)),
        out_shape=jax.ShapeDtypeStruct((batch * seq, NQ_A), BF16),
        compiler_params=_params(2),
        name="swa_prompt_attn",
    )(qt, k, k, vt, vt, bias, bias[0], sink_rows)


def _swa_sample_kernel(q_ref, kbt_ref, vbt_ref, knt_ref, vnt_ref, bias_ref, sink_ref,
                       o_ref, kout_ref, vout_ref):
    sink_all = sink_ref[...] * LOG2E
    n_rows, _, wb = kbt_ref.shape
    newest = lax.broadcasted_iota(jnp.int32, (NKV_A, wb), 1) == wb - 1
    kb, vb = [], []
    for r in range(n_rows):
        kw = jnp.where(newest, knt_ref[:, r:r + 1], pltpu.roll(kbt_ref[r], wb - 1, 1))
        vw = jnp.where(newest, vnt_ref[:, r:r + 1], pltpu.roll(vbt_ref[r], wb - 1, 1))
        kout_ref[r] = kw
        vout_ref[r] = vw
        kb.append(kw.astype(BF16))
        vb.append(vw.astype(BF16))
    units = [(r, g) for r in range(n_rows) for g in range(A_KV_HEADS)]
    heads = lambda g: slice(g * A_GROUP, (g + 1) * A_GROUP)
    dims = lambda g: slice(g * A_HEAD_DIM, (g + 1) * A_HEAD_DIM)
    scores = [jnp.dot(q_ref[r, heads(g), :].astype(BF16), kb[r][dims(g), :], preferred_element_type=F32)
              for r, g in units]
    probs, inv = [], []
    for (r, g), s in zip(units, scores):
        s = s + bias_ref[heads(g), :]
        sink = sink_all[heads(g), :]
        m = jnp.maximum(jnp.max(s, axis=-1, keepdims=True), sink)
        e = jnp.exp2(s - m)
        inv.append(1.0 / (jnp.sum(e, axis=-1, keepdims=True) + jnp.exp2(sink - m)))
        probs.append(e.astype(BF16))
    outs = [lax.dot_general(p, vb[r][dims(g), :], NT_DIMS, preferred_element_type=F32)
            for (r, g), p in zip(units, probs)]
    for (r, g), pv, rcp in zip(units, outs, inv):
        o_ref[r, heads(g), :] = pv * rcp


def _swa_sample(q, kt_new, vt_new, kbuf_t, vbuf_t, bias_row, sinks):
    nb = kbuf_t.shape[0]
    return pl.pallas_call(
        _swa_sample_kernel,
        in_specs=[_resident()] * 7,
        out_specs=[_resident()] * 3,
        out_shape=[
            jax.ShapeDtypeStruct((nb, A_HEADS, A_HEAD_DIM), F32),
            jax.ShapeDtypeStruct(kbuf_t.shape, F32),
            jax.ShapeDtypeStruct(vbuf_t.shape, F32),
        ],
        compiler_params=pltpu.CompilerParams(vmem_limit_bytes=VMEM_LIMIT_BYTES),
        name="swa_sample_attn",
    )(q, kbuf_t, vbuf_t, kt_new, vt_new, bias_row, sinks.reshape(A_HEADS, 1))


FF_CHUNK = 1024


def _out_mlp_kernel(*refs, pair_major, final, layer):
    if final:
        o_ref, x_ref, wo_ref, g_ref, wup_hbm, wdn_hbm, gf_ref, out_ref, wup_ref, wdn_ref, sem = refs
    else:
        o_ref, x_ref, wo_ref, g_ref, wup_hbm, wdn_hbm, out_ref, wup_ref, wdn_ref, sem = refs

    @pl.when(pl.program_id(0) == 0)
    def _():
        copies = [pltpu.make_async_copy(wup_hbm.at[layer], wup_ref, sem.at[0]),
                  pltpu.make_async_copy(wdn_hbm.at[layer], wdn_ref, sem.at[1])]
        for cp in copies:
            cp.start()
        for cp in copies:
            cp.wait()

    if pair_major:
        o = jnp.concatenate([o_ref[j] for j in range(o_ref.shape[0])], axis=-1)
    else:
        o = o_ref[...]
    x1 = x_ref[...] + jnp.dot(o, wo_ref[...], preferred_element_type=F32)
    h = _rms(x1, g_ref[...]).astype(BF16)
    acc = x1
    for c in range(D_FF // FF_CHUNK):
        sl = slice(c * FF_CHUNK, (c + 1) * FF_CHUNK)
        u = jnp.dot(h, wup_ref[:, sl], preferred_element_type=F32)
        u = jnp.square(jnp.maximum(u, 0.0)).astype(BF16)
        acc = acc + jnp.dot(u, wdn_ref[sl, :], preferred_element_type=F32)
    if final:
        acc = _rms(acc, gf_ref[...])
    out_ref[...] = acc


def _out_mlp(o, x, wo, g, wup_all, wdn_all, layer, gf, tm, pair_major):
    m = x.shape[0]
    final = gf is not None
    if pair_major:
        tiles_per_group = o.shape[2] // tm
        o_spec = pl.BlockSpec((None, o.shape[1], tm, LANES),
                              lambda i: (i // tiles_per_group, 0, i % tiles_per_group, 0))
    else:
        o_spec = pl.BlockSpec((tm, D_MODEL), lambda i: (i, 0))
    row_spec = pl.BlockSpec((tm, D_MODEL), lambda i: (i, 0))
    in_hbm = pl.BlockSpec(memory_space=pl.ANY)
    in_specs = [o_spec, row_spec, _resident(), _resident(), in_hbm, in_hbm]
    args = [o, x, wo, g, wup_all, wdn_all]
    if final:
        in_specs.append(_resident())
        args.append(gf)
    return pl.pallas_call(
        functools.partial(_out_mlp_kernel, pair_major=pair_major, final=final, layer=layer),
        grid=(m // tm,),
        in_specs=in_specs,
        out_specs=row_spec,
        out_shape=jax.ShapeDtypeStruct((m, D_MODEL), F32),
        scratch_shapes=[
            pltpu.VMEM((D_MODEL, D_FF), BF16),
            pltpu.VMEM((D_FF, D_MODEL), BF16),
            pltpu.SemaphoreType.DMA((2,)),
        ],
        compiler_params=_params(1),
        name="out_proj_mlp",
    )(*args)


N_PAIRS = B_HEADS // 2
VT_ROWS = V_DIM + BF16_SUBLANES
ROPE_LANE0 = NOPE_DIM
C_IN_EXT = Q_LORA + KV_LORA + LANES


def _rope_slab(t, c, s1, s2):
    return t * c + pltpu.roll(t, HALF_ROPE, 1) * s1 + pltpu.roll(t, LANES - HALF_ROPE, 1) * s2


def _mla_proj_kernel(x_ref, g_ref, win_ref, qn_ref, kvn_ref, wqb_ref, wk_ref, wvt_ref,
                     c_ref, s1_ref, s2_ref,
                     qcat_ref, kcat_ref, vt_ref, lat_ref, krt_ref, *, q_scale):
    h = _rms(x_ref[...], g_ref[...]).astype(BF16)
    c = jnp.dot(h, win_ref[...], preferred_element_type=F32)
    cq = _rms(c[:, :Q_LORA], qn_ref[...]).astype(BF16)
    lat = _rms(c[:, Q_LORA:Q_LORA + KV_LORA], kvn_ref[...])
    lat_ref[...] = lat
    cos, s1, s2 = c_ref[...], s1_ref[...], s2_ref[...]
    krp = _rope_slab(c[:, Q_LORA + KV_LORA:], cos, s1, s2)
    krt_ref[...] = krp.T[ROPE_LANE0:ROPE_LANE0 + ROPE_DIM, :]
    q = jnp.dot(cq, wqb_ref[...], preferred_element_type=F32)
    latb = lat.astype(BF16)
    kn = jnp.dot(latb, wk_ref[...], preferred_element_type=F32)
    vt = lax.dot_general(wvt_ref[...], latb, NT_DIMS, preferred_element_type=F32)
    for hh in range(B_HEADS):
        sl = slice(hh * LANES, (hh + 1) * LANES)
        qh = _rope_slab(q[:, sl], cos, s1, s2)
        if q_scale != 1.0:
            qh = qh * q_scale
        qcat_ref[hh] = qh.astype(BF16)
        kcat_ref[hh] = (kn[:, sl] + krp).astype(BF16)
        vt_ref[hh, :V_DIM, :] = vt[hh * V_DIM:(hh + 1) * V_DIM, :].astype(BF16)
        vt_ref[hh, V_DIM:, :] = jnp.ones((VT_ROWS - V_DIM, vt.shape[1]), BF16)


def _mla_proj(x, g, win, qn, kvn, wqb, wk, wvt, tabs, groups, rows, tm, tab_blocks, q_scale):
    m = x.shape[0]
    tiles = rows // tm
    head_map = lambda i: (i // tiles, 0, i % tiles, 0)
    tab_spec = pl.BlockSpec((tm, LANES), lambda i: (i % tab_blocks, 0))
    return pl.pallas_call(
        functools.partial(_mla_proj_kernel, q_scale=q_scale),
        grid=(m // tm,),
        in_specs=[pl.BlockSpec((tm, D_MODEL), lambda i: (i, 0))] + [_resident()] * 7 + [tab_spec] * 3,
        out_specs=[
            pl.BlockSpec((None, B_HEADS, tm, LANES), head_map),
            pl.BlockSpec((None, B_HEADS, tm, LANES), head_map),
            pl.BlockSpec((None, B_HEADS, VT_ROWS, tm), lambda i: (i // tiles, 0, 0, i % tiles)),
            pl.BlockSpec((tm, KV_LORA), lambda i: (i, 0)),
            pl.BlockSpec((None, ROPE_DIM, tm), lambda i: (i // tiles, 0, i % tiles)),
        ],
        out_shape=[
            jax.ShapeDtypeStruct((groups, B_HEADS, rows, LANES), BF16),
            jax.ShapeDtypeStruct((groups, B_HEADS, rows, LANES), BF16),
            jax.ShapeDtypeStruct((groups, B_HEADS, VT_ROWS, rows), BF16),
            jax.ShapeDtypeStruct((m, KV_LORA), F32),
            jax.ShapeDtypeStruct((groups, ROPE_DIM, rows), F32),
        ],
        compiler_params=_params(1),
        name="mla_proj",
    )(x, g, win, qn, kvn, wqb, wk, wvt, *tabs)


def _rope_tables(pos):
    pos = np.asarray(pos, np.float64)
    inv = ROPE_THETA ** (-np.arange(0, ROPE_DIM, 2, dtype=np.float64) / ROPE_DIM)
    ang = pos[:, None] * inv[None, :]
    cos, sin = np.cos(ang).astype(np.float32), np.sin(ang).astype(np.float32)
    s = pos.shape[0]
    ones = np.ones((s, ROPE_LANE0), np.float32)
    z16 = np.zeros((s, HALF_ROPE), np.float32)
    z64 = np.zeros((s, ROPE_LANE0), np.float32)
    z32 = np.zeros((s, LANES - ROPE_LANE0 - ROPE_DIM), np.float32)
    c = np.concatenate([ones, cos, cos, z32], axis=1)
    s1 = np.concatenate([z64, z16, sin, z32], axis=1)
    s2 = np.concatenate([z64, -sin, z16, z32], axis=1)
    return jnp.asarray(c), jnp.asarray(s1), jnp.asarray(s2)


HEADS_PER_BODY = 8
REDUCE_ROWS = 64


def _col_reduce(x, op):
    n, w = x.shape
    part = op(x.reshape(n // REDUCE_ROWS, REDUCE_ROWS, w), axis=0)
    return op(part, axis=0, keepdims=True)


def _mla_attn_kernel(q_ref, k_ref, vt_ref, o_ref, *, tq, n_tiles):
    qi = pl.program_id(1)
    key = lax.broadcasted_iota(jnp.int32, (tq, tq), 0)
    qry = lax.broadcasted_iota(jnp.int32, (tq, tq), 1)
    causal = key <= qry

    def scores(h, n_keys):
        k = k_ref[h, pl.ds(0, n_keys), :]
        return lax.dot_general(k, q_ref[h], NT_DIMS, preferred_element_type=F32)

    def attend(st, h, n_keys):
        diag = jnp.where(causal, st[n_keys - tq:, :], NEG)
        if n_keys > tq:
            full = st[:n_keys - tq, :]
            m = jnp.maximum(_col_reduce(full, jnp.max), _col_reduce(diag, jnp.max))
            p = jnp.concatenate([jnp.exp2(full - m), jnp.exp2(diag - m)], axis=0)
        else:
            m = _col_reduce(diag, jnp.max)
            p = jnp.exp2(diag - m)
        ot = jnp.dot(vt_ref[h, :, pl.ds(0, n_keys)], p.astype(BF16), preferred_element_type=F32)
        return ot[:V_DIM] / ot[V_DIM:V_DIM + 1]

    for c in range(n_tiles):
        @pl.when(qi == c)
        def _(c=c):
            n_keys = (c + 1) * tq

            def body(jj, carry):
                heads = [HEADS_PER_BODY * jj + u for u in range(HEADS_PER_BODY)]
                sts = [scores(h, n_keys) for h in heads]
                outs = [attend(st, h, n_keys) for h, st in zip(heads, sts)]
                for u in range(0, HEADS_PER_BODY, 2):
                    ot = jnp.concatenate([outs[u], outs[u + 1]], axis=0)
                    o_ref[(HEADS_PER_BODY // 2) * jj + u // 2] = ot.T.astype(BF16)
                return carry

            lax.fori_loop(0, B_HEADS // HEADS_PER_BODY, body, 0)


def _mla_attn(qcat, kcat, vt, tq):
    batch, _, seq, _ = qcat.shape
    n_tiles = seq // tq
    return pl.pallas_call(
        functools.partial(_mla_attn_kernel, tq=tq, n_tiles=n_tiles),
        grid=(batch, n_tiles),
        in_specs=[
            pl.BlockSpec((None, B_HEADS, tq, LANES), lambda b, i: (b, 0, i, 0)),
            pl.BlockSpec((None, B_HEADS, seq, LANES), lambda b, i: (b, 0, 0, 0)),
            pl.BlockSpec((None, B_HEADS, VT_ROWS, seq), lambda b, i: (b, 0, 0, 0)),
        ],
        out_specs=pl.BlockSpec((None, N_PAIRS, tq, LANES), lambda b, i: (b, 0, i, 0)),
        out_shape=jax.ShapeDtypeStruct((batch, N_PAIRS, seq, LANES), BF16),
        compiler_params=_params(2),
        name="mla_prompt_attn",
    )(qcat, kcat, vt)


def _absorb_kernel(qcat_ref, wuk_ref, qlat_ref):
    for h in range(B_HEADS):
        qlat_ref[h] = jnp.dot(qcat_ref[h], wuk_ref[h], preferred_element_type=F32).astype(BF16)


def _absorb(qcat, wuk_ext):
    nb = qcat.shape[1]
    return pl.pallas_call(
        _absorb_kernel,
        in_specs=[_resident(), _resident()],
        out_specs=_resident(),
        out_shape=jax.ShapeDtypeStruct((B_HEADS, nb, KV_LORA), BF16),
        name="mla_absorb_q",
    )(qcat, wuk_ext)


def _decode_kernel(pt_ref, qlat_ref, qr_ref, latnew_ref, krnew_ref, lat_hbm, krt_hbm, o_ref,
                   latbuf, krbuf, sem, m_sc, l_sc, acc_sc,
                   *, pages_per_step, steps_per_batch, n_chains, scale):
    b = pl.program_id(0)
    c = pl.program_id(1)
    step = b * steps_per_batch + c
    n_steps = pl.num_programs(0) * steps_per_batch
    last_step = step + 1 == n_steps
    n_slots = latbuf.shape[0]
    slot = step % n_slots
    page_size = latbuf.shape[2]

    def start_chunk(chunk, sl):
        for p in range(pages_per_step):
            page = pt_ref[chunk * pages_per_step + p]
            pltpu.make_async_copy(lat_hbm.at[page], latbuf.at[sl, p], sem.at[0, sl]).start()
            pltpu.make_async_copy(krt_hbm.at[page], krbuf.at[sl, p], sem.at[1, sl]).start()

    def wait_chunk(sl):
        pltpu.make_async_copy(lat_hbm.at[pl.ds(0, pages_per_step)], latbuf.at[sl], sem.at[0, sl]).wait()
        pltpu.make_async_copy(krt_hbm.at[pl.ds(0, pages_per_step)], krbuf.at[sl], sem.at[1, sl]).wait()

    @pl.when(step == 0)
    def _():
        for k in range(n_slots - 1):
            start_chunk(k, k)

    ahead = n_slots - 1
    start_chunk(jnp.minimum(step + ahead, n_steps - 1), (step + ahead) % n_slots)

    qlat = qlat_ref[...]
    qr = qr_ref[...]

    @pl.when(c == 0)
    def _():
        ln = latnew_ref[...].astype(BF16).astype(F32)
        kn = krnew_ref[...].astype(BF16).astype(F32)
        s_new = (jnp.sum(qlat.astype(F32) * ln, axis=-1, keepdims=True)
                 + jnp.sum(qr.astype(F32) * kn, axis=-1, keepdims=True)) * scale
        m_sc[...] = jnp.full(m_sc.shape, NEG, F32)
        l_sc[...] = jnp.zeros(l_sc.shape, F32)
        acc_sc[...] = jnp.zeros(acc_sc.shape, F32)
        m_sc[0] = s_new
        l_sc[0] = jnp.ones_like(s_new)
        acc_sc[0] = jnp.broadcast_to(ln, acc_sc.shape[1:])

    wait_chunk(slot)

    ppc = pages_per_step // n_chains
    chains = range(n_chains)
    s_rope = [jnp.concatenate(
        [jnp.dot(qr, krbuf[slot, i * ppc + p].astype(BF16), preferred_element_type=F32)
         for p in range(ppc)], axis=1) for i in chains]
    lat = [latbuf[slot, pl.ds(i * ppc, ppc)].reshape(ppc * page_size, KV_LORA).astype(BF16) for i in chains]
    s = [(lax.dot_general(qlat, lat[i], NT_DIMS, preferred_element_type=F32) + s_rope[i]) * scale
         for i in chains]
    m_old = [m_sc[i] for i in chains]
    m_new = [jnp.maximum(m_old[i], jnp.max(s[i], axis=-1, keepdims=True)) for i in chains]
    alpha = [jnp.exp(m_old[i] - m_new[i]) for i in chains]
    p = [jnp.exp(s[i] - m_new[i]) for i in chains]
    pv = [jnp.dot(p[i].astype(BF16), lat[i], preferred_element_type=F32) for i in chains]
    for i in chains:
        l_sc[i] = alpha[i] * l_sc[i] + jnp.sum(p[i], axis=-1, keepdims=True)
        acc_sc[i] = alpha[i] * acc_sc[i] + pv[i]
        m_sc[i] = m_new[i]

    @pl.when(c == steps_per_batch - 1)
    def _():
        m = m_sc[0]
        for i in range(1, n_chains):
            m = jnp.maximum(m, m_sc[i])
        l = jnp.zeros_like(m)
        acc = jnp.zeros(acc_sc.shape[1:], F32)
        for i in range(n_chains):
            w = jnp.exp(m_sc[i] - m)
            l = l + w * l_sc[i]
            acc = acc + w * acc_sc[i]
        o_ref[...] = acc / l

    @pl.when(last_step)
    def _():
        for k in range(1, n_slots):
            wait_chunk((step + k) % n_slots)


def _mla_decode(page_table, qlat, qr, lat_new, kr_new, lat_pool, krt_pool, pages_per_step, n_chains):
    nb, n_pages = page_table.shape
    page_size = lat_pool.shape[1]
    steps_per_batch = n_pages // pages_per_step
    scale = (NOPE_DIM + ROPE_DIM) ** -0.5
    per_b = lambda b, c, pt: (b, 0, 0)
    grid_spec = pltpu.PrefetchScalarGridSpec(
        num_scalar_prefetch=1,
        grid=(nb, steps_per_batch),
        in_specs=[
            pl.BlockSpec((None, B_HEADS, KV_LORA), per_b),
            pl.BlockSpec((None, B_HEADS, ROPE_DIM), per_b),
            pl.BlockSpec((None, 1, KV_LORA), per_b),
            pl.BlockSpec((None, 1, ROPE_DIM), per_b),
            pl.BlockSpec(memory_space=pl.ANY),
            pl.BlockSpec(memory_space=pl.ANY),
        ],
        out_specs=pl.BlockSpec((None, B_HEADS, KV_LORA), per_b),
        scratch_shapes=[
            pltpu.VMEM((DECODE_SLOTS, pages_per_step, page_size, KV_LORA), F32),
            pltpu.VMEM((DECODE_SLOTS, pages_per_step, ROPE_DIM, page_size), F32),
            pltpu.SemaphoreType.DMA((2, DECODE_SLOTS)),
            pltpu.VMEM((n_chains, B_HEADS, 1), F32),
            pltpu.VMEM((n_chains, B_HEADS, 1), F32),
            pltpu.VMEM((n_chains, B_HEADS, KV_LORA), F32),
        ],
    )
    return pl.pallas_call(
        functools.partial(_decode_kernel, pages_per_step=pages_per_step,
                          steps_per_batch=steps_per_batch, n_chains=n_chains, scale=scale),
        grid_spec=grid_spec,
        out_shape=jax.ShapeDtypeStruct((nb, B_HEADS, KV_LORA), F32),
        compiler_params=_params(2),
        name="mla_decode_attn",
    )(page_table.reshape(-1), qlat, qr, lat_new, kr_new, lat_pool, krt_pool)


def _uv_kernel(olat_ref, wuv_ref, o_ref):
    for h in range(B_HEADS):
        o_ref[:, h * V_DIM:(h + 1) * V_DIM] = jnp.dot(
            olat_ref[h].astype(BF16), wuv_ref[h], preferred_element_type=F32).astype(BF16)


def _uv(olat_hm, wuv):
    nb = olat_hm.shape[1]
    return pl.pallas_call(
        _uv_kernel,
        in_specs=[_resident(), _resident()],
        out_specs=_resident(),
        out_shape=jax.ShapeDtypeStruct((nb, B_HEADS * V_DIM), BF16),
        name="mla_value_up",
    )(olat_hm, wuv)


QKV_TM = 1024
PROMPT_TM = 1024
MLP_TM = 1024
MLA_TQ = 256
DECODE_PAGES = 64
DECODE_CHAINS = 4
DECODE_SLOTS = 3


def kernel(x_prompt, x_sample, cache_a_k, cache_a_v, cache_b_latent, cache_b_krope, page_table,
           rel_bias, norm_mix, norm_mlp, norm_final, a_w_qkv, a_w_o, a_sinks,
           b_w_in, b_q_norm, b_kv_norm, b_w_q_b, b_w_kv_b, b_w_o, mlp_w_up, mlp_w_down):
    batch, seq, _ = x_prompt.shape
    nb_s = x_sample.shape[0]
    past = page_table.shape[1] * cache_b_latent.shape[2]
    wb = cache_a_k.shape[2]

    xp = x_prompt.reshape(batch * seq, D_MODEL)
    xs = x_sample.reshape(nb_s, D_MODEL)
    row = lambda t: t.reshape(1, -1)

    order = jnp.array(SWA_HEAD_ORDER)
    inverse = jnp.argsort(order)
    w_q = a_w_qkv[0][:, :NQ_A].reshape(D_MODEL, A_HEADS, A_HEAD_DIM)[:, order].reshape(D_MODEL, NQ_A)
    w_qkvt = jnp.concatenate([w_q, a_w_qkv[0][:, NQ_A:]], axis=1).T.astype(BF16)
    w_o_a = a_w_o[0].reshape(A_HEADS, A_HEAD_DIM, D_MODEL)[order].reshape(NQ_A, D_MODEL).astype(BF16)
    w_up, w_dn = mlp_w_up.astype(BF16), mlp_w_down.astype(BF16)
    bias = _bias_table(rel_bias)
    sinks = a_sinks[0]
    wbp = min(WINDOW, seq)

    qtp, kp, vtp, kt_tail, vt_tail = _norm_qkv(xp, row(norm_mix[0]), w_qkvt, QKV_TM, batch, wbp)
    op = _swa_prompt(qtp, kp, vtp, bias, sinks, batch, seq)
    xp = _out_mlp(op, xp, w_o_a, row(norm_mlp[0]), w_up, w_dn, 0, None, MLP_TM, False)

    qts, kts, vts = _norm_qkv(xs, row(norm_mix[0]), w_qkvt, nb_s)
    qs3 = qts.T.astype(F32).reshape(nb_s, A_HEADS, A_HEAD_DIM)[:, inverse]
    bias_row = jnp.transpose(
        bias[0].reshape(N_KV_SLABS, 2 * WINDOW, HEADS_PER_KV_SLAB, WINDOW)[:, WINDOW:, :, WINDOW - 1],
        (0, 2, 1)).reshape(A_HEADS, WINDOW)
    to_t = lambda c: jnp.transpose(c[0], (0, 2, 3, 1)).reshape(nb_s, NKV_A, wb)
    from_t = lambda t, n, w: jnp.transpose(t.reshape(n, A_KV_HEADS, A_HEAD_DIM, w), (0, 3, 1, 2))[None]
    os3, a_k_st, a_v_st = _swa_sample(qs3, kts, vts, to_t(cache_a_k), to_t(cache_a_v), bias_row, sinks)
    os_ = os3[:, order].reshape(nb_s, NQ_A).astype(BF16)
    xs = _out_mlp(os_, xs, w_o_a, row(norm_mlp[0]), w_up, w_dn, 0, None, nb_s, False)

    w_in = b_w_in[0]
    zeros = lambda n: jnp.zeros((D_MODEL, n), F32)
    w_in_ext = jnp.concatenate([w_in[:, :Q_LORA + KV_LORA], zeros(ROPE_LANE0),
                                w_in[:, Q_LORA + KV_LORA:], zeros(LANES - ROPE_LANE0 - ROPE_DIM)],
                               axis=1).astype(BF16)
    qk_dim = NOPE_DIM + ROPE_DIM
    w_qb = jnp.pad(b_w_q_b[0].reshape(Q_LORA, B_HEADS, qk_dim),
                   ((0, 0), (0, 0), (0, LANES - qk_dim))).reshape(Q_LORA, B_HEADS * LANES).astype(BF16)
    w_kvb = b_w_kv_b[0]
    w_uk, w_uv = w_kvb[..., :NOPE_DIM], w_kvb[..., NOPE_DIM:]
    w_k = jnp.pad(w_uk, ((0, 0), (0, 0), (0, LANES - NOPE_DIM))).reshape(KV_LORA, B_HEADS * LANES).astype(BF16)
    w_vt = w_uv.reshape(KV_LORA, B_HEADS * V_DIM).T.astype(BF16)
    w_o_b = b_w_o[0].astype(BF16)
    proj_w = (row(norm_mix[1]), w_in_ext, row(b_q_norm[0]), row(b_kv_norm[0]), w_qb, w_k, w_vt)

    q_scale_p = (NOPE_DIM + ROPE_DIM) ** -0.5 * LOG2E
    tabs_p = _rope_tables(np.arange(seq))
    qcat, kcat, vt, lat_p, krt_p = _mla_proj(xp, *proj_w, tabs_p, batch, seq, PROMPT_TM, seq // PROMPT_TM,
                                            q_scale_p)
    o_pm = _mla_attn(qcat, kcat, vt, MLA_TQ)
    yp = _out_mlp(o_pm, xp, w_o_b, row(norm_mlp[1]), w_up, w_dn, 1, row(norm_final), MLP_TM, True)

    tabs_s = _rope_tables(np.full((nb_s,), past))
    qcat_s, _, _, lat_s, krt_s = _mla_proj(xs, *proj_w, tabs_s, 1, nb_s, nb_s, 1, 1.0)
    w_uk_ext = jnp.pad(jnp.transpose(w_uk, (1, 2, 0)), ((0, 0), (0, LANES - NOPE_DIM), (0, 0))).astype(BF16)
    qlat = jnp.transpose(_absorb(qcat_s[0], w_uk_ext), (1, 0, 2))
    qr = jnp.transpose(qcat_s[0, :, :, ROPE_LANE0:ROPE_LANE0 + ROPE_DIM], (1, 0, 2))
    kr_s = krt_s[0].T
    krt_pool = jnp.swapaxes(cache_b_krope[0], 1, 2)
    olat = _mla_decode(page_table, qlat, qr, lat_s.reshape(nb_s, 1, KV_LORA), kr_s.reshape(nb_s, 1, ROPE_DIM),
                       cache_b_latent[0], krt_pool, DECODE_PAGES, DECODE_CHAINS)
    o_s = _uv(jnp.transpose(olat, (1, 0, 2)), jnp.transpose(w_uv, (1, 0, 2)).astype(BF16))
    ys = _out_mlp(o_s, xs, w_o_b, row(norm_mlp[1]), w_up, w_dn, 1, row(norm_final), nb_s, False)

    k4 = from_t(kt_tail, batch, wbp)
    v4 = from_t(vt_tail, batch, wbp)
    kr_p = jnp.swapaxes(krt_p, 1, 2)
    return (
        yp.reshape(batch, seq, D_MODEL),
        ys.reshape(nb_s, 1, D_MODEL),
        k4, v4,
        lat_p.reshape(1, batch, seq, KV_LORA),
        kr_p.reshape(1, batch, seq, ROPE_DIM),
        from_t(a_k_st, nb_s, wb),
        from_t(a_v_st, nb_s, wb),
        lat_s.reshape(1, nb_s, 1, KV_LORA),
        kr_s.reshape(1, nb_s, 1, ROPE_DIM),
    )
```

```python
import functools
import math

import jax
import jax.numpy as jnp
import numpy as np
from jax import lax
from jax.experimental import pallas as pl
from jax.experimental.pallas import tpu as pltpu

F32 = jnp.float32
BF16 = jnp.bfloat16

D_MODEL = 1024
A_HEADS = 16
A_KV_HEADS = 4
A_HEAD_DIM = 64
A_GROUP = A_HEADS // A_KV_HEADS
WINDOW = 128
N_BUCKETS = 32
MAX_EXACT = N_BUCKETS // 2
MAX_DISTANCE = 128
B_HEADS = 16
Q_LORA = 768
KV_LORA = 256
NOPE_DIM = 64
ROPE_DIM = 32
HALF_ROPE = ROPE_DIM // 2
V_DIM = 64
ROPE_THETA = 10000.0
D_FF = 4 * D_MODEL
EPS = 1e-6

LANES = 128
BF16_SUBLANES = 16
VMEM_LIMIT_BYTES = 56 * 1024 * 1024

NEG = -1e30
LOG2E = math.log2(math.e)

NT_DIMS = (((1,), (1,)), ((), ()))


def _rms(x, g):
    return x * lax.rsqrt(jnp.mean(x * x, axis=-1, keepdims=True) + EPS) * g


def _params(n_axes):
    return pltpu.CompilerParams(
        dimension_semantics=("arbitrary",) * n_axes,
        vmem_limit_bytes=VMEM_LIMIT_BYTES,
    )


def _resident():
    return pl.BlockSpec(memory_space=pltpu.VMEM)


NQ_A = A_HEADS * A_HEAD_DIM
NKV_A = A_KV_HEADS * A_HEAD_DIM


def _qkv_kernel(x_ref, g_ref, wt_ref, *out_refs, tail_rows, tiles_per_group):
    h = _rms(x_ref[...], g_ref[...]).astype(BF16)
    qkvt = lax.dot_general(wt_ref[...], h, NT_DIMS, preferred_element_type=F32)
    out_refs[0][...] = (qkvt[:NQ_A] * (A_HEAD_DIM ** -0.5 * LOG2E)).astype(BF16)
    kvt = qkvt[NQ_A:]
    if tail_rows:
        _, k_ref, vt_ref, ktail_ref, vtail_ref = out_refs
        k_ref[...] = kvt[:NKV_A].T
        vt_ref[...] = kvt[NKV_A:].astype(BF16)

        @pl.when(pl.program_id(0) % tiles_per_group == tiles_per_group - 1)
        def _():
            ktail_ref[...] = kvt[:NKV_A, kvt.shape[1] - tail_rows:]
            vtail_ref[...] = kvt[NKV_A:, kvt.shape[1] - tail_rows:]
    else:
        _, kt_ref, vt_ref = out_refs
        kt_ref[...] = kvt[:NKV_A]
        vt_ref[...] = kvt[NKV_A:]


def _norm_qkv(x, g, wt_bf16, tm, groups=1, tail_rows=0):
    m = x.shape[0]
    tiles_per_group = m // groups // tm
    col_blk = lambda i: (0, i)
    in_specs = [pl.BlockSpec((tm, D_MODEL), lambda i: (i, 0)), _resident(), _resident()]
    qt_spec = pl.BlockSpec((NQ_A, tm), col_blk)
    qt_shape = jax.ShapeDtypeStruct((NQ_A, m), BF16)
    if tail_rows:
        tail_spec = pl.BlockSpec((None, NKV_A, tail_rows), lambda i: (i // tiles_per_group, 0, 0))
        tail_shape = jax.ShapeDtypeStruct((groups, NKV_A, tail_rows), F32)
        out_specs = [qt_spec, pl.BlockSpec((tm, NKV_A), lambda i: (i, 0)),
                     pl.BlockSpec((NKV_A, tm), col_blk), tail_spec, tail_spec]
        out_shape = [qt_shape, jax.ShapeDtypeStruct((m, NKV_A), F32),
                     jax.ShapeDtypeStruct((NKV_A, m), BF16), tail_shape, tail_shape]
    else:
        out_specs = [qt_spec, pl.BlockSpec((NKV_A, tm), col_blk), pl.BlockSpec((NKV_A, tm), col_blk)]
        out_shape = [qt_shape, jax.ShapeDtypeStruct((NKV_A, m), F32), jax.ShapeDtypeStruct((NKV_A, m), F32)]
    return pl.pallas_call(
        functools.partial(_qkv_kernel, tail_rows=tail_rows, tiles_per_group=tiles_per_group),
        grid=(m // tm,),
        in_specs=in_specs,
        out_specs=out_specs,
        out_shape=out_shape,
        compiler_params=_params(1),
        name="norm_qkv",
    )(x, g, wt_bf16)


def _t5_bucket(dist):
    d = np.maximum(dist, 0)
    df = np.maximum(d, 1).astype(np.float32)
    large = MAX_EXACT + (np.log(df / np.float32(MAX_EXACT)) / np.float32(math.log(MAX_DISTANCE / MAX_EXACT))
                         * np.float32(N_BUCKETS - MAX_EXACT)).astype(np.int32)
    large = np.minimum(large, N_BUCKETS - 1)
    return np.where(d < MAX_EXACT, d, large)


HEADS_PER_SLAB = LANES // A_HEAD_DIM
N_KV_SLABS = NKV_A // LANES
HEADS_PER_KV_SLAB = HEADS_PER_SLAB * A_GROUP
TILE_COLS = HEADS_PER_KV_SLAB * WINDOW


def _bias_kernel(bucket_ref, rb_ref, out_ref):
    bucket = bucket_ref[...]
    has_prev = lax.broadcasted_iota(jnp.int32, bucket.shape, 0) >= WINDOW
    for h in range(A_HEADS):
        acc = jnp.full(bucket.shape, NEG, F32)
        for b in range(N_BUCKETS):
            acc = jnp.where(bucket == b, rb_ref[b, h], acc)
        acc = acc * LOG2E
        t, hh = divmod(h, HEADS_PER_KV_SLAB)
        out_ref[0, t, :, hh * WINDOW:(hh + 1) * WINDOW] = acc
        out_ref[1, t, :, hh * WINDOW:(hh + 1) * WINDOW] = jnp.where(has_prev, acc, NEG)


def _bias_table(rel_bias):
    kj = np.arange(2 * WINDOW, dtype=np.int32)[:, None]
    qi = np.arange(WINDOW, dtype=np.int32)[None, :]
    dist = qi + WINDOW - kj
    valid = (dist >= 0) & (dist < WINDOW)
    bucket = jnp.asarray(np.where(valid, _t5_bucket(dist), -1).astype(np.int32))
    return pl.pallas_call(
        _bias_kernel,
        in_specs=[_resident(), pl.BlockSpec(memory_space=pltpu.SMEM)],
        out_specs=_resident(),
        out_shape=jax.ShapeDtypeStruct((2, N_KV_SLABS, 2 * WINDOW, TILE_COLS), F32),
        name="t5_bias_table",
    )(bucket, rel_bias)


SWA_BLOCKS = 8
SWA_HEAD_ORDER = tuple(
    HEADS_PER_KV_SLAB * t + A_GROUP * half + i
    for t in range(N_KV_SLABS) for i in range(A_GROUP) for half in range(HEADS_PER_SLAB))


def _swa_prompt_kernel(qt_ref, kp_ref, kc_ref, vtp_ref, vtc_ref, bias0_ref, bias_ref, sink_ref, o_ref):
    k = jnp.concatenate([kp_ref[...], kc_ref[...]], axis=0).astype(BF16)
    vt = jnp.concatenate([vtp_ref[...], vtc_ref[...]], axis=1)
    top = lax.broadcasted_iota(jnp.int32, (LANES, WINDOW), 0) < A_HEAD_DIM
    zero = jnp.zeros((LANES, WINDOW), BF16)
    units = [(blk, t) for blk in range(SWA_BLOCKS) for t in range(N_KV_SLABS)]

    scores = []
    for blk, t in units:
        cols = slice(blk * WINDOW, (blk + 1) * WINDOW)
        qs = [qt_ref[(A_GROUP * t + i) * LANES:(A_GROUP * t + i + 1) * LANES, cols] for i in range(A_GROUP)]
        rhs = jnp.concatenate([jnp.where(top, x, zero) for x in qs] + [jnp.where(top, zero, x) for x in qs],
                              axis=1)
        kwin = k[blk * WINDOW:(blk + 2) * WINDOW, t * LANES:(t + 1) * LANES]
        scores.append(jnp.dot(kwin, rhs, preferred_element_type=F32))

    probs, sink_terms = [], []
    for (blk, t), s in zip(units, scores):
        s = s + (bias0_ref[t] if blk == 0 else bias_ref[t])
        sink = sink_ref[t] * LOG2E
        m = jnp.maximum(jnp.max(s, axis=0, keepdims=True), sink)
        sink_terms.append(jnp.exp2(sink - m))
        probs.append(jnp.exp2(s - m).astype(BF16))

    ones = jnp.ones((BF16_SUBLANES, 2 * WINDOW), BF16)
    outs = []
    for (blk, t), p in zip(units, probs):
        vwin = vt[t * LANES:(t + 1) * LANES, blk * WINDOW:(blk + 2) * WINDOW]
        outs.append(jnp.dot(jnp.concatenate([vwin, ones], axis=0), p, preferred_element_type=F32))

    for (blk, t), ot, sink_term in zip(units, outs, sink_terms):
        ot = ot[:LANES] * (1.0 / (ot[LANES:LANES + 1] + sink_term))
        for i in range(A_GROUP):
            first = ot[:, i * WINDOW:(i + 1) * WINDOW]
            second = ot[:, (A_GROUP + i) * WINDOW:(A_GROUP + i + 1) * WINDOW]
            o_ref[blk * WINDOW:(blk + 1) * WINDOW, (A_GROUP * t + i) * LANES:(A_GROUP * t + i + 1) * LANES] = (
                jnp.where(top, first, second).T.astype(BF16))


def _swa_prompt(qt, k, vt, bias, sinks, batch, seq):
    tq = SWA_BLOCKS * WINDOW
    steps = seq // tq
    blocks = seq // WINDOW
    sink_rows = jnp.repeat(sinks, WINDOW).reshape(N_KV_SLABS, 1, TILE_COLS)
    prev_blk = lambda b, j: b * blocks + jnp.maximum(SWA_BLOCKS * j - 1, 0)
    tile = (None, N_KV_SLABS, 2 * WINDOW, TILE_COLS)
    return pl.pallas_call(
        _swa_prompt_kernel,
        grid=(batch, steps),
        in_specs=[
            pl.BlockSpec((NQ_A, tq), lambda b, j: (0, b * steps + j)),
            pl.BlockSpec((WINDOW, NKV_A), lambda b, j: (prev_blk(b, j), 0)),
            pl.BlockSpec((tq, NKV_A), lambda b, j: (b * steps + j, 0)),
            pl.BlockSpec((NKV_A, WINDOW), lambda b, j: (0, prev_blk(b, j))),
            pl.BlockSpec((NKV_A, tq), lambda b, j: (0, b * steps + j)),
            pl.BlockSpec(tile, lambda b, j: (jnp.where(j == 0, 1, 0), 0, 0, 0)),
            _resident(),
            _resident(),
        ],
        out_specs=pl.BlockSpec((tq, NQ_A), lambda b, j: (b * steps + j, 0)),
        out_shape=jax.ShapeDtypeStruct((batch * seq, NQ_A), BF16),
        compiler_params=_params(2),
        name="swa_prompt_attn",
    )(qt, k, k, vt, vt, bias, bias[0], sink_rows)


def _swa_sample_kernel(q_ref, kbt_ref, vbt_ref, knt_ref, vnt_ref, bias_ref, sink_ref,
                       o_ref, kout_ref, vout_ref):
    sink_all = sink_ref[...] * LOG2E
    n_rows, _, wb = kbt_ref.shape
    newest = lax.broadcasted_iota(jnp.int32, (NKV_A, wb), 1) == wb - 1
    kb, vb = [], []
    for r in range(n_rows):
        kw = jnp.where(newest, knt_ref[:, r:r + 1], pltpu.roll(kbt_ref[r], wb - 1, 1))
        vw = jnp.where(newest, vnt_ref[:, r:r + 1], pltpu.roll(vbt_ref[r], wb - 1, 1))
        kout_ref[r] = kw
        vout_ref[r] = vw
        kb.append(kw.astype(BF16))
        vb.append(vw.astype(BF16))
    units = [(r, g) for r in range(n_rows) for g in range(A_KV_HEADS)]
    heads = lambda g: slice(g * A_GROUP, (g + 1) * A_GROUP)
    dims = lambda g: slice(g * A_HEAD_DIM, (g + 1) * A_HEAD_DIM)
    scores = [jnp.dot(q_ref[r, heads(g), :].astype(BF16), kb[r][dims(g), :], preferred_element_type=F32)
              for r, g in units]
    probs, inv = [], []
    for (r, g), s in zip(units, scores):
        s = s + bias_ref[heads(g), :]
        sink = sink_all[heads(g), :]
        m = jnp.maximum(jnp.max(s, axis=-1, keepdims=True), sink)
        e = jnp.exp2(s - m)
        inv.append(1.0 / (jnp.sum(e, axis=-1, keepdims=True) + jnp.exp2(sink - m)))
        probs.append(e.astype(BF16))
    outs = [lax.dot_general(p, vb[r][dims(g), :], NT_DIMS, preferred_element_type=F32)
            for (r, g), p in zip(units, probs)]
    for (r, g), pv, rcp in zip(units, outs, inv):
        o_ref[r, heads(g), :] = pv * rcp


def _swa_sample(q, kt_new, vt_new, kbuf_t, vbuf_t, bias_row, sinks):
    nb = kbuf_t.shape[0]
    return pl.pallas_call(
        _swa_sample_kernel,
        in_specs=[_resident()] * 7,
        out_specs=[_resident()] * 3,
        out_shape=[
            jax.ShapeDtypeStruct((nb, A_HEADS, A_HEAD_DIM), F32),
            jax.ShapeDtypeStruct(kbuf_t.shape, F32),
            jax.ShapeDtypeStruct(vbuf_t.shape, F32),
        ],
        compiler_params=pltpu.CompilerParams(vmem_limit_bytes=VMEM_LIMIT_BYTES),
        name="swa_sample_attn",
    )(q, kbuf_t, vbuf_t, kt_new, vt_new, bias_row, sinks.reshape(A_HEADS, 1))


FF_CHUNK = 1024


def _out_mlp_kernel(*refs, pair_major, final, layer):
    if final:
        o_ref, x_ref, wo_ref, g_ref, wup_hbm, wdn_hbm, gf_ref, out_ref, wup_ref, wdn_ref, sem = refs
    else:
        o_ref, x_ref, wo_ref, g_ref, wup_hbm, wdn_hbm, out_ref, wup_ref, wdn_ref, sem = refs

    @pl.when(pl.program_id(0) == 0)
    def _():
        copies = [pltpu.make_async_copy(wup_hbm.at[layer], wup_ref, sem.at[0]),
                  pltpu.make_async_copy(wdn_hbm.at[layer], wdn_ref, sem.at[1])]
        for cp in copies:
            cp.start()
        for cp in copies:
            cp.wait()

    if pair_major:
        o = jnp.concatenate([o_ref[j] for j in range(o_ref.shape[0])], axis=-1)
    else:
        o = o_ref[...]
    x1 = x_ref[...] + jnp.dot(o, wo_ref[...], preferred_element_type=F32)
    h = _rms(x1, g_ref[...]).astype(BF16)
    acc = x1
    for c in range(D_FF // FF_CHUNK):
        sl = slice(c * FF_CHUNK, (c + 1) * FF_CHUNK)
        u = jnp.dot(h, wup_ref[:, sl], preferred_element_type=F32)
        u = jnp.square(jnp.maximum(u, 0.0)).astype(BF16)
        acc = acc + jnp.dot(u, wdn_ref[sl, :], preferred_element_type=F32)
    if final:
        acc = _rms(acc, gf_ref[...])
    out_ref[...] = acc


def _out_mlp(o, x, wo, g, wup_all, wdn_all, layer, gf, tm, pair_major):
    m = x.shape[0]
    final = gf is not None
    if pair_major:
        tiles_per_group = o.shape[2] // tm
        o_spec = pl.BlockSpec((None, o.shape[1], tm, LANES),
                              lambda i: (i // tiles_per_group, 0, i % tiles_per_group, 0))
    else:
        o_spec = pl.BlockSpec((tm, D_MODEL), lambda i: (i, 0))
    row_spec = pl.BlockSpec((tm, D_MODEL), lambda i: (i, 0))
    in_hbm = pl.BlockSpec(memory_space=pl.ANY)
    in_specs = [o_spec, row_spec, _resident(), _resident(), in_hbm, in_hbm]
    args = [o, x, wo, g, wup_all, wdn_all]
    if final:
        in_specs.append(_resident())
        args.append(gf)
    return pl.pallas_call(
        functools.partial(_out_mlp_kernel, pair_major=pair_major, final=final, layer=layer),
        grid=(m // tm,),
        in_specs=in_specs,
        out_specs=row_spec,
        out_shape=jax.ShapeDtypeStruct((m, D_MODEL), F32),
        scratch_shapes=[
            pltpu.VMEM((D_MODEL, D_FF), BF16),
            pltpu.VMEM((D_FF, D_MODEL), BF16),
            pltpu.SemaphoreType.DMA((2,)),
        ],
        compiler_params=_params(1),
        name="out_proj_mlp",
    )(*args)


N_PAIRS = B_HEADS // 2
VT_ROWS = V_DIM + BF16_SUBLANES
ROPE_LANE0 = NOPE_DIM
C_IN_EXT = Q_LORA + KV_LORA + LANES


def _rope_slab(t, c, s1, s2):
    return t * c + pltpu.roll(t, HALF_ROPE, 1) * s1 + pltpu.roll(t, LANES - HALF_ROPE, 1) * s2


def _mla_proj_kernel(x_ref, g_ref, win_ref, qn_ref, kvn_ref, wqb_ref, wk_ref, wvt_ref,
                     c_ref, s1_ref, s2_ref,
                     qcat_ref, kcat_ref, vt_ref, lat_ref, krt_ref, *, q_scale):
    h = _rms(x_ref[...], g_ref[...]).astype(BF16)
    c = jnp.dot(h, win_ref[...], preferred_element_type=F32)
    cq = _rms(c[:, :Q_LORA], qn_ref[...]).astype(BF16)
    lat = _rms(c[:, Q_LORA:Q_LORA + KV_LORA], kvn_ref[...])
    lat_ref[...] = lat
    cos, s1, s2 = c_ref[...], s1_ref[...], s2_ref[...]
    krp = _rope_slab(c[:, Q_LORA + KV_LORA:], cos, s1, s2)
    krt_ref[...] = krp.T[ROPE_LANE0:ROPE_LANE0 + ROPE_DIM, :]
    q = jnp.dot(cq, wqb_ref[...], preferred_element_type=F32)
    latb = lat.astype(BF16)
    kn = jnp.dot(latb, wk_ref[...], preferred_element_type=F32)
    vt = lax.dot_general(wvt_ref[...], latb, NT_DIMS, preferred_element_type=F32)
    for hh in range(B_HEADS):
        sl = slice(hh * LANES, (hh + 1) * LANES)
        qh = _rope_slab(q[:, sl], cos, s1, s2)
        if q_scale != 1.0:
            qh = qh * q_scale
        qcat_ref[hh] = qh.astype(BF16)
        kcat_ref[hh] = (kn[:, sl] + krp).astype(BF16)
        vt_ref[hh, :V_DIM, :] = vt[hh * V_DIM:(hh + 1) * V_DIM, :].astype(BF16)
        vt_ref[hh, V_DIM:, :] = jnp.ones((VT_ROWS - V_DIM, vt.shape[1]), BF16)


def _mla_proj(x, g, win, qn, kvn, wqb, wk, wvt, tabs, groups, rows, tm, tab_blocks, q_scale):
    m = x.shape[0]
    tiles = rows // tm
    head_map = lambda i: (i // tiles, 0, i % tiles, 0)
    tab_spec = pl.BlockSpec((tm, LANES), lambda i: (i % tab_blocks, 0))
    return pl.pallas_call(
        functools.partial(_mla_proj_kernel, q_scale=q_scale),
        grid=(m // tm,),
        in_specs=[pl.BlockSpec((tm, D_MODEL), lambda i: (i, 0))] + [_resident()] * 7 + [tab_spec] * 3,
        out_specs=[
            pl.BlockSpec((None, B_HEADS, tm, LANES), head_map),
            pl.BlockSpec((None, B_HEADS, tm, LANES), head_map),
            pl.BlockSpec((None, B_HEADS, VT_ROWS, tm), lambda i: (i // tiles, 0, 0, i % tiles)),
            pl.BlockSpec((tm, KV_LORA), lambda i: (i, 0)),
            pl.BlockSpec((None, ROPE_DIM, tm), lambda i: (i // tiles, 0, i % tiles)),
        ],
        out_shape=[
            jax.ShapeDtypeStruct((groups, B_HEADS, rows, LANES), BF16),
            jax.ShapeDtypeStruct((groups, B_HEADS, rows, LANES), BF16),
            jax.ShapeDtypeStruct((groups, B_HEADS, VT_ROWS, rows), BF16),
            jax.ShapeDtypeStruct((m, KV_LORA), F32),
            jax.ShapeDtypeStruct((groups, ROPE_DIM, rows), F32),
        ],
        compiler_params=_params(1),
        name="mla_proj",
    )(x, g, win, qn, kvn, wqb, wk, wvt, *tabs)


def _rope_tables(pos):
    pos = np.asarray(pos, np.float64)
    inv = ROPE_THETA ** (-np.arange(0, ROPE_DIM, 2, dtype=np.float64) / ROPE_DIM)
    ang = pos[:, None] * inv[None, :]
    cos, sin = np.cos(ang).astype(np.float32), np.sin(ang).astype(np.float32)
    s = pos.shape[0]
    ones = np.ones((s, ROPE_LANE0), np.float32)
    z16 = np.zeros((s, HALF_ROPE), np.float32)
    z64 = np.zeros((s, ROPE_LANE0), np.float32)
    z32 = np.zeros((s, LANES - ROPE_LANE0 - ROPE_DIM), np.float32)
    c = np.concatenate([ones, cos, cos, z32], axis=1)
    s1 = np.concatenate([z64, z16, sin, z32], axis=1)
    s2 = np.concatenate([z64, -sin, z16, z32], axis=1)
    return jnp.asarray(c), jnp.asarray(s1), jnp.asarray(s2)


HEADS_PER_BODY = 8
REDUCE_ROWS = 64


def _col_reduce(x, op):
    n, w = x.shape
    part = op(x.reshape(n // REDUCE_ROWS, REDUCE_ROWS, w), axis=0)
    return op(part, axis=0, keepdims=True)


def _mla_attn_kernel(q_ref, k_ref, vt_ref, o_ref, *, tq, n_tiles):
    qi = pl.program_id(1)
    key = lax.broadcasted_iota(jnp.int32, (tq, tq), 0)
    qry = lax.broadcasted_iota(jnp.int32, (tq, tq), 1)
    causal = key <= qry

    def scores(h, n_keys):
        k = k_ref[h, pl.ds(0, n_keys), :]
        return lax.dot_general(k, q_ref[h], NT_DIMS, preferred_element_type=F32)

    def attend(st, h, n_keys):
        diag = jnp.where(causal, st[n_keys - tq:, :], NEG)
        if n_keys > tq:
            full = st[:n_keys - tq, :]
            m = jnp.maximum(_col_reduce(full, jnp.max), _col_reduce(diag, jnp.max))
            p = jnp.concatenate([jnp.exp2(full - m), jnp.exp2(diag - m)], axis=0)
        else:
            m = _col_reduce(diag, jnp.max)
            p = jnp.exp2(diag - m)
        ot = jnp.dot(vt_ref[h, :, pl.ds(0, n_keys)], p.astype(BF16), preferred_element_type=F32)
        return ot[:V_DIM] / ot[V_DIM:V_DIM + 1]

    for c in range(n_tiles):
        @pl.when(qi == c)
        def _(c=c):
            n_keys = (c + 1) * tq

            def body(jj, carry):
                heads = [HEADS_PER_BODY * jj + u for u in range(HEADS_PER_BODY)]
                sts = [scores(h, n_keys) for h in heads]
                outs = [attend(st, h, n_keys) for h, st in zip(heads, sts)]
                for u in range(0, HEADS_PER_BODY, 2):
                    ot = jnp.concatenate([outs[u], outs[u + 1]], axis=0)
                    o_ref[(HEADS_PER_BODY // 2) * jj + u // 2] = ot.T.astype(BF16)
                return carry

            lax.fori_loop(0, B_HEADS // HEADS_PER_BODY, body, 0)


def _mla_attn(qcat, kcat, vt, tq):
    batch, _, seq, _ = qcat.shape
    n_tiles = seq // tq
    return pl.pallas_call(
        functools.partial(_mla_attn_kernel, tq=tq, n_tiles=n_tiles),
        grid=(batch, n_tiles),
        in_specs=[
            pl.BlockSpec((None, B_HEADS, tq, LANES), lambda b, i: (b, 0, i, 0)),
            pl.BlockSpec((None, B_HEADS, seq, LANES), lambda b, i: (b, 0, 0, 0)),
            pl.BlockSpec((None, B_HEADS, VT_ROWS, seq), lambda b, i: (b, 0, 0, 0)),
        ],
        out_specs=pl.BlockSpec((None, N_PAIRS, tq, LANES), lambda b, i: (b, 0, i, 0)),
        out_shape=jax.ShapeDtypeStruct((batch, N_PAIRS, seq, LANES), BF16),
        compiler_params=_params(2),
        name="mla_prompt_attn",
    )(qcat, kcat, vt)


def _absorb_kernel(qcat_ref, wuk_ref, qlat_ref):
    for h in range(B_HEADS):
        qlat_ref[h] = jnp.dot(qcat_ref[h], wuk_ref[h], preferred_element_type=F32).astype(BF16)


def _absorb(qcat, wuk_ext):
    nb = qcat.shape[1]
    return pl.pallas_call(
        _absorb_kernel,
        in_specs=[_resident(), _resident()],
        out_specs=_resident(),
        out_shape=jax.ShapeDtypeStruct((B_HEADS, nb, KV_LORA), BF16),
        name="mla_absorb_q",
    )(qcat, wuk_ext)


def _decode_kernel(pt_ref, qlat_ref, qr_ref, latnew_ref, krnew_ref, lat_hbm, krt_hbm, o_ref,
                   latbuf, krbuf, sem, m_sc, l_sc, acc_sc,
                   *, pages_per_step, steps_per_batch, n_chains, scale):
    b = pl.program_id(0)
    c = pl.program_id(1)
    step = b * steps_per_batch + c
    n_steps = pl.num_programs(0) * steps_per_batch
    last_step = step + 1 == n_steps
    n_slots = latbuf.shape[0]
    slot = step % n_slots
    page_size = latbuf.shape[2]

    def start_chunk(chunk, sl):
        for p in range(pages_per_step):
            page = pt_ref[chunk * pages_per_step + p]
            pltpu.make_async_copy(lat_hbm.at[page], latbuf.at[sl, p], sem.at[0, sl]).start(priority=0)
            pltpu.make_async_copy(krt_hbm.at[page], krbuf.at[sl, p], sem.at[1, sl]).start(priority=1)

    def wait_chunk(sl):
        pltpu.make_async_copy(lat_hbm.at[pl.ds(0, pages_per_step)], latbuf.at[sl], sem.at[0, sl]).wait()
        pltpu.make_async_copy(krt_hbm.at[pl.ds(0, pages_per_step)], krbuf.at[sl], sem.at[1, sl]).wait()

    @pl.when(step == 0)
    def _():
        for k in range(n_slots - 1):
            start_chunk(k, k)

    ahead = n_slots - 1
    start_chunk(jnp.minimum(step + ahead, n_steps - 1), (step + ahead) % n_slots)

    qlat = qlat_ref[...]
    qr = qr_ref[...]

    @pl.when(c == 0)
    def _():
        ln = latnew_ref[...].astype(BF16).astype(F32)
        kn = krnew_ref[...].astype(BF16).astype(F32)
        s_new = (jnp.sum(qlat.astype(F32) * ln, axis=-1, keepdims=True)
                 + jnp.sum(qr.astype(F32) * kn, axis=-1, keepdims=True)) * scale
        m_sc[...] = jnp.full(m_sc.shape, NEG, F32)
        l_sc[...] = jnp.zeros(l_sc.shape, F32)
        acc_sc[...] = jnp.zeros(acc_sc.shape, F32)
        m_sc[0] = s_new
        l_sc[0] = jnp.ones_like(s_new)
        acc_sc[0] = jnp.broadcast_to(ln, acc_sc.shape[1:])

    wait_chunk(slot)

    ppc = pages_per_step // n_chains
    chains = range(n_chains)
    s_rope = [jnp.concatenate(
        [jnp.dot(qr, krbuf[slot, i * ppc + p].astype(BF16), preferred_element_type=F32)
         for p in range(ppc)], axis=1) for i in chains]
    lat = [latbuf[slot, pl.ds(i * ppc, ppc)].reshape(ppc * page_size, KV_LORA).astype(BF16) for i in chains]
    s = [(lax.dot_general(qlat, lat[i], NT_DIMS, preferred_element_type=F32) + s_rope[i]) * scale
         for i in chains]
    m_old = [m_sc[i] for i in chains]
    m_new = [jnp.maximum(m_old[i], jnp.max(s[i], axis=-1, keepdims=True)) for i in chains]
    alpha = [jnp.exp(m_old[i] - m_new[i]) for i in chains]
    p = [jnp.exp(s[i] - m_new[i]) for i in chains]
    pv = [jnp.dot(p[i].astype(BF16), lat[i], preferred_element_type=F32) for i in chains]
    for i in chains:
        l_sc[i] = alpha[i] * l_sc[i] + jnp.sum(p[i], axis=-1, keepdims=True)
        acc_sc[i] = alpha[i] * acc_sc[i] + pv[i]
        m_sc[i] = m_new[i]

    @pl.when(c == steps_per_batch - 1)
    def _():
        m = m_sc[0]
        for i in range(1, n_chains):
            m = jnp.maximum(m, m_sc[i])
        l = jnp.zeros_like(m)
        acc = jnp.zeros(acc_sc.shape[1:], F32)
        for i in range(n_chains):
            w = jnp.exp(m_sc[i] - m)
            l = l + w * l_sc[i]
            acc = acc + w * acc_sc[i]
        o_ref[...] = acc / l

    @pl.when(last_step)
    def _():
        for k in range(1, n_slots):
            wait_chunk((step + k) % n_slots)


def _mla_decode(page_table, qlat, qr, lat_new, kr_new, lat_pool, krt_pool, pages_per_step, n_chains):
    nb, n_pages = page_table.shape
    page_size = lat_pool.shape[1]
    steps_per_batch = n_pages // pages_per_step
    scale = (NOPE_DIM + ROPE_DIM) ** -0.5
    per_b = lambda b, c, pt: (b, 0, 0)
    grid_spec = pltpu.PrefetchScalarGridSpec(
        num_scalar_prefetch=1,
        grid=(nb, steps_per_batch),
        in_specs=[
            pl.BlockSpec((None, B_HEADS, KV_LORA), per_b),
            pl.BlockSpec((None, B_HEADS, ROPE_DIM), per_b),
            pl.BlockSpec((None, 1, KV_LORA), per_b),
            pl.BlockSpec((None, 1, ROPE_DIM), per_b),
            pl.BlockSpec(memory_space=pl.ANY),
            pl.BlockSpec(memory_space=pl.ANY),
        ],
        out_specs=pl.BlockSpec((None, B_HEADS, KV_LORA), per_b),
        scratch_shapes=[
            pltpu.VMEM((DECODE_SLOTS, pages_per_step, page_size, KV_LORA), F32),
            pltpu.VMEM((DECODE_SLOTS, pages_per_step, ROPE_DIM, page_size), F32),
            pltpu.SemaphoreType.DMA((2, DECODE_SLOTS)),
            pltpu.VMEM((n_chains, B_HEADS, 1), F32),
            pltpu.VMEM((n_chains, B_HEADS, 1), F32),
            pltpu.VMEM((n_chains, B_HEADS, KV_LORA), F32),
        ],
    )
    return pl.pallas_call(
        functools.partial(_decode_kernel, pages_per_step=pages_per_step,
                          steps_per_batch=steps_per_batch, n_chains=n_chains, scale=scale),
        grid_spec=grid_spec,
        out_shape=jax.ShapeDtypeStruct((nb, B_HEADS, KV_LORA), F32),
        compiler_params=_params(2),
        name="mla_decode_attn",
    )(page_table.reshape(-1), qlat, qr, lat_new, kr_new, lat_pool, krt_pool)


def _uv_kernel(olat_ref, wuv_ref, o_ref):
    for h in range(B_HEADS):
        o_ref[:, h * V_DIM:(h + 1) * V_DIM] = jnp.dot(
            olat_ref[h].astype(BF16), wuv_ref[h], preferred_element_type=F32).astype(BF16)


def _uv(olat_hm, wuv):
    nb = olat_hm.shape[1]
    return pl.pallas_call(
        _uv_kernel,
        in_specs=[_resident(), _resident()],
        out_specs=_resident(),
        out_shape=jax.ShapeDtypeStruct((nb, B_HEADS * V_DIM), BF16),
        name="mla_value_up",
    )(olat_hm, wuv)


QKV_TM = 1024
PROMPT_TM = 1024
MLP_TM = 1024
MLA_TQ = 256
DECODE_PAGES = 64
DECODE_CHAINS = 4
DECODE_SLOTS = 3


def kernel(x_prompt, x_sample, cache_a_k, cache_a_v, cache_b_latent, cache_b_krope, page_table,
           rel_bias, norm_mix, norm_mlp, norm_final, a_w_qkv, a_w_o, a_sinks,
           b_w_in, b_q_norm, b_kv_norm, b_w_q_b, b_w_kv_b, b_w_o, mlp_w_up, mlp_w_down):
    batch, seq, _ = x_prompt.shape
    nb_s = x_sample.shape[0]
    past = page_table.shape[1] * cache_b_latent.shape[2]
    wb = cache_a_k.shape[2]

    xp = x_prompt.reshape(batch * seq, D_MODEL)
    xs = x_sample.reshape(nb_s, D_MODEL)
    row = lambda t: t.reshape(1, -1)

    order = jnp.array(SWA_HEAD_ORDER)
    inverse = jnp.argsort(order)
    w_q = a_w_qkv[0][:, :NQ_A].reshape(D_MODEL, A_HEADS, A_HEAD_DIM)[:, order].reshape(D_MODEL, NQ_A)
    w_qkvt = jnp.concatenate([w_q, a_w_qkv[0][:, NQ_A:]], axis=1).T.astype(BF16)
    w_o_a = a_w_o[0].reshape(A_HEADS, A_HEAD_DIM, D_MODEL)[order].reshape(NQ_A, D_MODEL).astype(BF16)
    w_up, w_dn = mlp_w_up.astype(BF16), mlp_w_down.astype(BF16)
    bias = _bias_table(rel_bias)
    sinks = a_sinks[0]
    wbp = min(WINDOW, seq)

    qtp, kp, vtp, kt_tail, vt_tail = _norm_qkv(xp, row(norm_mix[0]), w_qkvt, QKV_TM, batch, wbp)
    op = _swa_prompt(qtp, kp, vtp, bias, sinks, batch, seq)
    xp = _out_mlp(op, xp, w_o_a, row(norm_mlp[0]), w_up, w_dn, 0, None, MLP_TM, False)

    qts, kts, vts = _norm_qkv(xs, row(norm_mix[0]), w_qkvt, nb_s)
    qs3 = qts.T.astype(F32).reshape(nb_s, A_HEADS, A_HEAD_DIM)[:, inverse]
    bias_row = jnp.transpose(
        bias[0].reshape(N_KV_SLABS, 2 * WINDOW, HEADS_PER_KV_SLAB, WINDOW)[:, WINDOW:, :, WINDOW - 1],
        (0, 2, 1)).reshape(A_HEADS, WINDOW)
    to_t = lambda c: jnp.transpose(c[0], (0, 2, 3, 1)).reshape(nb_s, NKV_A, wb)
    from_t = lambda t, n, w: jnp.transpose(t.reshape(n, A_KV_HEADS, A_HEAD_DIM, w), (0, 3, 1, 2))[None]
    os3, a_k_st, a_v_st = _swa_sample(qs3, kts, vts, to_t(cache_a_k), to_t(cache_a_v), bias_row, sinks)
    os_ = os3[:, order].reshape(nb_s, NQ_A).astype(BF16)
    xs = _out_mlp(os_, xs, w_o_a, row(norm_mlp[0]), w_up, w_dn, 0, None, nb_s, False)

    w_in = b_w_in[0]
    zeros = lambda n: jnp.zeros((D_MODEL, n), F32)
    w_in_ext = jnp.concatenate([w_in[:, :Q_LORA + KV_LORA], zeros(ROPE_LANE0),
                                w_in[:, Q_LORA + KV_LORA:], zeros(LANES - ROPE_LANE0 - ROPE_DIM)],
                               axis=1).astype(BF16)
    qk_dim = NOPE_DIM + ROPE_DIM
    w_qb = jnp.pad(b_w_q_b[0].reshape(Q_LORA, B_HEADS, qk_dim),
                   ((0, 0), (0, 0), (0, LANES - qk_dim))).reshape(Q_LORA, B_HEADS * LANES).astype(BF16)
    w_kvb = b_w_kv_b[0]
    w_uk, w_uv = w_kvb[..., :NOPE_DIM], w_kvb[..., NOPE_DIM:]
    w_k = jnp.pad(w_uk, ((0, 0), (0, 0), (0, LANES - NOPE_DIM))).reshape(KV_LORA, B_HEADS * LANES).astype(BF16)
    w_vt = w_uv.reshape(KV_LORA, B_HEADS * V_DIM).T.astype(BF16)
    w_o_b = b_w_o[0].astype(BF16)
    proj_w = (row(norm_mix[1]), w_in_ext, row(b_q_norm[0]), row(b_kv_norm[0]), w_qb, w_k, w_vt)

    q_scale_p = (NOPE_DIM + ROPE_DIM) ** -0.5 * LOG2E
    tabs_p = _rope_tables(np.arange(seq))
    qcat, kcat, vt, lat_p, krt_p = _mla_proj(xp, *proj_w, tabs_p, batch, seq, PROMPT_TM, seq // PROMPT_TM,
                                            q_scale_p)
    o_pm = _mla_attn(qcat, kcat, vt, MLA_TQ)
    yp = _out_mlp(o_pm, xp, w_o_b, row(norm_mlp[1]), w_up, w_dn, 1, row(norm_final), MLP_TM, True)

    tabs_s = _rope_tables(np.full((nb_s,), past))
    qcat_s, _, _, lat_s, krt_s = _mla_proj(xs, *proj_w, tabs_s, 1, nb_s, nb_s, 1, 1.0)
    w_uk_ext = jnp.pad(jnp.transpose(w_uk, (1, 2, 0)), ((0, 0), (0, LANES - NOPE_DIM), (0, 0))).astype(BF16)
    qlat = jnp.transpose(_absorb(qcat_s[0], w_uk_ext), (1, 0, 2))
    qr = jnp.transpose(qcat_s[0, :, :, ROPE_LANE0:ROPE_LANE0 + ROPE_DIM], (1, 0, 2))
    kr_s = krt_s[0].T
    krt_pool = jnp.swapaxes(cache_b_krope[0], 1, 2)
    olat = _mla_decode(page_table, qlat, qr, lat_s.reshape(nb_s, 1, KV_LORA), kr_s.reshape(nb_s, 1, ROPE_DIM),
                       cache_b_latent[0], krt_pool, DECODE_PAGES, DECODE_CHAINS)
    o_s = _uv(jnp.transpose(olat, (1, 0, 2)), jnp.transpose(w_uv, (1, 0, 2)).astype(BF16))
    ys = _out_mlp(o_s, xs, w_o_b, row(norm_mlp[1]), w_up, w_dn, 1, row(norm_final), nb_s, False)

    k4 = from_t(kt_tail, batch, wbp)
    v4 = from_t(vt_tail, batch, wbp)
    kr_p = jnp.swapaxes(krt_p, 1, 2)
    return (
        yp.reshape(batch, seq, D_MODEL),
        ys.reshape(nb_s, 1, D_MODEL),
        k4, v4,
        lat_p.reshape(1, batch, seq, KV_LORA),
        kr_p.reshape(1, batch, seq, ROPE_DIM),
        from_t(a_k_st, nb_s, wb),
        from_t(a_v_st, nb_s, wb),
        lat_s.reshape(1, nb_s, 1, KV_LORA),
        kr_s.reshape(1, nb_s, 1, ROPE_DIM),
    )
```

```python
import functools
import math

import jax
import jax.numpy as jnp
import numpy as np
from jax import lax
from jax.experimental import pallas as pl
from jax.experimental.pallas import tpu as pltpu

F32 = jnp.float32
BF16 = jnp.bfloat16

D_MODEL = 1024
A_HEADS = 16
A_KV_HEADS = 4
A_HEAD_DIM = 64
A_GROUP = A_HEADS // A_KV_HEADS
WINDOW = 128
N_BUCKETS = 32
MAX_EXACT = N_BUCKETS // 2
MAX_DISTANCE = 128
B_HEADS = 16
Q_LORA = 768
KV_LORA = 256
NOPE_DIM = 64
ROPE_DIM = 32
HALF_ROPE = ROPE_DIM // 2
V_DIM = 64
ROPE_THETA = 10000.0
D_FF = 4 * D_MODEL
EPS = 1e-6

LANES = 128
BF16_SUBLANES = 16
VMEM_LIMIT_BYTES = 56 * 1024 * 1024

NEG = -1e30
LOG2E = math.log2(math.e)

NT_DIMS = (((1,), (1,)), ((), ()))


def _rms(x, g):
    return x * lax.rsqrt(jnp.mean(x * x, axis=-1, keepdims=True) + EPS) * g


def _params(n_axes):
    return pltpu.CompilerParams(
        dimension_semantics=("arbitrary",) * n_axes,
        vmem_limit_bytes=VMEM_LIMIT_BYTES,
    )


def _resident():
    return pl.BlockSpec(memory_space=pltpu.VMEM)


NQ_A = A_HEADS * A_HEAD_DIM
NKV_A = A_KV_HEADS * A_HEAD_DIM


def _qkv_kernel(x_ref, g_ref, wt_ref, *out_refs, tail_rows, tiles_per_group):
    h = _rms(x_ref[...], g_ref[...]).astype(BF16)
    qkvt = lax.dot_general(wt_ref[...], h, NT_DIMS, preferred_element_type=F32)
    out_refs[0][...] = (qkvt[:NQ_A] * (A_HEAD_DIM ** -0.5 * LOG2E)).astype(BF16)
    kvt = qkvt[NQ_A:]
    if tail_rows:
        _, k_ref, vt_ref, ktail_ref, vtail_ref = out_refs
        k_ref[...] = kvt[:NKV_A].T
        vt_ref[...] = kvt[NKV_A:].astype(BF16)

        @pl.when(pl.program_id(0) % tiles_per_group == tiles_per_group - 1)
        def _():
            ktail_ref[...] = kvt[:NKV_A, kvt.shape[1] - tail_rows:]
            vtail_ref[...] = kvt[NKV_A:, kvt.shape[1] - tail_rows:]
    else:
        _, kt_ref, vt_ref = out_refs
        kt_ref[...] = kvt[:NKV_A]
        vt_ref[...] = kvt[NKV_A:]


def _norm_qkv(x, g, wt_bf16, tm, groups=1, tail_rows=0):
    m = x.shape[0]
    tiles_per_group = m // groups // tm
    col_blk = lambda i: (0, i)
    in_specs = [pl.BlockSpec((tm, D_MODEL), lambda i: (i, 0)), _resident(), _resident()]
    qt_spec = pl.BlockSpec((NQ_A, tm), col_blk)
    qt_shape = jax.ShapeDtypeStruct((NQ_A, m), BF16)
    if tail_rows:
        tail_spec = pl.BlockSpec((None, NKV_A, tail_rows), lambda i: (i // tiles_per_group, 0, 0))
        tail_shape = jax.ShapeDtypeStruct((groups, NKV_A, tail_rows), F32)
        out_specs = [qt_spec, pl.BlockSpec((tm, NKV_A), lambda i: (i, 0)),
                     pl.BlockSpec((NKV_A, tm), col_blk), tail_spec, tail_spec]
        out_shape = [qt_shape, jax.ShapeDtypeStruct((m, NKV_A), F32),
                     jax.ShapeDtypeStruct((NKV_A, m), BF16), tail_shape, tail_shape]
    else:
        out_specs = [qt_spec, pl.BlockSpec((NKV_A, tm), col_blk), pl.BlockSpec((NKV_A, tm), col_blk)]
        out_shape = [qt_shape, jax.ShapeDtypeStruct((NKV_A, m), F32), jax.ShapeDtypeStruct((NKV_A, m), F32)]
    return pl.pallas_call(
        functools.partial(_qkv_kernel, tail_rows=tail_rows, tiles_per_group=tiles_per_group),
        grid=(m // tm,),
        in_specs=in_specs,
        out_specs=out_specs,
        out_shape=out_shape,
        compiler_params=_params(1),
        name="norm_qkv",
    )(x, g, wt_bf16)


def _t5_bucket(dist):
    d = np.maximum(dist, 0)
    df = np.maximum(d, 1).astype(np.float32)
    large = MAX_EXACT + (np.log(df / np.float32(MAX_EXACT)) / np.float32(math.log(MAX_DISTANCE / MAX_EXACT))
                         * np.float32(N_BUCKETS - MAX_EXACT)).astype(np.int32)
    large = np.minimum(large, N_BUCKETS - 1)
    return np.where(d < MAX_EXACT, d, large)


HEADS_PER_SLAB = LANES // A_HEAD_DIM
N_KV_SLABS = NKV_A // LANES
HEADS_PER_KV_SLAB = HEADS_PER_SLAB * A_GROUP
TILE_COLS = HEADS_PER_KV_SLAB * WINDOW


def _bias_kernel(bucket_ref, rb_ref, out_ref):
    bucket = bucket_ref[...]
    has_prev = lax.broadcasted_iota(jnp.int32, bucket.shape, 0) >= WINDOW
    for h in range(A_HEADS):
        acc = jnp.full(bucket.shape, NEG, F32)
        for b in range(N_BUCKETS):
            acc = jnp.where(bucket == b, rb_ref[b, h], acc)
        acc = acc * LOG2E
        t, hh = divmod(h, HEADS_PER_KV_SLAB)
        out_ref[0, t, :, hh * WINDOW:(hh + 1) * WINDOW] = acc
        out_ref[1, t, :, hh * WINDOW:(hh + 1) * WINDOW] = jnp.where(has_prev, acc, NEG)


def _bias_table(rel_bias):
    kj = np.arange(2 * WINDOW, dtype=np.int32)[:, None]
    qi = np.arange(WINDOW, dtype=np.int32)[None, :]
    dist = qi + WINDOW - kj
    valid = (dist >= 0) & (dist < WINDOW)
    bucket = jnp.asarray(np.where(valid, _t5_bucket(dist), -1).astype(np.int32))
    return pl.pallas_call(
        _bias_kernel,
        in_specs=[_resident(), pl.BlockSpec(memory_space=pltpu.SMEM)],
        out_specs=_resident(),
        out_shape=jax.ShapeDtypeStruct((2, N_KV_SLABS, 2 * WINDOW, TILE_COLS), F32),
        name="t5_bias_table",
    )(bucket, rel_bias)


SWA_BLOCKS = 8
SWA_HEAD_ORDER = tuple(
    HEADS_PER_KV_SLAB * t + A_GROUP * half + i
    for t in range(N_KV_SLABS) for i in range(A_GROUP) for half in range(HEADS_PER_SLAB))


def _swa_prompt_kernel(qt_ref, kp_ref, kc_ref, vtp_ref, vtc_ref, bias0_ref, bias_ref, sink_ref, o_ref):
    k = jnp.concatenate([kp_ref[...], kc_ref[...]], axis=0).astype(BF16)
    vt = jnp.concatenate([vtp_ref[...], vtc_ref[...]], axis=1)
    top = lax.broadcasted_iota(jnp.int32, (LANES, WINDOW), 0) < A_HEAD_DIM
    zero = jnp.zeros((LANES, WINDOW), BF16)
    units = [(blk, t) for blk in range(SWA_BLOCKS) for t in range(N_KV_SLABS)]

    scores = []
    for blk, t in units:
        cols = slice(blk * WINDOW, (blk + 1) * WINDOW)
        qs = [qt_ref[(A_GROUP * t + i) * LANES:(A_GROUP * t + i + 1) * LANES, cols] for i in range(A_GROUP)]
        rhs = jnp.concatenate([jnp.where(top, x, zero) for x in qs] + [jnp.where(top, zero, x) for x in qs],
                              axis=1)
        kwin = k[blk * WINDOW:(blk + 2) * WINDOW, t * LANES:(t + 1) * LANES]
        scores.append(jnp.dot(kwin, rhs, preferred_element_type=F32))

    probs, sink_terms = [], []
    for (blk, t), s in zip(units, scores):
        s = s + (bias0_ref[t] if blk == 0 else bias_ref[t])
        sink = sink_ref[t] * LOG2E
        m = jnp.maximum(jnp.max(s, axis=0, keepdims=True), sink)
        sink_terms.append(jnp.exp2(sink - m))
        probs.append(jnp.exp2(s - m).astype(BF16))

    ones = jnp.ones((BF16_SUBLANES, 2 * WINDOW), BF16)
    outs = []
    for (blk, t), p in zip(units, probs):
        vwin = vt[t * LANES:(t + 1) * LANES, blk * WINDOW:(blk + 2) * WINDOW]
        outs.append(jnp.dot(jnp.concatenate([vwin, ones], axis=0), p, preferred_element_type=F32))

    for (blk, t), ot, sink_term in zip(units, outs, sink_terms):
        ot = ot[:LANES] * (1.0 / (ot[LANES:LANES + 1] + sink_term))
        for i in range(A_GROUP):
            first = ot[:, i * WINDOW:(i + 1) * WINDOW]
            second = ot[:, (A_GROUP + i) * WINDOW:(A_GROUP + i + 1) * WINDOW]
            o_ref[blk * WINDOW:(blk + 1) * WINDOW, (A_GROUP * t + i) * LANES:(A_GROUP * t + i + 1) * LANES] = (
                jnp.where(top, first, second).T.astype(BF16))


def _swa_prompt(qt, k, vt, bias, sinks, batch, seq):
    tq = SWA_BLOCKS * WINDOW
    steps = seq // tq
    blocks = seq // WINDOW
    sink_rows = jnp.repeat(sinks, WINDOW).reshape(N_KV_SLABS, 1, TILE_COLS)
    prev_blk = lambda b, j: b * blocks + jnp.maximum(SWA_BLOCKS * j - 1, 0)
    tile = (None, N_KV_SLABS, 2 * WINDOW, TILE_COLS)
    return pl.pallas_call(
        _swa_prompt_kernel,
        grid=(batch, steps),
        in_specs=[
            pl.BlockSpec((NQ_A, tq), lambda b, j: (0, b * steps + j)),
            pl.BlockSpec((WINDOW, NKV_A), lambda b, j: (prev_blk(b, j), 0)),
            pl.BlockSpec((tq, NKV_A), lambda b, j: (b * steps + j, 0)),
            pl.BlockSpec((NKV_A, WINDOW), lambda b, j: (0, prev_blk(b, j))),
            pl.BlockSpec((NKV_A, tq), lambda b, j: (0, b * steps + j)),
            pl.BlockSpec(tile, lambda b, j: (jnp.where(j == 0, 1, 0), 0, 0, 0)),
            _resident(),
            _resident(),
        ],
        out_specs=pl.BlockSpec((tq, NQ_A), lambda b, j: (b * steps + j, 0)),
        out_shape=jax.ShapeDtypeStruct((batch * seq, NQ_A), BF16),
        compiler_params=_params(2),
        name="swa_prompt_attn",
    )(qt, k, k, vt, vt, bias, bias[0], sink_rows)


def _swa_sample_kernel(q_ref, kbt_ref, vbt_ref, knt_ref, vnt_ref, bias_ref, sink_ref,
                       o_ref, kout_ref, vout_ref):
    sink_all = sink_ref[...] * LOG2E
    n_rows, _, wb = kbt_ref.shape
    newest = lax.broadcasted_iota(jnp.int32, (NKV_A, wb), 1) == wb - 1
    kb, vb = [], []
    for r in range(n_rows):
        kw = jnp.where(newest, knt_ref[:, r:r + 1], pltpu.roll(kbt_ref[r], wb - 1, 1))
        vw = jnp.where(newest, vnt_ref[:, r:r + 1], pltpu.roll(vbt_ref[r], wb - 1, 1))
        kout_ref[r] = kw
        vout_ref[r] = vw
        kb.append(kw.astype(BF16))
        vb.append(vw.astype(BF16))
    units = [(r, g) for r in range(n_rows) for g in range(A_KV_HEADS)]
    heads = lambda g: slice(g * A_GROUP, (g + 1) * A_GROUP)
    dims = lambda g: slice(g * A_HEAD_DIM, (g + 1) * A_HEAD_DIM)
    scores = [jnp.dot(q_ref[r, heads(g), :].astype(BF16), kb[r][dims(g), :], preferred_element_type=F32)
              for r, g in units]
    probs, inv = [], []
    for (r, g), s in zip(units, scores):
        s = s + bias_ref[heads(g), :]
        sink = sink_all[heads(g), :]
        m = jnp.maximum(jnp.max(s, axis=-1, keepdims=True), sink)
        e = jnp.exp2(s - m)
        inv.append(1.0 / (jnp.sum(e, axis=-1, keepdims=True) + jnp.exp2(sink - m)))
        probs.append(e.astype(BF16))
    outs = [lax.dot_general(p, vb[r][dims(g), :], NT_DIMS, preferred_element_type=F32)
            for (r, g), p in zip(units, probs)]
    for (r, g), pv, rcp in zip(units, outs, inv):
        o_ref[r, heads(g), :] = pv * rcp


def _swa_sample(q, kt_new, vt_new, kbuf_t, vbuf_t, bias_row, sinks):
    nb = kbuf_t.shape[0]
    return pl.pallas_call(
        _swa_sample_kernel,
        in_specs=[_resident()] * 7,
        out_specs=[_resident()] * 3,
        out_shape=[
            jax.ShapeDtypeStruct((nb, A_HEADS, A_HEAD_DIM), F32),
            jax.ShapeDtypeStruct(kbuf_t.shape, F32),
            jax.ShapeDtypeStruct(vbuf_t.shape, F32),
        ],
        compiler_params=pltpu.CompilerParams(vmem_limit_bytes=VMEM_LIMIT_BYTES),
        name="swa_sample_attn",
    )(q, kbuf_t, vbuf_t, kt_new, vt_new, bias_row, sinks.reshape(A_HEADS, 1))


FF_CHUNK = 1024


def _out_mlp_kernel(*refs, pair_major, final, layer):
    if final:
        o_ref, x_ref, wo_ref, g_ref, wup_hbm, wdn_hbm, gf_ref, out_ref, wup_ref, wdn_ref, sem = refs
    else:
        o_ref, x_ref, wo_ref, g_ref, wup_hbm, wdn_hbm, out_ref, wup_ref, wdn_ref, sem = refs

    @pl.when(pl.program_id(0) == 0)
    def _():
        copies = [pltpu.make_async_copy(wup_hbm.at[layer], wup_ref, sem.at[0]),
                  pltpu.make_async_copy(wdn_hbm.at[layer], wdn_ref, sem.at[1])]
        for cp in copies:
            cp.start()
        for cp in copies:
            cp.wait()

    if pair_major:
        o = jnp.concatenate([o_ref[j] for j in range(o_ref.shape[0])], axis=-1)
    else:
        o = o_ref[...]
    x1 = x_ref[...] + jnp.dot(o, wo_ref[...], preferred_element_type=F32)
    h = _rms(x1, g_ref[...]).astype(BF16)
    acc = x1
    for c in range(D_FF // FF_CHUNK):
        sl = slice(c * FF_CHUNK, (c + 1) * FF_CHUNK)
        u = jnp.dot(h, wup_ref[:, sl], preferred_element_type=F32)
        u = jnp.square(jnp.maximum(u, 0.0)).astype(BF16)
        acc = acc + jnp.dot(u, wdn_ref[sl, :], preferred_element_type=F32)
    if final:
        acc = _rms(acc, gf_ref[...])
    out_ref[...] = acc


def _out_mlp(o, x, wo, g, wup_all, wdn_all, layer, gf, tm, pair_major):
    m = x.shape[0]
    final = gf is not None
    if pair_major:
        tiles_per_group = o.shape[2] // tm
        o_spec = pl.BlockSpec((None, o.shape[1], tm, LANES),
                              lambda i: (i // tiles_per_group, 0, i % tiles_per_group, 0))
    else:
        o_spec = pl.BlockSpec((tm, D_MODEL), lambda i: (i, 0))
    row_spec = pl.BlockSpec((tm, D_MODEL), lambda i: (i, 0))
    in_hbm = pl.BlockSpec(memory_space=pl.ANY)
    in_specs = [o_spec, row_spec, _resident(), _resident(), in_hbm, in_hbm]
    args = [o, x, wo, g, wup_all, wdn_all]
    if final:
        in_specs.append(_resident())
        args.append(gf)
    return pl.pallas_call(
        functools.partial(_out_mlp_kernel, pair_major=pair_major, final=final, layer=layer),
        grid=(m // tm,),
        in_specs=in_specs,
        out_specs=row_spec,
        out_shape=jax.ShapeDtypeStruct((m, D_MODEL), F32),
        scratch_shapes=[
            pltpu.VMEM((D_MODEL, D_FF), BF16),
            pltpu.VMEM((D_FF, D_MODEL), BF16),
            pltpu.SemaphoreType.DMA((2,)),
        ],
        compiler_params=_params(1),
        name="out_proj_mlp",
    )(*args)


N_PAIRS = B_HEADS // 2
VT_ROWS = V_DIM + BF16_SUBLANES
ROPE_LANE0 = NOPE_DIM
C_IN_EXT = Q_LORA + KV_LORA + LANES


def _rope_slab(t, c, s1, s2):
    return t * c + pltpu.roll(t, HALF_ROPE, 1) * s1 + pltpu.roll(t, LANES - HALF_ROPE, 1) * s2


def _mla_proj_kernel(x_ref, g_ref, win_ref, qn_ref, kvn_ref, wqb_ref, wk_ref, wvt_ref,
                     c_ref, s1_ref, s2_ref,
                     qcat_ref, kcat_ref, vt_ref, lat_ref, krt_ref, *, q_scale):
    h = _rms(x_ref[...], g_ref[...]).astype(BF16)
    c = jnp.dot(h, win_ref[...], preferred_element_type=F32)
    cq = _rms(c[:, :Q_LORA], qn_ref[...]).astype(BF16)
    lat = _rms(c[:, Q_LORA:Q_LORA + KV_LORA], kvn_ref[...])
    lat_ref[...] = lat
    cos, s1, s2 = c_ref[...], s1_ref[...], s2_ref[...]
    krp = _rope_slab(c[:, Q_LORA + KV_LORA:], cos, s1, s2)
    krt_ref[...] = krp.T[ROPE_LANE0:ROPE_LANE0 + ROPE_DIM, :]
    q = jnp.dot(cq, wqb_ref[...], preferred_element_type=F32)
    latb = lat.astype(BF16)
    kn = jnp.dot(latb, wk_ref[...], preferred_element_type=F32)
    vt = lax.dot_general(wvt_ref[...], latb, NT_DIMS, preferred_element_type=F32)
    for hh in range(B_HEADS):
        sl = slice(hh * LANES, (hh + 1) * LANES)
        qh = _rope_slab(q[:, sl], cos, s1, s2)
        if q_scale != 1.0:
            qh = qh * q_scale
        qcat_ref[hh] = qh.astype(BF16)
        kcat_ref[hh] = (kn[:, sl] + krp).astype(BF16)
        vt_ref[hh, :V_DIM, :] = vt[hh * V_DIM:(hh + 1) * V_DIM, :].astype(BF16)
        vt_ref[hh, V_DIM:, :] = jnp.ones((VT_ROWS - V_DIM, vt.shape[1]), BF16)


def _mla_proj(x, g, win, qn, kvn, wqb, wk, wvt, tabs, groups, rows, tm, tab_blocks, q_scale):
    m = x.shape[0]
    tiles = rows // tm
    head_map = lambda i: (i // tiles, 0, i % tiles, 0)
    tab_spec = pl.BlockSpec((tm, LANES), lambda i: (i % tab_blocks, 0))
    return pl.pallas_call(
        functools.partial(_mla_proj_kernel, q_scale=q_scale),
        grid=(m // tm,),
        in_specs=[pl.BlockSpec((tm, D_MODEL), lambda i: (i, 0))] + [_resident()] * 7 + [tab_spec] * 3,
        out_specs=[
            pl.BlockSpec((None, B_HEADS, tm, LANES), head_map),
            pl.BlockSpec((None, B_HEADS, tm, LANES), head_map),
            pl.BlockSpec((None, B_HEADS, VT_ROWS, tm), lambda i: (i // tiles, 0, 0, i % tiles)),
            pl.BlockSpec((tm, KV_LORA), lambda i: (i, 0)),
            pl.BlockSpec((None, ROPE_DIM, tm), lambda i: (i // tiles, 0, i % tiles)),
        ],
        out_shape=[
            jax.ShapeDtypeStruct((groups, B_HEADS, rows, LANES), BF16),
            jax.ShapeDtypeStruct((groups, B_HEADS, rows, LANES), BF16),
            jax.ShapeDtypeStruct((groups, B_HEADS, VT_ROWS, rows), BF16),
            jax.ShapeDtypeStruct((m, KV_LORA), F32),
            jax.ShapeDtypeStruct((groups, ROPE_DIM, rows), F32),
        ],
        compiler_params=_params(1),
        name="mla_proj",
    )(x, g, win, qn, kvn, wqb, wk, wvt, *tabs)


def _rope_tables(pos):
    pos = np.asarray(pos, np.float64)
    inv = ROPE_THETA ** (-np.arange(0, ROPE_DIM, 2, dtype=np.float64) / ROPE_DIM)
    ang = pos[:, None] * inv[None, :]
    cos, sin = np.cos(ang).astype(np.float32), np.sin(ang).astype(np.float32)
    s = pos.shape[0]
    ones = np.ones((s, ROPE_LANE0), np.float32)
    z16 = np.zeros((s, HALF_ROPE), np.float32)
    z64 = np.zeros((s, ROPE_LANE0), np.float32)
    z32 = np.zeros((s, LANES - ROPE_LANE0 - ROPE_DIM), np.float32)
    c = np.concatenate([ones, cos, cos, z32], axis=1)
    s1 = np.concatenate([z64, z16, sin, z32], axis=1)
    s2 = np.concatenate([z64, -sin, z16, z32], axis=1)
    return jnp.asarray(c), jnp.asarray(s1), jnp.asarray(s2)


HEADS_PER_BODY = 8
REDUCE_ROWS = 64


def _col_reduce(x, op):
    n, w = x.shape
    part = op(x.reshape(n // REDUCE_ROWS, REDUCE_ROWS, w), axis=0)
    return op(part, axis=0, keepdims=True)


def _mla_attn_kernel(q_ref, k_ref, vt_ref, o_ref, *, tq, n_tiles):
    qi = pl.program_id(1)
    key = lax.broadcasted_iota(jnp.int32, (tq, tq), 0)
    qry = lax.broadcasted_iota(jnp.int32, (tq, tq), 1)
    causal = key <= qry

    def scores(h, n_keys):
        k = k_ref[h, pl.ds(0, n_keys), :]
        return lax.dot_general(k, q_ref[h], NT_DIMS, preferred_element_type=F32)

    def attend(st, h, n_keys):
        diag = jnp.where(causal, st[n_keys - tq:, :], NEG)
        if n_keys > tq:
            full = st[:n_keys - tq, :]
            m = jnp.maximum(_col_reduce(full, jnp.max), _col_reduce(diag, jnp.max))
            p = jnp.concatenate([jnp.exp2(full - m), jnp.exp2(diag - m)], axis=0)
        else:
            m = _col_reduce(diag, jnp.max)
            p = jnp.exp2(diag - m)
        ot = jnp.dot(vt_ref[h, :, pl.ds(0, n_keys)], p.astype(BF16), preferred_element_type=F32)
        return ot[:V_DIM] / ot[V_DIM:V_DIM + 1]

    for c in range(n_tiles):
        @pl.when(qi == c)
        def _(c=c):
            n_keys = (c + 1) * tq

            def body(jj, carry):
                heads = [HEADS_PER_BODY * jj + u for u in range(HEADS_PER_BODY)]
                sts = [scores(h, n_keys) for h in heads]
                outs = [attend(st, h, n_keys) for h, st in zip(heads, sts)]
                for u in range(0, HEADS_PER_BODY, 2):
                    ot = jnp.concatenate([outs[u], outs[u + 1]], axis=0)
                    o_ref[(HEADS_PER_BODY // 2) * jj + u // 2] = ot.T.astype(BF16)
                return carry

            lax.fori_loop(0, B_HEADS // HEADS_PER_BODY, body, 0)


def _mla_attn(qcat, kcat, vt, tq):
    batch, _, seq, _ = qcat.shape
    n_tiles = seq // tq
    return pl.pallas_call(
        functools.partial(_mla_attn_kernel, tq=tq, n_tiles=n_tiles),
        grid=(batch, n_tiles),
        in_specs=[
            pl.BlockSpec((None, B_HEADS, tq, LANES), lambda b, i: (b, 0, i, 0)),
            pl.BlockSpec((None, B_HEADS, seq, LANES), lambda b, i: (b, 0, 0, 0)),
            pl.BlockSpec((None, B_HEADS, VT_ROWS, seq), lambda b, i: (b, 0, 0, 0)),
        ],
        out_specs=pl.BlockSpec((None, N_PAIRS, tq, LANES), lambda b, i: (b, 0, i, 0)),
        out_shape=jax.ShapeDtypeStruct((batch, N_PAIRS, seq, LANES), BF16),
        compiler_params=_params(2),
        name="mla_prompt_attn",
    )(qcat, kcat, vt)


def _absorb_kernel(qcat_ref, wuk_ref, qlat_ref):
    for h in range(B_HEADS):
        qlat_ref[h] = jnp.dot(qcat_ref[h], wuk_ref[h], preferred_element_type=F32).astype(BF16)


def _absorb(qcat, wuk_ext):
    nb = qcat.shape[1]
    return pl.pallas_call(
        _absorb_kernel,
        in_specs=[_resident(), _resident()],
        out_specs=_resident(),
        out_shape=jax.ShapeDtypeStruct((B_HEADS, nb, KV_LORA), BF16),
        name="mla_absorb_q",
    )(qcat, wuk_ext)


def _decode_kernel(pt_ref, qlat_ref, qr_ref, latnew_ref, krnew_ref, lat_hbm, krt_hbm, o_ref,
                   latbuf, krbuf, sem, m_sc, l_sc, acc_sc,
                   *, pages_per_step, steps_per_batch, n_chains, scale):
    b = pl.program_id(0)
    c = pl.program_id(1)
    step = b * steps_per_batch + c
    n_steps = pl.num_programs(0) * steps_per_batch
    last_step = step + 1 == n_steps
    n_slots = latbuf.shape[0]
    slot = step % n_slots
    page_size = latbuf.shape[2]

    def start_chunk(chunk, sl):
        for p in range(pages_per_step):
            page = pt_ref[chunk * pages_per_step + p]
            pltpu.make_async_copy(lat_hbm.at[page], latbuf.at[sl, p], sem.at[0, sl]).start()
            pltpu.make_async_copy(krt_hbm.at[page], krbuf.at[sl, p], sem.at[1, sl]).start()

    def wait_chunk(sl):
        pltpu.make_async_copy(lat_hbm.at[pl.ds(0, pages_per_step)], latbuf.at[sl], sem.at[0, sl]).wait()
        pltpu.make_async_copy(krt_hbm.at[pl.ds(0, pages_per_step)], krbuf.at[sl], sem.at[1, sl]).wait()

    @pl.when(step == 0)
    def _():
        for k in range(n_slots - 1):
            start_chunk(k, k)

    ahead = n_slots - 1
    start_chunk(jnp.minimum(step + ahead, n_steps - 1), (step + ahead) % n_slots)

    qlat = qlat_ref[...]
    qr = qr_ref[...]

    @pl.when(c == 0)
    def _():
        ln = latnew_ref[...].astype(BF16).astype(F32)
        kn = krnew_ref[...].astype(BF16).astype(F32)
        s_new = (jnp.sum(qlat.astype(F32) * ln, axis=-1, keepdims=True)
                 + jnp.sum(qr.astype(F32) * kn, axis=-1, keepdims=True)) * scale
        m_sc[...] = jnp.full(m_sc.shape, NEG, F32)
        l_sc[...] = jnp.zeros(l_sc.shape, F32)
        acc_sc[...] = jnp.zeros(acc_sc.shape, F32)
        m_sc[0] = s_new
        l_sc[0] = jnp.ones_like(s_new)
        acc_sc[0] = jnp.broadcast_to(ln, acc_sc.shape[1:])

    wait_chunk(slot)

    ppc = pages_per_step // n_chains
    chains = range(n_chains)
    s_rope = [jnp.concatenate(
        [jnp.dot(qr, krbuf[slot, i * ppc + p].astype(BF16), preferred_element_type=F32)
         for p in range(ppc)], axis=1) for i in chains]
    lat = [latbuf[slot, pl.ds(i * ppc, ppc)].reshape(ppc * page_size, KV_LORA).astype(BF16) for i in chains]
    s = [(lax.dot_general(qlat, lat[i], NT_DIMS, preferred_element_type=F32) + s_rope[i]) * scale
         for i in chains]
    m_old = [m_sc[i] for i in chains]
    m_new = [jnp.maximum(m_old[i], jnp.max(s[i], axis=-1, keepdims=True)) for i in chains]
    alpha = [jnp.exp(m_old[i] - m_new[i]) for i in chains]
    p = [jnp.exp(s[i] - m_new[i]) for i in chains]
    pv = [jnp.dot(p[i].astype(BF16), lat[i], preferred_element_type=F32) for i in chains]
    for i in chains:
        l_sc[i] = alpha[i] * l_sc[i] + jnp.sum(p[i], axis=-1, keepdims=True)
        acc_sc[i] = alpha[i] * acc_sc[i] + pv[i]
        m_sc[i] = m_new[i]

    @pl.when(c == steps_per_batch - 1)
    def _():
        m = m_sc[0]
        for i in range(1, n_chains):
            m = jnp.maximum(m, m_sc[i])
        l = jnp.zeros_like(m)
        acc = jnp.zeros(acc_sc.shape[1:], F32)
        for i in range(n_chains):
            w = jnp.exp(m_sc[i] - m)
            l = l + w * l_sc[i]
            acc = acc + w * acc_sc[i]
        o_ref[...] = acc / l

    @pl.when(last_step)
    def _():
        for k in range(1, n_slots):
            wait_chunk((step + k) % n_slots)


def _mla_decode(page_table, qlat, qr, lat_new, kr_new, lat_pool, krt_pool, pages_per_step, n_chains):
    nb, n_pages = page_table.shape
    page_size = lat_pool.shape[1]
    steps_per_batch = n_pages // pages_per_step
    scale = (NOPE_DIM + ROPE_DIM) ** -0.5
    per_b = lambda b, c, pt: (b, 0, 0)
    grid_spec = pltpu.PrefetchScalarGridSpec(
        num_scalar_prefetch=1,
        grid=(nb, steps_per_batch),
        in_specs=[
            pl.BlockSpec((None, B_HEADS, KV_LORA), per_b),
            pl.BlockSpec((None, B_HEADS, ROPE_DIM), per_b),
            pl.BlockSpec((None, 1, KV_LORA), per_b),
            pl.BlockSpec((None, 1, ROPE_DIM), per_b),
            pl.BlockSpec(memory_space=pl.ANY),
            pl.BlockSpec(memory_space=pl.ANY),
        ],
        out_specs=pl.BlockSpec((None, B_HEADS, KV_LORA), per_b),
        scratch_shapes=[
            pltpu.VMEM((DECODE_SLOTS, pages_per_step, page_size, KV_LORA), F32),
            pltpu.VMEM((DECODE_SLOTS, pages_per_step, ROPE_DIM, page_size), F32),
            pltpu.SemaphoreType.DMA((2, DECODE_SLOTS)),
            pltpu.VMEM((n_chains, B_HEADS, 1), F32),
            pltpu.VMEM((n_chains, B_HEADS, 1), F32),
            pltpu.VMEM((n_chains, B_HEADS, KV_LORA), F32),
        ],
    )
    return pl.pallas_call(
        functools.partial(_decode_kernel, pages_per_step=pages_per_step,
                          steps_per_batch=steps_per_batch, n_chains=n_chains, scale=scale),
        grid_spec=grid_spec,
        out_shape=jax.ShapeDtypeStruct((nb, B_HEADS, KV_LORA), F32),
        compiler_params=_params(2),
        name="mla_decode_attn",
    )(page_table.reshape(-1), qlat, qr, lat_new, kr_new, lat_pool, krt_pool)


def _uv_kernel(olat_ref, wuv_ref, o_ref):
    for h in range(B_HEADS):
        o_ref[:, h * V_DIM:(h + 1) * V_DIM] = jnp.dot(
            olat_ref[h].astype(BF16), wuv_ref[h], preferred_element_type=F32).astype(BF16)


def _uv(olat_hm, wuv):
    nb = olat_hm.shape[1]
    return pl.pallas_call(
        _uv_kernel,
        in_specs=[_resident(), _resident()],
        out_specs=_resident(),
        out_shape=jax.ShapeDtypeStruct((nb, B_HEADS * V_DIM), BF16),
        name="mla_value_up",
    )(olat_hm, wuv)


QKV_TM = 1024
PROMPT_TM = 1024
MLP_TM = 1024
MLA_TQ = 256
DECODE_PAGES = 64
DECODE_CHAINS = 4
DECODE_SLOTS = 3


def kernel(x_prompt, x_sample, cache_a_k, cache_a_v, cache_b_latent, cache_b_krope, page_table,
           rel_bias, norm_mix, norm_mlp, norm_final, a_w_qkv, a_w_o, a_sinks,
           b_w_in, b_q_norm, b_kv_norm, b_w_q_b, b_w_kv_b, b_w_o, mlp_w_up, mlp_w_down):
    batch, seq, _ = x_prompt.shape
    nb_s = x_sample.shape[0]
    past = page_table.shape[1] * cache_b_latent.shape[2]
    wb = cache_a_k.shape[2]

    xp = x_prompt.reshape(batch * seq, D_MODEL)
    xs = x_sample.reshape(nb_s, D_MODEL)
    row = lambda t: t.reshape(1, -1)

    order = jnp.array(SWA_HEAD_ORDER)
    inverse = jnp.argsort(order)
    w_q = a_w_qkv[0][:, :NQ_A].reshape(D_MODEL, A_HEADS, A_HEAD_DIM)[:, order].reshape(D_MODEL, NQ_A)
    w_qkvt = jnp.concatenate([w_q, a_w_qkv[0][:, NQ_A:]], axis=1).T.astype(BF16)
    w_o_a = a_w_o[0].reshape(A_HEADS, A_HEAD_DIM, D_MODEL)[order].reshape(NQ_A, D_MODEL).astype(BF16)
    w_up, w_dn = mlp_w_up.astype(BF16), mlp_w_down.astype(BF16)
    bias = _bias_table(rel_bias)
    sinks = a_sinks[0]
    wbp = min(WINDOW, seq)

    qtp, kp, vtp, kt_tail, vt_tail = _norm_qkv(xp, row(norm_mix[0]), w_qkvt, QKV_TM, batch, wbp)
    op = _swa_prompt(qtp, kp, vtp, bias, sinks, batch, seq)
    xp = _out_mlp(op, xp, w_o_a, row(norm_mlp[0]), w_up, w_dn, 0, None, MLP_TM, False)

    qts, kts, vts = _norm_qkv(xs, row(norm_mix[0]), w_qkvt, nb_s)
    qs3 = qts.T.astype(F32).reshape(nb_s, A_HEADS, A_HEAD_DIM)[:, inverse]
    bias_row = jnp.transpose(
        bias[0].reshape(N_KV_SLABS, 2 * WINDOW, HEADS_PER_KV_SLAB, WINDOW)[:, WINDOW:, :, WINDOW - 1],
        (0, 2, 1)).reshape(A_HEADS, WINDOW)
    to_t = lambda c: jnp.transpose(c[0], (0, 2, 3, 1)).reshape(nb_s, NKV_A, wb)
    from_t = lambda t, n, w: jnp.transpose(t.reshape(n, A_KV_HEADS, A_HEAD_DIM, w), (0, 3, 1, 2))[None]
    os3, a_k_st, a_v_st = _swa_sample(qs3, kts, vts, to_t(cache_a_k), to_t(cache_a_v), bias_row, sinks)
    os_ = os3[:, order].reshape(nb_s, NQ_A).astype(BF16)
    xs = _out_mlp(os_, xs, w_o_a, row(norm_mlp[0]), w_up, w_dn, 0, None, nb_s, False)

    w_in = b_w_in[0]
    zeros = lambda n: jnp.zeros((D_MODEL, n), F32)
    w_in_ext = jnp.concatenate([w_in[:, :Q_LORA + KV_LORA], zeros(ROPE_LANE0),
                                w_in[:, Q_LORA + KV_LORA:], zeros(LANES - ROPE_LANE0 - ROPE_DIM)],
                               axis=1).astype(BF16)
    qk_dim = NOPE_DIM + ROPE_DIM
    w_qb = jnp.pad(b_w_q_b[0].reshape(Q_LORA, B_HEADS, qk_dim),
                   ((0, 0), (0, 0), (0, LANES - qk_dim))).reshape(Q_LORA, B_HEADS * LANES).astype(BF16)
    w_kvb = b_w_kv_b[0]
    w_uk, w_uv = w_kvb[..., :NOPE_DIM], w_kvb[..., NOPE_DIM:]
    w_k = jnp.pad(w_uk, ((0, 0), (0, 0), (0, LANES - NOPE_DIM))).reshape(KV_LORA, B_HEADS * LANES).astype(BF16)
    w_vt = w_uv.reshape(KV_LORA, B_HEADS * V_DIM).T.astype(BF16)
    w_o_b = b_w_o[0].astype(BF16)
    proj_w = (row(norm_mix[1]), w_in_ext, row(b_q_norm[0]), row(b_kv_norm[0]), w_qb, w_k, w_vt)

    q_scale_p = (NOPE_DIM + ROPE_DIM) ** -0.5 * LOG2E
    tabs_p = _rope_tables(np.arange(seq))
    qcat, kcat, vt, lat_p, krt_p = _mla_proj(xp, *proj_w, tabs_p, batch, seq, PROMPT_TM, seq // PROMPT_TM,
                                            q_scale_p)
    o_pm = _mla_attn(qcat, kcat, vt, MLA_TQ)
    yp = _out_mlp(o_pm, xp, w_o_b, row(norm_mlp[1]), w_up, w_dn, 1, row(norm_final), MLP_TM, True)

    tabs_s = _rope_tables(np.full((nb_s,), past))
    qcat_s, _, _, lat_s, krt_s = _mla_proj(xs, *proj_w, tabs_s, 1, nb_s, nb_s, 1, 1.0)
    w_uk_ext = jnp.pad(jnp.transpose(w_uk, (1, 2, 0)), ((0, 0), (0, LANES - NOPE_DIM), (0, 0))).astype(BF16)
    qlat = jnp.transpose(_absorb(qcat_s[0], w_uk_ext), (1, 0, 2))
    qr = jnp.transpose(qcat_s[0, :, :, ROPE_LANE0:ROPE_LANE0 + ROPE_DIM], (1, 0, 2))
    kr_s = krt_s[0].T
    krt_pool = jnp.swapaxes(cache_b_krope[0], 1, 2)
    olat = _mla_decode(page_table, qlat, qr, lat_s.reshape(nb_s, 1, KV_LORA), kr_s.reshape(nb_s, 1, ROPE_DIM),
                       cache_b_latent[0], krt_pool, DECODE_PAGES, DECODE_CHAINS)
    o_s = _uv(jnp.transpose(olat, (1, 0, 2)), jnp.transpose(w_uv, (1, 0, 2)).astype(BF16))
    ys = _out_mlp(o_s, xs, w_o_b, row(norm_mlp[1]), w_up, w_dn, 1, row(norm_final), nb_s, False)

    k4 = from_t(kt_tail, batch, wbp)
    v4 = from_t(vt_tail, batch, wbp)
    kr_p = jnp.swapaxes(krt_p, 1, 2)
    return (
        yp.reshape(batch, seq, D_MODEL),
        ys.reshape(nb_s, 1, D_MODEL),
        k4, v4,
        lat_p.reshape(1, batch, seq, KV_LORA),
        kr_p.reshape(1, batch, seq, ROPE_DIM),
        from_t(a_k_st, nb_s, wb),
        from_t(a_v_st, nb_s, wb),
        lat_s.reshape(1, nb_s, 1, KV_LORA),
        kr_s.reshape(1, nb_s, 1, ROPE_DIM),
    )
```

```python
import functools
import math

import jax
import jax.numpy as jnp
import numpy as np
from jax import lax
from jax.experimental import pallas as pl
from jax.experimental.pallas import tpu as pltpu

F32 = jnp.float32
BF16 = jnp.bfloat16

D_MODEL = 1024
A_HEADS = 16
A_KV_HEADS = 4
A_HEAD_DIM = 64
A_GROUP = A_HEADS // A_KV_HEADS
WINDOW = 128
N_BUCKETS = 32
MAX_EXACT = N_BUCKETS // 2
MAX_DISTANCE = 128
B_HEADS = 16
Q_LORA = 768
KV_LORA = 256
NOPE_DIM = 64
ROPE_DIM = 32
HALF_ROPE = ROPE_DIM // 2
V_DIM = 64
ROPE_THETA = 10000.0
D_FF = 4 * D_MODEL
EPS = 1e-6

LANES = 128
BF16_SUBLANES = 16
VMEM_LIMIT_BYTES = 56 * 1024 * 1024

NEG = -1e30
LOG2E = math.log2(math.e)

NT_DIMS = (((1,), (1,)), ((), ()))


def _rms(x, g):
    return x * lax.rsqrt(jnp.mean(x * x, axis=-1, keepdims=True) + EPS) * g


def _params(n_axes):
    return pltpu.CompilerParams(
        dimension_semantics=("arbitrary",) * n_axes,
        vmem_limit_bytes=VMEM_LIMIT_BYTES,
    )


def _resident():
    return pl.BlockSpec(memory_space=pltpu.VMEM)


NQ_A = A_HEADS * A_HEAD_DIM
NKV_A = A_KV_HEADS * A_HEAD_DIM


def _qkv_kernel(x_ref, g_ref, wt_ref, *out_refs, tail_rows, tiles_per_group):
    h = _rms(x_ref[...], g_ref[...]).astype(BF16)
    qkvt = lax.dot_general(wt_ref[...], h, NT_DIMS, preferred_element_type=F32)
    out_refs[0][...] = (qkvt[:NQ_A] * (A_HEAD_DIM ** -0.5 * LOG2E)).astype(BF16)
    kvt = qkvt[NQ_A:]
    if tail_rows:
        _, k_ref, vt_ref, ktail_ref, vtail_ref = out_refs
        k_ref[...] = kvt[:NKV_A].T
        vt_ref[...] = kvt[NKV_A:].astype(BF16)

        @pl.when(pl.program_id(0) % tiles_per_group == tiles_per_group - 1)
        def _():
            ktail_ref[...] = kvt[:NKV_A, kvt.shape[1] - tail_rows:]
            vtail_ref[...] = kvt[NKV_A:, kvt.shape[1] - tail_rows:]
    else:
        _, kt_ref, vt_ref = out_refs
        kt_ref[...] = kvt[:NKV_A]
        vt_ref[...] = kvt[NKV_A:]


def _norm_qkv(x, g, wt_bf16, tm, groups=1, tail_rows=0):
    m = x.shape[0]
    tiles_per_group = m // groups // tm
    col_blk = lambda i: (0, i)
    in_specs = [pl.BlockSpec((tm, D_MODEL), lambda i: (i, 0)), _resident(), _resident()]
    qt_spec = pl.BlockSpec((NQ_A, tm), col_blk)
    qt_shape = jax.ShapeDtypeStruct((NQ_A, m), BF16)
    if tail_rows:
        tail_spec = pl.BlockSpec((None, NKV_A, tail_rows), lambda i: (i // tiles_per_group, 0, 0))
        tail_shape = jax.ShapeDtypeStruct((groups, NKV_A, tail_rows), F32)
        out_specs = [qt_spec, pl.BlockSpec((tm, NKV_A), lambda i: (i, 0)),
                     pl.BlockSpec((NKV_A, tm), col_blk), tail_spec, tail_spec]
        out_shape = [qt_shape, jax.ShapeDtypeStruct((m, NKV_A), F32),
                     jax.ShapeDtypeStruct((NKV_A, m), BF16), tail_shape, tail_shape]
    else:
        out_specs = [qt_spec, pl.BlockSpec((NKV_A, tm), col_blk), pl.BlockSpec((NKV_A, tm), col_blk)]
        out_shape = [qt_shape, jax.ShapeDtypeStruct((NKV_A, m), F32), jax.ShapeDtypeStruct((NKV_A, m), F32)]
    return pl.pallas_call(
        functools.partial(_qkv_kernel, tail_rows=tail_rows, tiles_per_group=tiles_per_group),
        grid=(m // tm,),
        in_specs=in_specs,
        out_specs=out_specs,
        out_shape=out_shape,
        compiler_params=_params(1),
        name="norm_qkv",
    )(x, g, wt_bf16)


def _t5_bucket(dist):
    d = np.maximum(dist, 0)
    df = np.maximum(d, 1).astype(np.float32)
    large = MAX_EXACT + (np.log(df / np.float32(MAX_EXACT)) / np.float32(math.log(MAX_DISTANCE / MAX_EXACT))
                         * np.float32(N_BUCKETS - MAX_EXACT)).astype(np.int32)
    large = np.minimum(large, N_BUCKETS - 1)
    return np.where(d < MAX_EXACT, d, large)


HEADS_PER_SLAB = LANES // A_HEAD_DIM
N_KV_SLABS = NKV_A // LANES
HEADS_PER_KV_SLAB = HEADS_PER_SLAB * A_GROUP
TILE_COLS = HEADS_PER_KV_SLAB * WINDOW


def _bias_kernel(bucket_ref, rb_ref, out_ref):
    bucket = bucket_ref[...]
    has_prev = lax.broadcasted_iota(jnp.int32, bucket.shape, 0) >= WINDOW
    for h in range(A_HEADS):
        acc = jnp.full(bucket.shape, NEG, F32)
        for b in range(N_BUCKETS):
            acc = jnp.where(bucket == b, rb_ref[b, h], acc)
        acc = acc * LOG2E
        t, hh = divmod(h, HEADS_PER_KV_SLAB)
        out_ref[0, t, :, hh * WINDOW:(hh + 1) * WINDOW] = acc
        out_ref[1, t, :, hh * WINDOW:(hh + 1) * WINDOW] = jnp.where(has_prev, acc, NEG)


def _bias_table(rel_bias):
    kj = np.arange(2 * WINDOW, dtype=np.int32)[:, None]
    qi = np.arange(WINDOW, dtype=np.int32)[None, :]
    dist = qi + WINDOW - kj
    valid = (dist >= 0) & (dist < WINDOW)
    bucket = jnp.asarray(np.where(valid, _t5_bucket(dist), -1).astype(np.int32))
    return pl.pallas_call(
        _bias_kernel,
        in_specs=[_resident(), pl.BlockSpec(memory_space=pltpu.SMEM)],
        out_specs=_resident(),
        out_shape=jax.ShapeDtypeStruct((2, N_KV_SLABS, 2 * WINDOW, TILE_COLS), F32),
        name="t5_bias_table",
    )(bucket, rel_bias)


SWA_BLOCKS = 8
SWA_HEAD_ORDER = tuple(
    HEADS_PER_KV_SLAB * t + A_GROUP * half + i
    for t in range(N_KV_SLABS) for i in range(A_GROUP) for half in range(HEADS_PER_SLAB))


def _swa_prompt_kernel(qt_ref, kp_ref, kc_ref, vtp_ref, vtc_ref, bias0_ref, bias_ref, sink_ref, o_ref):
    k = jnp.concatenate([kp_ref[...], kc_ref[...]], axis=0).astype(BF16)
    vt = jnp.concatenate([vtp_ref[...], vtc_ref[...]], axis=1)
    top = lax.broadcasted_iota(jnp.int32, (LANES, WINDOW), 0) < A_HEAD_DIM
    zero = jnp.zeros((LANES, WINDOW), BF16)
    units = [(blk, t) for blk in range(SWA_BLOCKS) for t in range(N_KV_SLABS)]

    scores = []
    for blk, t in units:
        cols = slice(blk * WINDOW, (blk + 1) * WINDOW)
        qs = [qt_ref[(A_GROUP * t + i) * LANES:(A_GROUP * t + i + 1) * LANES, cols] for i in range(A_GROUP)]
        rhs = jnp.concatenate([jnp.where(top, x, zero) for x in qs] + [jnp.where(top, zero, x) for x in qs],
                              axis=1)
        kwin = k[blk * WINDOW:(blk + 2) * WINDOW, t * LANES:(t + 1) * LANES]
        scores.append(jnp.dot(kwin, rhs, preferred_element_type=F32))

    probs, sink_terms = [], []
    for (blk, t), s in zip(units, scores):
        s = s + (bias0_ref[t] if blk == 0 else bias_ref[t])
        sink = sink_ref[t] * LOG2E
        m = jnp.maximum(jnp.max(s, axis=0, keepdims=True), sink)
        sink_terms.append(jnp.exp2(sink - m))
        probs.append(jnp.exp2(s - m).astype(BF16))

    ones = jnp.ones((BF16_SUBLANES, 2 * WINDOW), BF16)
    outs = []
    for (blk, t), p in zip(units, probs):
        vwin = vt[t * LANES:(t + 1) * LANES, blk * WINDOW:(blk + 2) * WINDOW]
        outs.append(jnp.dot(jnp.concatenate([vwin, ones], axis=0), p, preferred_element_type=F32))

    for (blk, t), ot, sink_term in zip(units, outs, sink_terms):
        ot = ot[:LANES] * (1.0 / (ot[LANES:LANES + 1] + sink_term))
        for i in range(A_GROUP):
            first = ot[:, i * WINDOW:(i + 1) * WINDOW]
            second = ot[:, (A_GROUP + i) * WINDOW:(A_GROUP + i + 1) * WINDOW]
            o_ref[blk * WINDOW:(blk + 1) * WINDOW, (A_GROUP * t + i) * LANES:(A_GROUP * t + i + 1) * LANES] = (
                jnp.where(top, first, second).T.astype(BF16))


def _swa_prompt(qt, k, vt, bias, sinks, batch, seq):
    tq = SWA_BLOCKS * WINDOW
    steps = seq // tq
    blocks = seq // WINDOW
    sink_rows = jnp.repeat(sinks, WINDOW).reshape(N_KV_SLABS, 1, TILE_COLS)
    prev_blk = lambda b, j: b * blocks + jnp.maximum(SWA_BLOCKS * j - 1, 0)
    tile = (None, N_KV_SLABS, 2 * WINDOW, TILE_COLS)
    return pl.pallas_call(
        _swa_prompt_kernel,
        grid=(batch, steps),
        in_specs=[
            pl.BlockSpec((NQ_A, tq), lambda b, j: (0, b * steps + j)),
            pl.BlockSpec((WINDOW, NKV_A), lambda b, j: (prev_blk(b, j), 0)),
            pl.BlockSpec((tq, NKV_A), lambda b, j: (b * steps + j, 0)),
            pl.BlockSpec((NKV_A, WINDOW), lambda b, j: (0, prev_blk(b, j))),
            pl.BlockSpec((NKV_A, tq), lambda b, j: (0, b * steps + j)),
            pl.BlockSpec(tile, lambda b, j: (jnp.where(j == 0, 1, 0), 0, 0, 0)),
            _resident(),
            _resident(),
        ],
        out_specs=pl.BlockSpec((tq, NQ_A), lambda b, j: (b * steps + j, 0)),
        out_shape=jax.ShapeDtypeStruct((batch * seq, NQ_A), BF16),
        compiler_params=_params(2),
        name="swa_prompt_attn",
    )(qt, k, k, vt, vt, bias, bias[0], sink_rows)


def _swa_sample_kernel(q_ref, kbt_ref, vbt_ref, knt_ref, vnt_ref, bias_ref, sink_ref,
                       o_ref, kout_ref, vout_ref):
    sink_all = sink_ref[...] * LOG2E
    n_rows, _, wb = kbt_ref.shape
    newest = lax.broadcasted_iota(jnp.int32, (NKV_A, wb), 1) == wb - 1
    kb, vb = [], []
    for r in range(n_rows):
        kw = jnp.where(newest, knt_ref[:, r:r + 1], pltpu.roll(kbt_ref[r], wb - 1, 1))
        vw = jnp.where(newest, vnt_ref[:, r:r + 1], pltpu.roll(vbt_ref[r], wb - 1, 1))
        kout_ref[r] = kw
        vout_ref[r] = vw
        kb.append(kw.astype(BF16))
        vb.append(vw.astype(BF16))
    units = [(r, g) for r in range(n_rows) for g in range(A_KV_HEADS)]
    heads = lambda g: slice(g * A_GROUP, (g + 1) * A_GROUP)
    dims = lambda g: slice(g * A_HEAD_DIM, (g + 1) * A_HEAD_DIM)
    scores = [jnp.dot(q_ref[r, heads(g), :].astype(BF16), kb[r][dims(g), :], preferred_element_type=F32)
              for r, g in units]
    probs, inv = [], []
    for (r, g), s in zip(units, scores):
        s = s + bias_ref[heads(g), :]
        sink = sink_all[heads(g), :]
        m = jnp.maximum(jnp.max(s, axis=-1, keepdims=True), sink)
        e = jnp.exp2(s - m)
        inv.append(1.0 / (jnp.sum(e, axis=-1, keepdims=True) + jnp.exp2(sink - m)))
        probs.append(e.astype(BF16))
    outs = [lax.dot_general(p, vb[r][dims(g), :], NT_DIMS, preferred_element_type=F32)
            for (r, g), p in zip(units, probs)]
    for (r, g), pv, rcp in zip(units, outs, inv):
        o_ref[r, heads(g), :] = pv * rcp


def _swa_sample(q, kt_new, vt_new, kbuf_t, vbuf_t, bias_row, sinks):
    nb = kbuf_t.shape[0]
    return pl.pallas_call(
        _swa_sample_kernel,
        in_specs=[_resident()] * 7,
        out_specs=[_resident()] * 3,
        out_shape=[
            jax.ShapeDtypeStruct((nb, A_HEADS, A_HEAD_DIM), F32),
            jax.ShapeDtypeStruct(kbuf_t.shape, F32),
            jax.ShapeDtypeStruct(vbuf_t.shape, F32),
        ],
        compiler_params=pltpu.CompilerParams(vmem_limit_bytes=VMEM_LIMIT_BYTES),
        name="swa_sample_attn",
    )(q, kbuf_t, vbuf_t, kt_new, vt_new, bias_row, sinks.reshape(A_HEADS, 1))


FF_CHUNK = 1024


def _out_mlp_kernel(*refs, pair_major, final, layer):
    if final:
        o_ref, x_ref, wo_ref, g_ref, wup_hbm, wdn_hbm, gf_ref, out_ref, wup_ref, wdn_ref, sem = refs
    else:
        o_ref, x_ref, wo_ref, g_ref, wup_hbm, wdn_hbm, out_ref, wup_ref, wdn_ref, sem = refs

    @pl.when(pl.program_id(0) == 0)
    def _():
        copies = [pltpu.make_async_copy(wup_hbm.at[layer], wup_ref, sem.at[0]),
                  pltpu.make_async_copy(wdn_hbm.at[layer], wdn_ref, sem.at[1])]
        for cp in copies:
            cp.start()
        for cp in copies:
            cp.wait()

    if pair_major:
        o = jnp.concatenate([o_ref[j] for j in range(o_ref.shape[0])], axis=-1)
    else:
        o = o_ref[...]
    x1 = x_ref[...] + jnp.dot(o, wo_ref[...], preferred_element_type=F32)
    h = _rms(x1, g_ref[...]).astype(BF16)
    acc = x1
    for c in range(D_FF // FF_CHUNK):
        sl = slice(c * FF_CHUNK, (c + 1) * FF_CHUNK)
        u = jnp.dot(h, wup_ref[:, sl], preferred_element_type=F32)
        u = jnp.square(jnp.maximum(u, 0.0)).astype(BF16)
        acc = acc + jnp.dot(u, wdn_ref[sl, :], preferred_element_type=F32)
    if final:
        acc = _rms(acc, gf_ref[...])
    out_ref[...] = acc


def _out_mlp(o, x, wo, g, wup_all, wdn_all, layer, gf, tm, pair_major):
    m = x.shape[0]
    final = gf is not None
    if pair_major:
        tiles_per_group = o.shape[2] // tm
        o_spec = pl.BlockSpec((None, o.shape[1], tm, LANES),
                              lambda i: (i // tiles_per_group, 0, i % tiles_per_group, 0))
    else:
        o_spec = pl.BlockSpec((tm, D_MODEL), lambda i: (i, 0))
    row_spec = pl.BlockSpec((tm, D_MODEL), lambda i: (i, 0))
    in_hbm = pl.BlockSpec(memory_space=pl.ANY)
    in_specs = [o_spec, row_spec, _resident(), _resident(), in_hbm, in_hbm]
    args = [o, x, wo, g, wup_all, wdn_all]
    if final:
        in_specs.append(_resident())
        args.append(gf)
    return pl.pallas_call(
        functools.partial(_out_mlp_kernel, pair_major=pair_major, final=final, layer=layer),
        grid=(m // tm,),
        in_specs=in_specs,
        out_specs=row_spec,
        out_shape=jax.ShapeDtypeStruct((m, D_MODEL), F32),
        scratch_shapes=[
            pltpu.VMEM((D_MODEL, D_FF), BF16),
            pltpu.VMEM((D_FF, D_MODEL), BF16),
            pltpu.SemaphoreType.DMA((2,)),
        ],
        compiler_params=_params(1),
        name="out_proj_mlp",
    )(*args)


N_PAIRS = B_HEADS // 2
VT_ROWS = V_DIM + BF16_SUBLANES
ROPE_LANE0 = NOPE_DIM
C_IN_EXT = Q_LORA + KV_LORA + LANES


def _rope_slab(t, c, s1, s2):
    return t * c + pltpu.roll(t, HALF_ROPE, 1) * s1 + pltpu.roll(t, LANES - HALF_ROPE, 1) * s2


def _mla_proj_kernel(x_ref, g_ref, win_ref, qn_ref, kvn_ref, wqb_ref, wk_ref, wvt_ref,
                     c_ref, s1_ref, s2_ref,
                     qcat_ref, kcat_ref, vt_ref, lat_ref, krt_ref, *, q_scale):
    h = _rms(x_ref[...], g_ref[...]).astype(BF16)
    c = jnp.dot(h, win_ref[...], preferred_element_type=F32)
    cq = _rms(c[:, :Q_LORA], qn_ref[...]).astype(BF16)
    lat = _rms(c[:, Q_LORA:Q_LORA + KV_LORA], kvn_ref[...])
    lat_ref[...] = lat
    cos, s1, s2 = c_ref[...], s1_ref[...], s2_ref[...]
    krp = _rope_slab(c[:, Q_LORA + KV_LORA:], cos, s1, s2)
    krt_ref[...] = krp.T[ROPE_LANE0:ROPE_LANE0 + ROPE_DIM, :]
    q = jnp.dot(cq, wqb_ref[...], preferred_element_type=F32)
    latb = lat.astype(BF16)
    kn = jnp.dot(latb, wk_ref[...], preferred_element_type=F32)
    vt = lax.dot_general(wvt_ref[...], latb, NT_DIMS, preferred_element_type=F32)
    for hh in range(B_HEADS):
        sl = slice(hh * LANES, (hh + 1) * LANES)
        qh = _rope_slab(q[:, sl], cos, s1, s2)
        if q_scale != 1.0:
            qh = qh * q_scale
        qcat_ref[hh] = qh.astype(BF16)
        kcat_ref[hh] = (kn[:, sl] + krp).astype(BF16)
        vt_ref[hh, :V_DIM, :] = vt[hh * V_DIM:(hh + 1) * V_DIM, :].astype(BF16)
        vt_ref[hh, V_DIM:, :] = jnp.ones((VT_ROWS - V_DIM, vt.shape[1]), BF16)


def _mla_proj(x, g, win, qn, kvn, wqb, wk, wvt, tabs, groups, rows, tm, tab_blocks, q_scale):
    m = x.shape[0]
    tiles = rows // tm
    head_map = lambda i: (i // tiles, 0, i % tiles, 0)
    tab_spec = pl.BlockSpec((tm, LANES), lambda i: (i % tab_blocks, 0))
    return pl.pallas_call(
        functools.partial(_mla_proj_kernel, q_scale=q_scale),
        grid=(m // tm,),
        in_specs=[pl.BlockSpec((tm, D_MODEL), lambda i: (i, 0))] + [_resident()] * 7 + [tab_spec] * 3,
        out_specs=[
            pl.BlockSpec((None, B_HEADS, tm, LANES), head_map),
            pl.BlockSpec((None, B_HEADS, tm, LANES), head_map),
            pl.BlockSpec((None, B_HEADS, VT_ROWS, tm), lambda i: (i // tiles, 0, 0, i % tiles)),
            pl.BlockSpec((tm, KV_LORA), lambda i: (i, 0)),
            pl.BlockSpec((None, ROPE_DIM, tm), lambda i: (i // tiles, 0, i % tiles)),
        ],
        out_shape=[
            jax.ShapeDtypeStruct((groups, B_HEADS, rows, LANES), BF16),
            jax.ShapeDtypeStruct((groups, B_HEADS, rows, LANES), BF16),
            jax.ShapeDtypeStruct((groups, B_HEADS, VT_ROWS, rows), BF16),
            jax.ShapeDtypeStruct((m, KV_LORA), F32),
            jax.ShapeDtypeStruct((groups, ROPE_DIM, rows), F32),
        ],
        compiler_params=_params(1),
        name="mla_proj",
    )(x, g, win, qn, kvn, wqb, wk, wvt, *tabs)


def _rope_tables(pos):
    pos = np.asarray(pos, np.float64)
    inv = ROPE_THETA ** (-np.arange(0, ROPE_DIM, 2, dtype=np.float64) / ROPE_DIM)
    ang = pos[:, None] * inv[None, :]
    cos, sin = np.cos(ang).astype(np.float32), np.sin(ang).astype(np.float32)
    s = pos.shape[0]
    ones = np.ones((s, ROPE_LANE0), np.float32)
    z16 = np.zeros((s, HALF_ROPE), np.float32)
    z64 = np.zeros((s, ROPE_LANE0), np.float32)
    z32 = np.zeros((s, LANES - ROPE_LANE0 - ROPE_DIM), np.float32)
    c = np.concatenate([ones, cos, cos, z32], axis=1)
    s1 = np.concatenate([z64, z16, sin, z32], axis=1)
    s2 = np.concatenate([z64, -sin, z16, z32], axis=1)
    return jnp.asarray(c), jnp.asarray(s1), jnp.asarray(s2)


HEADS_PER_BODY = 8
REDUCE_ROWS = 64


def _col_reduce(x, op):
    n, w = x.shape
    part = op(x.reshape(n // REDUCE_ROWS, REDUCE_ROWS, w), axis=0)
    return op(part, axis=0, keepdims=True)


def _mla_attn_kernel(q_ref, k_ref, vt_ref, o_ref, *, tq, n_tiles):
    qi = pl.program_id(1)
    key = lax.broadcasted_iota(jnp.int32, (tq, tq), 0)
    qry = lax.broadcasted_iota(jnp.int32, (tq, tq), 1)
    causal = key <= qry

    def scores(h, n_keys):
        k = k_ref[h, pl.ds(0, n_keys), :]
        return lax.dot_general(k, q_ref[h], NT_DIMS, preferred_element_type=F32)

    def attend(st, h, n_keys):
        diag = jnp.where(causal, st[n_keys - tq:, :], NEG)
        if n_keys > tq:
            full = st[:n_keys - tq, :]
            m = jnp.maximum(_col_reduce(full, jnp.max), _col_reduce(diag, jnp.max))
            p = jnp.concatenate([jnp.exp2(full - m), jnp.exp2(diag - m)], axis=0)
        else:
            m = _col_reduce(diag, jnp.max)
            p = jnp.exp2(diag - m)
        ot = jnp.dot(vt_ref[h, :, pl.ds(0, n_keys)], p.astype(BF16), preferred_element_type=F32)
        return ot[:V_DIM] / ot[V_DIM:V_DIM + 1]

    for c in range(n_tiles):
        @pl.when(qi == c)
        def _(c=c):
            n_keys = (c + 1) * tq

            def body(jj, carry):
                heads = [HEADS_PER_BODY * jj + u for u in range(HEADS_PER_BODY)]
                sts = [scores(h, n_keys) for h in heads]
                outs = [attend(st, h, n_keys) for h, st in zip(heads, sts)]
                for u in range(0, HEADS_PER_BODY, 2):
                    ot = jnp.concatenate([outs[u], outs[u + 1]], axis=0)
                    o_ref[(HEADS_PER_BODY // 2) * jj + u // 2] = ot.T.astype(BF16)
                return carry

            lax.fori_loop(0, B_HEADS // HEADS_PER_BODY, body, 0)


def _mla_attn(qcat, kcat, vt, tq):
    batch, _, seq, _ = qcat.shape
    n_tiles = seq // tq
    return pl.pallas_call(
        functools.partial(_mla_attn_kernel, tq=tq, n_tiles=n_tiles),
        grid=(batch, n_tiles),
        in_specs=[
            pl.BlockSpec((None, B_HEADS, tq, LANES), lambda b, i: (b, 0, i, 0)),
            pl.BlockSpec((None, B_HEADS, seq, LANES), lambda b, i: (b, 0, 0, 0)),
            pl.BlockSpec((None, B_HEADS, VT_ROWS, seq), lambda b, i: (b, 0, 0, 0)),
        ],
        out_specs=pl.BlockSpec((None, N_PAIRS, tq, LANES), lambda b, i: (b, 0, i, 0)),
        out_shape=jax.ShapeDtypeStruct((batch, N_PAIRS, seq, LANES), BF16),
        compiler_params=_params(2),
        name="mla_prompt_attn",
    )(qcat, kcat, vt)


def _absorb_kernel(qcat_ref, wuk_ref, qlat_ref):
    for h in range(B_HEADS):
        qlat_ref[h] = jnp.dot(qcat_ref[h], wuk_ref[h], preferred_element_type=F32).astype(BF16)


def _absorb(qcat, wuk_ext):
    nb = qcat.shape[1]
    return pl.pallas_call(
        _absorb_kernel,
        in_specs=[_resident(), _resident()],
        out_specs=_resident(),
        out_shape=jax.ShapeDtypeStruct((B_HEADS, nb, KV_LORA), BF16),
        name="mla_absorb_q",
    )(qcat, wuk_ext)


def _decode_kernel(pt_ref, qlat_ref, qr_ref, latnew_ref, krnew_ref, lat_hbm, krt_hbm, o_ref,
                   latbuf, krbuf, sem, m_sc, l_sc, acc_sc,
                   *, pages_per_step, steps_per_batch, n_chains, scale):
    b = pl.program_id(0)
    c = pl.program_id(1)
    step = b * steps_per_batch + c
    n_steps = pl.num_programs(0) * steps_per_batch
    last_step = step + 1 == n_steps
    n_slots = latbuf.shape[0]
    slot = step % n_slots
    page_size = latbuf.shape[2]

    def start_chunk(chunk, sl):
        for p in range(pages_per_step):
            page = pt_ref[chunk * pages_per_step + p]
            pltpu.make_async_copy(lat_hbm.at[page], latbuf.at[sl, p], sem.at[0, sl]).start()
            pltpu.make_async_copy(krt_hbm.at[page], krbuf.at[sl, p], sem.at[1, sl]).start()

    def wait_chunk(sl):
        pltpu.make_async_copy(lat_hbm.at[pl.ds(0, pages_per_step)], latbuf.at[sl], sem.at[0, sl]).wait()
        pltpu.make_async_copy(krt_hbm.at[pl.ds(0, pages_per_step)], krbuf.at[sl], sem.at[1, sl]).wait()

    @pl.when(step == 0)
    def _():
        for k in range(n_slots - 1):
            start_chunk(k, k)

    ahead = n_slots - 1
    start_chunk(jnp.minimum(step + ahead, n_steps - 1), (step + ahead) % n_slots)

    qlat = qlat_ref[...]
    qr = qr_ref[...]

    @pl.when(c == 0)
    def _():
        ln = latnew_ref[...].astype(BF16).astype(F32)
        kn = krnew_ref[...].astype(BF16).astype(F32)
        s_new = (jnp.sum(qlat.astype(F32) * ln, axis=-1, keepdims=True)
                 + jnp.sum(qr.astype(F32) * kn, axis=-1, keepdims=True)) * scale
        m_sc[...] = jnp.full(m_sc.shape, NEG, F32)
        l_sc[...] = jnp.zeros(l_sc.shape, F32)
        acc_sc[...] = jnp.zeros(acc_sc.shape, F32)
        m_sc[0] = s_new
        l_sc[0] = jnp.ones_like(s_new)
        acc_sc[0] = jnp.broadcast_to(ln, acc_sc.shape[1:])

    wait_chunk(slot)

    ppc = pages_per_step // n_chains
    chains = range(n_chains)
    s_rope = [jnp.concatenate(
        [jnp.dot(qr, krbuf[slot, i * ppc + p].astype(BF16), preferred_element_type=F32)
         for p in range(ppc)], axis=1) for i in chains]
    lat = [latbuf[slot, pl.ds(i * ppc, ppc)].reshape(ppc * page_size, KV_LORA).astype(BF16) for i in chains]
    s = [(lax.dot_general(qlat, lat[i], NT_DIMS, preferred_element_type=F32) + s_rope[i]) * scale
         for i in chains]
    m_old = [m_sc[i] for i in chains]
    m_new = [jnp.maximum(m_old[i], jnp.max(s[i], axis=-1, keepdims=True)) for i in chains]
    alpha = [jnp.exp(m_old[i] - m_new[i]) for i in chains]
    p = [jnp.exp(s[i] - m_new[i]) for i in chains]
    pv = [jnp.dot(p[i].astype(BF16), lat[i], preferred_element_type=F32) for i in chains]
    for i in chains:
        l_sc[i] = alpha[i] * l_sc[i] + jnp.sum(p[i], axis=-1, keepdims=True)
        acc_sc[i] = alpha[i] * acc_sc[i] + pv[i]
        m_sc[i] = m_new[i]

    @pl.when(c == steps_per_batch - 1)
    def _():
        m = m_sc[0]
        for i in range(1, n_chains):
            m = jnp.maximum(m, m_sc[i])
        l = jnp.zeros_like(m)
        acc = jnp.zeros(acc_sc.shape[1:], F32)
        for i in range(n_chains):
            w = jnp.exp(m_sc[i] - m)
            l = l + w * l_sc[i]
            acc = acc + w * acc_sc[i]
        o_ref[...] = acc / l

    @pl.when(last_step)
    def _():
        for k in range(1, n_slots):
            wait_chunk((step + k) % n_slots)


def _mla_decode(page_table, qlat, qr, lat_new, kr_new, lat_pool, krt_pool, pages_per_step, n_chains):
    nb, n_pages = page_table.shape
    page_size = lat_pool.shape[1]
    steps_per_batch = n_pages // pages_per_step
    scale = (NOPE_DIM + ROPE_DIM) ** -0.5
    per_b = lambda b, c, pt: (b, 0, 0)
    grid_spec = pltpu.PrefetchScalarGridSpec(
        num_scalar_prefetch=1,
        grid=(nb, steps_per_batch),
        in_specs=[
            pl.BlockSpec((None, B_HEADS, KV_LORA), per_b),
            pl.BlockSpec((None, B_HEADS, ROPE_DIM), per_b),
            pl.BlockSpec((None, 1, KV_LORA), per_b),
            pl.BlockSpec((None, 1, ROPE_DIM), per_b),
            pl.BlockSpec(memory_space=pl.ANY),
            pl.BlockSpec(memory_space=pl.ANY),
        ],
        out_specs=pl.BlockSpec((None, B_HEADS, KV_LORA), per_b),
        scratch_shapes=[
            pltpu.VMEM((DECODE_SLOTS, pages_per_step, page_size, KV_LORA), F32),
            pltpu.VMEM((DECODE_SLOTS, pages_per_step, ROPE_DIM, page_size), F32),
            pltpu.SemaphoreType.DMA((2, DECODE_SLOTS)),
            pltpu.VMEM((n_chains, B_HEADS, 1), F32),
            pltpu.VMEM((n_chains, B_HEADS, 1), F32),
            pltpu.VMEM((n_chains, B_HEADS, KV_LORA), F32),
        ],
    )
    return pl.pallas_call(
        functools.partial(_decode_kernel, pages_per_step=pages_per_step,
                          steps_per_batch=steps_per_batch, n_chains=n_chains, scale=scale),
        grid_spec=grid_spec,
        out_shape=jax.ShapeDtypeStruct((nb, B_HEADS, KV_LORA), F32),
        compiler_params=_params(2),
        name="mla_decode_attn",
    )(page_table.reshape(-1), qlat, qr, lat_new, kr_new, lat_pool, krt_pool)


def _uv_kernel(olat_ref, wuv_ref, o_ref):
    for h in range(B_HEADS):
        o_ref[:, h * V_DIM:(h + 1) * V_DIM] = jnp.dot(
            olat_ref[h].astype(BF16), wuv_ref[h], preferred_element_type=F32).astype(BF16)


def _uv(olat_hm, wuv):
    nb = olat_hm.shape[1]
    return pl.pallas_call(
        _uv_kernel,
        in_specs=[_resident(), _resident()],
        out_specs=_resident(),
        out_shape=jax.ShapeDtypeStruct((nb, B_HEADS * V_DIM), BF16),
        name="mla_value_up",
    )(olat_hm, wuv)


QKV_TM = 1024
PROMPT_TM = 1024
MLP_TM = 1024
MLA_TQ = 256
DECODE_PAGES = 32
DECODE_CHAINS = 4
DECODE_SLOTS = 5


def kernel(x_prompt, x_sample, cache_a_k, cache_a_v, cache_b_latent, cache_b_krope, page_table,
           rel_bias, norm_mix, norm_mlp, norm_final, a_w_qkv, a_w_o, a_sinks,
           b_w_in, b_q_norm, b_kv_norm, b_w_q_b, b_w_kv_b, b_w_o, mlp_w_up, mlp_w_down):
    batch, seq, _ = x_prompt.shape
    nb_s = x_sample.shape[0]
    past = page_table.shape[1] * cache_b_latent.shape[2]
    wb = cache_a_k.shape[2]

    xp = x_prompt.reshape(batch * seq, D_MODEL)
    xs = x_sample.reshape(nb_s, D_MODEL)
    row = lambda t: t.reshape(1, -1)

    order = jnp.array(SWA_HEAD_ORDER)
    inverse = jnp.argsort(order)
    w_q = a_w_qkv[0][:, :NQ_A].reshape(D_MODEL, A_HEADS, A_HEAD_DIM)[:, order].reshape(D_MODEL, NQ_A)
    w_qkvt = jnp.concatenate([w_q, a_w_qkv[0][:, NQ_A:]], axis=1).T.astype(BF16)
    w_o_a = a_w_o[0].reshape(A_HEADS, A_HEAD_DIM, D_MODEL)[order].reshape(NQ_A, D_MODEL).astype(BF16)
    w_up, w_dn = mlp_w_up.astype(BF16), mlp_w_down.astype(BF16)
    bias = _bias_table(rel_bias)
    sinks = a_sinks[0]
    wbp = min(WINDOW, seq)

    qtp, kp, vtp, kt_tail, vt_tail = _norm_qkv(xp, row(norm_mix[0]), w_qkvt, QKV_TM, batch, wbp)
    op = _swa_prompt(qtp, kp, vtp, bias, sinks, batch, seq)
    xp = _out_mlp(op, xp, w_o_a, row(norm_mlp[0]), w_up, w_dn, 0, None, MLP_TM, False)

    qts, kts, vts = _norm_qkv(xs, row(norm_mix[0]), w_qkvt, nb_s)
    qs3 = qts.T.astype(F32).reshape(nb_s, A_HEADS, A_HEAD_DIM)[:, inverse]
    bias_row = jnp.transpose(
        bias[0].reshape(N_KV_SLABS, 2 * WINDOW, HEADS_PER_KV_SLAB, WINDOW)[:, WINDOW:, :, WINDOW - 1],
        (0, 2, 1)).reshape(A_HEADS, WINDOW)
    to_t = lambda c: jnp.transpose(c[0], (0, 2, 3, 1)).reshape(nb_s, NKV_A, wb)
    from_t = lambda t, n, w: jnp.transpose(t.reshape(n, A_KV_HEADS, A_HEAD_DIM, w), (0, 3, 1, 2))[None]
    os3, a_k_st, a_v_st = _swa_sample(qs3, kts, vts, to_t(cache_a_k), to_t(cache_a_v), bias_row, sinks)
    os_ = os3[:, order].reshape(nb_s, NQ_A).astype(BF16)
    xs = _out_mlp(os_, xs, w_o_a, row(norm_mlp[0]), w_up, w_dn, 0, None, nb_s, False)

    w_in = b_w_in[0]
    zeros = lambda n: jnp.zeros((D_MODEL, n), F32)
    w_in_ext = jnp.concatenate([w_in[:, :Q_LORA + KV_LORA], zeros(ROPE_LANE0),
                                w_in[:, Q_LORA + KV_LORA:], zeros(LANES - ROPE_LANE0 - ROPE_DIM)],
                               axis=1).astype(BF16)
    qk_dim = NOPE_DIM + ROPE_DIM
    w_qb = jnp.pad(b_w_q_b[0].reshape(Q_LORA, B_HEADS, qk_dim),
                   ((0, 0), (0, 0), (0, LANES - qk_dim))).reshape(Q_LORA, B_HEADS * LANES).astype(BF16)
    w_kvb = b_w_kv_b[0]
    w_uk, w_uv = w_kvb[..., :NOPE_DIM], w_kvb[..., NOPE_DIM:]
    w_k = jnp.pad(w_uk, ((0, 0), (0, 0), (0, LANES - NOPE_DIM))).reshape(KV_LORA, B_HEADS * LANES).astype(BF16)
    w_vt = w_uv.reshape(KV_LORA, B_HEADS * V_DIM).T.astype(BF16)
    w_o_b = b_w_o[0].astype(BF16)
    proj_w = (row(norm_mix[1]), w_in_ext, row(b_q_norm[0]), row(b_kv_norm[0]), w_qb, w_k, w_vt)

    q_scale_p = (NOPE_DIM + ROPE_DIM) ** -0.5 * LOG2E
    tabs_p = _rope_tables(np.arange(seq))
    qcat, kcat, vt, lat_p, krt_p = _mla_proj(xp, *proj_w, tabs_p, batch, seq, PROMPT_TM, seq // PROMPT_TM,
                                            q_scale_p)
    o_pm = _mla_attn(qcat, kcat, vt, MLA_TQ)
    yp = _out_mlp(o_pm, xp, w_o_b, row(norm_mlp[1]), w_up, w_dn, 1, row(norm_final), MLP_TM, True)

    tabs_s = _rope_tables(np.full((nb_s,), past))
    qcat_s, _, _, lat_s, krt_s = _mla_proj(xs, *proj_w, tabs_s, 1, nb_s, nb_s, 1, 1.0)
    w_uk_ext = jnp.pad(jnp.transpose(w_uk, (1, 2, 0)), ((0, 0), (0, LANES - NOPE_DIM), (0, 0))).astype(BF16)
    qlat = jnp.transpose(_absorb(qcat_s[0], w_uk_ext), (1, 0, 2))
    qr = jnp.transpose(qcat_s[0, :, :, ROPE_LANE0:ROPE_LANE0 + ROPE_DIM], (1, 0, 2))
    kr_s = krt_s[0].T
    krt_pool = jnp.swapaxes(cache_b_krope[0], 1, 2)
    olat = _mla_decode(page_table, qlat, qr, lat_s.reshape(nb_s, 1, KV_LORA), kr_s.reshape(nb_s, 1, ROPE_DIM),
                       cache_b_latent[0], krt_pool, DECODE_PAGES, DECODE_CHAINS)
    o_s = _uv(jnp.transpose(olat, (1, 0, 2)), jnp.transpose(w_uv, (1, 0, 2)).astype(BF16))
    ys = _out_mlp(o_s, xs, w_o_b, row(norm_mlp[1]), w_up, w_dn, 1, row(norm_final), nb_s, False)

    k4 = from_t(kt_tail, batch, wbp)
    v4 = from_t(vt_tail, batch, wbp)
    kr_p = jnp.swapaxes(krt_p, 1, 2)
    return (
        yp.reshape(batch, seq, D_MODEL),
        ys.reshape(nb_s, 1, D_MODEL),
        k4, v4,
        lat_p.reshape(1, batch, seq, KV_LORA),
        kr_p.reshape(1, batch, seq, ROPE_DIM),
        from_t(a_k_st, nb_s, wb),
        from_t(a_v_st, nb_s, wb),
        lat_s.reshape(1, nb_s, 1, KV_LORA),
        kr_s.reshape(1, nb_s, 1, ROPE_DIM),
    )
```
